```python
import math
import jax, jax.numpy as jnp
from jax import lax
import numpy as np

D_MODEL = 2048
BATCH = 2
SEQ = 4096
DEPTH = 4

N_MIXERS = 3
HEAD_DIM = 64
N_HEADS = D_MODEL // HEAD_DIM
N_KV = N_HEADS // 8
GQA_GROUP = N_HEADS // N_KV
Q_DIM = N_HEADS * HEAD_DIM
KV_DIM = N_KV * HEAD_DIM
ATTN_SCALE = HEAD_DIM ** -0.5
BLOCK = 128
SWA_WINDOW = 128
DIL_PATTERNS = ((128, 1), (512, 4), (2048, 16))
N_DIL = len(DIL_PATTERNS)
IDX_HEADS = 16
IDX_DIM = 64
TOPK_MAX = 256
D_FF = 256 * math.ceil(8 * D_MODEL / (3 * 256))
DEEPNORM_ALPHA = (2 * DEPTH) ** 0.25
DEEPNORM_BETA = (8 * DEPTH) ** -0.25
LN_EPS = 1e-5

A_IN = Q_DIM + 2 * KV_DIM
B_IN = N_DIL * (Q_DIM + 2 * KV_DIM)
C_SPLITS = (Q_DIM, Q_DIM + KV_DIM, Q_DIM + 2 * KV_DIM,
            Q_DIM + 2 * KV_DIM + IDX_HEADS * IDX_DIM,
            Q_DIM + 2 * KV_DIM + IDX_HEADS * IDX_DIM + IDX_DIM)
C_IN = C_SPLITS[-1] + IDX_HEADS

kernel_name = 'hybrid_swa_dilated_dsa_deepnorm'


def _n_layers_of(kind):
    return len(range(kind, DEPTH, N_MIXERS))


def alibi_slopes(n):
    return jnp.asarray(2.0 ** (-8.0 * np.arange(1, n + 1) / n), dtype=jnp.float32)


def layer_norm(x, g, b):
    xf = x.astype(jnp.float32)
    mu = xf.mean(-1, keepdims=True)
    var = jnp.square(xf - mu).mean(-1, keepdims=True)
    return ((xf - mu) * lax.rsqrt(var + LN_EPS) * g + b).astype(x.dtype)


def swiglu_ffn(x, w_gate_up, w_down):
    g, u = jnp.split(x @ w_gate_up, 2, axis=-1)
    return (jax.nn.silu(g) * u) @ w_down


def swa_sink_attention(x, w_in, sinks, w_out):
    B, S, _ = x.shape
    nb = S // BLOCK
    q, k, v = jnp.split(x @ w_in, [Q_DIM, Q_DIM + KV_DIM], axis=-1)
    q = q.reshape(B, nb, BLOCK, N_KV, GQA_GROUP, HEAD_DIM)
    k = k.reshape(B, nb, BLOCK, N_KV, HEAD_DIM)
    v = v.reshape(B, nb, BLOCK, N_KV, HEAD_DIM)

    def with_prev(t):
        prev = jnp.concatenate([jnp.zeros_like(t[:, :1]), t[:, :-1]], axis=1)
        return jnp.concatenate([prev, t], axis=2)

    kb, vb = with_prev(k), with_prev(v)
    s = jnp.einsum('bnqkgd,bnskd->bnkgqs', q, kb).astype(jnp.float32) * ATTN_SCALE
    qpos = jnp.arange(BLOCK)[:, None] + BLOCK
    kpos = jnp.arange(2 * BLOCK)[None, :]
    dist = qpos - kpos
    in_window = (dist >= 0) & (dist < SWA_WINDOW)
    has_prev = (jnp.arange(nb) > 0)[:, None, None] | (kpos >= BLOCK)[None]
    valid = in_window[None] & has_prev
    slopes = alibi_slopes(N_HEADS).reshape(N_KV, GQA_GROUP)[:, :, None, None]
    s = jnp.where(valid[None, :, None, None], s - slopes * dist, -jnp.inf)
    sink = sinks.astype(jnp.float32).reshape(N_KV, GQA_GROUP)[:, :, None, None]
    m = jnp.maximum(s.max(-1, keepdims=True), sink)
    p = jnp.exp(s - m)
    p = p / (p.sum(-1, keepdims=True) + jnp.exp(sink - m))
    o = jnp.einsum('bnkgqs,bnskd->bnqkgd', p.astype(vb.dtype), vb)
    return o.reshape(B, S, Q_DIM) @ w_out


def dilated_attention(x, w_in, w_out):
    B, S, _ = x.shape
    nb = S // BLOCK
    h = (x @ w_in).reshape(B, S, N_DIL, Q_DIM + 2 * KV_DIM)
    q = h[..., :Q_DIM].reshape(B, S, N_DIL, N_KV, GQA_GROUP, HEAD_DIM)
    k = h[..., Q_DIM:Q_DIM + KV_DIM].reshape(B, S, N_DIL, N_KV, HEAD_DIM)
    v = h[..., Q_DIM + KV_DIM:].reshape(B, S, N_DIL, N_KV, HEAD_DIM)
    k_groups = [k[:, :, g] for g in range(N_DIL)]
    v_groups = [v[:, :, g] for g in range(N_DIL)]
    slopes = alibi_slopes(N_HEADS).reshape(N_KV, GQA_GROUP)[:, :, None, None]

    def block(i):
        t = i * BLOCK + jnp.arange(BLOCK)
        qb = lax.dynamic_slice_in_dim(q, i * BLOCK, BLOCK, axis=1)
        outs, lses = [], []
        for g, (window, dilation) in enumerate(DIL_PATTERNS):
            dist = jnp.arange(window // dilation + 1) * dilation
            pos = t[:, None] - dist[None, :]
            valid = pos >= 0
            pos = jnp.maximum(pos, 0)
            kg = k_groups[g][:, pos]
            vg = v_groups[g][:, pos]
            s = jnp.einsum('bqkgd,bqjkd->bkgqj', qb[:, :, g], kg).astype(jnp.float32) * ATTN_SCALE
            s = jnp.where(valid, s - slopes * dist, -jnp.inf)
            lse = jax.nn.logsumexp(s, axis=-1)
            p = jnp.exp(s - lse[..., None])
            outs.append(jnp.einsum('bkgqj,bqjkd->bqkgd', p.astype(vg.dtype), vg))
            lses.append(lse.transpose(0, 3, 1, 2))
        wts = jax.nn.softmax(jnp.stack(lses), axis=0)
        o = jnp.sum(wts[..., None] * jnp.stack(outs).astype(jnp.float32), axis=0)
        return o.reshape(B, BLOCK, Q_DIM).astype(x.dtype)

    o = lax.map(block, jnp.arange(nb))
    o = jnp.moveaxis(o, 0, 1).reshape(B, S, Q_DIM)
    return o @ w_out


def dsa_sparse_attention(x, w_in, w_out):
    B, S, _ = x.shape
    nb = S // BLOCK
    top_k = min(TOPK_MAX, S // 4)
    q, k, v, qi, ki, wi = jnp.split(x @ w_in, list(C_SPLITS), axis=-1)
    q = q.reshape(B, S, N_KV, GQA_GROUP, HEAD_DIM)
    k = k.reshape(B, S, N_KV, HEAD_DIM)
    v = v.reshape(B, S, N_KV, HEAD_DIM)
    qi = qi.reshape(B, S, IDX_HEADS, IDX_DIM)
    wi = wi.astype(jnp.float32) * IDX_HEADS ** -0.5
    kpos = jnp.arange(S)
    slopes = alibi_slopes(N_HEADS).reshape(N_KV, GQA_GROUP)[None, :, :, None, None]
    gather = jax.vmap(lambda tab, idx: tab[idx])

    def block(i):
        t = i * BLOCK + jnp.arange(BLOCK)
        qib = lax.dynamic_slice_in_dim(qi, i * BLOCK, BLOCK, axis=1)
        wib = lax.dynamic_slice_in_dim(wi, i * BLOCK, BLOCK, axis=1)
        rel = jax.nn.relu(jnp.einsum('bqhd,bsd->bqhs', qib, ki).astype(jnp.float32) * IDX_DIM ** -0.5)
        score = jnp.einsum('bqhs,bqh->bqs', rel, wib)
        score = jnp.where(kpos[None, None, :] <= t[None, :, None], score, -jnp.inf)
        _, idx = lax.top_k(score, top_k)
        valid = idx <= t[None, :, None]
        kg = gather(k, idx)
        vg = gather(v, idx)
        qb = lax.dynamic_slice_in_dim(q, i * BLOCK, BLOCK, axis=1)
        s = jnp.einsum('bqkgd,bqjkd->bkgqj', qb, kg).astype(jnp.float32) * ATTN_SCALE
        dist = (t[None, :, None] - idx)[:, None, None]
        s = jnp.where(valid[:, None, None], s - slopes * dist, -jnp.inf)
        p = jax.nn.softmax(s, axis=-1)
        o = jnp.einsum('bkgqj,bqjkd->bqkgd', p.astype(vg.dtype), vg)
        return o.reshape(B, BLOCK, Q_DIM)

    o = lax.map(block, jnp.arange(nb))
    o = jnp.moveaxis(o, 0, 1).reshape(B, S, Q_DIM)
    return o @ w_out


def setup_inputs(seed: int = 0) -> dict:
    key = jax.random.key(seed)
    ks = jax.random.split(key, 12)
    n_a, n_b, n_c = (_n_layers_of(m) for m in range(N_MIXERS))

    def nrm(k, shape, scale):
        return jax.random.normal(k, shape, jnp.float32) * scale

    out_scale = Q_DIM ** -0.5 * DEEPNORM_BETA
    return {
        'x': nrm(ks[0], (BATCH, SEQ, D_MODEL), 1.0),
        'a_w_in': nrm(ks[1], (n_a, D_MODEL, A_IN), D_MODEL ** -0.5),
        'a_sinks': nrm(ks[2], (n_a, N_HEADS), 0.5),
        'a_w_out': nrm(ks[3], (n_a, Q_DIM, D_MODEL), out_scale),
        'b_w_in': nrm(ks[4], (n_b, D_MODEL, B_IN), D_MODEL ** -0.5),
        'b_w_out': nrm(ks[5], (n_b, Q_DIM, D_MODEL), out_scale),
        'c_w_in': nrm(ks[6], (n_c, D_MODEL, C_IN), D_MODEL ** -0.5),
        'c_w_out': nrm(ks[7], (n_c, Q_DIM, D_MODEL), out_scale),
        'ln_g': 1.0 + nrm(ks[8], (DEPTH, 2, D_MODEL), 0.02),
        'ln_b': nrm(ks[9], (DEPTH, 2, D_MODEL), 0.02),
        'ffn_w_gate_up': nrm(ks[10], (DEPTH, D_MODEL, 2 * D_FF), D_MODEL ** -0.5),
        'ffn_w_down': nrm(ks[11], (DEPTH, D_FF, D_MODEL), D_FF ** -0.5 * DEEPNORM_BETA),
    }


def reference(x, a_w_in, a_sinks, a_w_out, b_w_in, b_w_out, c_w_in, c_w_out, ln_g, ln_b, ffn_w_gate_up, ffn_w_down):
    for i in range(DEPTH):
        kind, j = i % N_MIXERS, i // N_MIXERS
        if kind == 0:
            y = swa_sink_attention(x, a_w_in[j], a_sinks[j], a_w_out[j])
        elif kind == 1:
            y = dilated_attention(x, b_w_in[j], b_w_out[j])
        else:
            y = dsa_sparse_attention(x, c_w_in[j], c_w_out[j])
        x = layer_norm(DEEPNORM_ALPHA * x + y, ln_g[i, 0], ln_b[i, 0])
        x = layer_norm(DEEPNORM_ALPHA * x + swiglu_ffn(x, ffn_w_gate_up[i], ffn_w_down[i]), ln_g[i, 1], ln_b[i, 1])
    return x
```

```python
import functools
import math

import numpy as np
import jax
import jax.numpy as jnp
from jax import lax
from jax.experimental import pallas as pl
from jax.experimental.pallas import tpu as pltpu

D_MODEL = 2048
DEPTH = 4
N_MIXERS = 3
HEAD_DIM = 64
N_HEADS = D_MODEL // HEAD_DIM
N_KV = N_HEADS // 8
GQA_GROUP = N_HEADS // N_KV
Q_DIM = N_HEADS * HEAD_DIM
KV_DIM = N_KV * HEAD_DIM
ATTN_SCALE = HEAD_DIM ** -0.5
BLOCK = 128
SWA_WINDOW = 128
DIL_PATTERNS = ((128, 1), (512, 4), (2048, 16))
N_DIL = len(DIL_PATTERNS)
IDX_HEADS = 16
IDX_DIM = 64
TOPK_MAX = 256
D_FF = 256 * math.ceil(8 * D_MODEL / (3 * 256))
DEEPNORM_ALPHA = (2 * DEPTH) ** 0.25
LN_EPS = 1e-5
GRP_IN = Q_DIM + 2 * KV_DIM
C_MAIN = Q_DIM + 2 * KV_DIM + IDX_HEADS * IDX_DIM
C_IN = C_MAIN + IDX_DIM + IDX_HEADS
LANES = 128
MASKED = -1e30
INT_MIN = -2 ** 31
VMEM_LIMIT = 56 * 1024 * 1024

_SLOPES = [float(np.float32(2.0 ** (-8.0 * (i + 1) / N_HEADS))) for i in range(N_HEADS)]


def _params(*sem):
    return pltpu.CompilerParams(dimension_semantics=sem, vmem_limit_bytes=VMEM_LIMIT)


def _mm_kernel(x_ref, w_ref, o_ref):
    o_ref[...] = jnp.dot(x_ref[...], w_ref[...],
                         preferred_element_type=jnp.float32).astype(o_ref.dtype)


def _matmul(x, w, out_dtype, tm, tn):
    m, k = x.shape
    n = w.shape[1]
    tm, tn = min(tm, m), min(tn, n)
    assert m % tm == 0 and n % tn == 0
    return pl.pallas_call(
        _mm_kernel,
        grid=(m // tm, n // tn),
        in_specs=[pl.BlockSpec((tm, k), lambda i, j: (i, 0)),
                  pl.BlockSpec((k, tn), lambda i, j: (0, j))],
        out_specs=pl.BlockSpec((tm, tn), lambda i, j: (i, j)),
        out_shape=jax.ShapeDtypeStruct((m, n), out_dtype),
        compiler_params=_params("parallel", "parallel"),
        name="proj_matmul",
    )(x, w)


def _gate_up_kernel(x_ref, wg_ref, wu_ref, o_ref):
    x = x_ref[...]
    g = jnp.dot(x, wg_ref[...], preferred_element_type=jnp.float32)
    u = jnp.dot(x, wu_ref[...], preferred_element_type=jnp.float32)
    o_ref[...] = (g * (1.0 / (1.0 + jnp.exp(-g))) * u).astype(o_ref.dtype)


def _ffn_gate_up(x, w, tm, tn):
    m, k = x.shape
    d_ff = w.shape[1] // 2
    tm = min(tm, m)
    assert m % tm == 0 and d_ff % tn == 0
    nj = d_ff // tn
    return pl.pallas_call(
        _gate_up_kernel,
        grid=(m // tm, nj),
        in_specs=[pl.BlockSpec((tm, k), lambda i, j: (i, 0)),
                  pl.BlockSpec((k, tn), lambda i, j: (0, j)),
                  pl.BlockSpec((k, tn), lambda i, j: (0, j + nj))],
        out_specs=pl.BlockSpec((tm, tn), lambda i, j: (i, j)),
        out_shape=jax.ShapeDtypeStruct((m, d_ff), jnp.bfloat16),
        compiler_params=_params("parallel", "parallel"),
        name="ffn_gate_up",
    )(x, w, w)


def _proj_ln_kernel(a_ref, w_ref, r_ref, g_ref, b_ref, o_ref, ob_ref, acc_ref):
    kk = pl.program_id(1)
    part = jnp.dot(a_ref[...], w_ref[...], preferred_element_type=jnp.float32)

    @pl.when(kk == 0)
    def _():
        acc_ref[...] = part

    @pl.when(kk > 0)
    def _():
        acc_ref[...] += part

    @pl.when(kk == pl.num_programs(1) - 1)
    def _():
        z = DEEPNORM_ALPHA * r_ref[...] + acc_ref[...]
        mu = jnp.mean(z, axis=-1, keepdims=True)
        zc = z - mu
        var = jnp.mean(zc * zc, axis=-1, keepdims=True)
        y = zc * lax.rsqrt(var + LN_EPS) * g_ref[...] + b_ref[...]
        o_ref[...] = y
        ob_ref[...] = y.astype(jnp.bfloat16)


def _proj_residual_ln(a, w, resid, g, b, tm, tk):
    m, k = a.shape
    n = w.shape[1]
    tm, tk = min(tm, m), min(tk, k)
    assert m % tm == 0 and k % tk == 0
    return pl.pallas_call(
        _proj_ln_kernel,
        grid=(m // tm, k // tk),
        in_specs=[pl.BlockSpec((tm, tk), lambda i, kk: (i, kk)),
                  pl.BlockSpec((tk, n), lambda i, kk: (kk, 0)),
                  pl.BlockSpec((tm, n), lambda i, kk: (i, 0)),
                  pl.BlockSpec((1, n), lambda i, kk: (0, 0)),
                  pl.BlockSpec((1, n), lambda i, kk: (0, 0))],
        out_specs=[pl.BlockSpec((tm, n), lambda i, kk: (i, 0)),
                   pl.BlockSpec((tm, n), lambda i, kk: (i, 0))],
        out_shape=[jax.ShapeDtypeStruct((m, n), jnp.float32),
                   jax.ShapeDtypeStruct((m, n), jnp.bfloat16)],
        scratch_shapes=[pltpu.VMEM((tm, n), jnp.float32)],
        compiler_params=_params("parallel", "arbitrary"),
        name="proj_residual_ln",
    )(a, w, resid, g.reshape(1, n), b.reshape(1, n))


def _band_kernel(*refs, max_j, dilation, has_sink):
    if has_sink:
        sink_ref, q_ref, kp_ref, kc_ref, vp_ref, vc_ref, o_ref = refs
    else:
        q_ref, kp_ref, kc_ref, vp_ref, vc_ref, o_ref, lse_ref = refs
    i = pl.program_id(1)
    k = jnp.concatenate([kp_ref[...], kc_ref[...]], axis=0)
    v = jnp.concatenate([vp_ref[...], vc_ref[...]], axis=0)
    qpos = lax.broadcasted_iota(jnp.int32, (BLOCK, 2 * BLOCK), 0) + BLOCK
    kpos = lax.broadcasted_iota(jnp.int32, (BLOCK, 2 * BLOCK), 1)
    j = qpos - kpos
    valid = (j >= 0) & (j <= max_j) & ((kpos >= BLOCK) | (i > 0))
    dist = (j * dilation).astype(jnp.float32)
    for kv in range(N_KV):
        k_kv = k[:, kv * HEAD_DIM:(kv + 1) * HEAD_DIM]
        v_kv = v[:, kv * HEAD_DIM:(kv + 1) * HEAD_DIM]
        qs = jnp.concatenate(
            [q_ref[:, (kv * GQA_GROUP + g) * HEAD_DIM:(kv * GQA_GROUP + g + 1) * HEAD_DIM]
             for g in range(GQA_GROUP)], axis=0)
        s_all = lax.dot_general(qs, k_kv, (((1,), (1,)), ((), ())),
                                preferred_element_type=jnp.float32)
        ps, ms, ls = [], [], []
        for g in range(GQA_GROUP):
            h = kv * GQA_GROUP + g
            s = s_all[g * BLOCK:(g + 1) * BLOCK]
            s = jnp.where(valid, s - _SLOPES[h] * dist, MASKED)
            m = jnp.max(s, axis=-1, keepdims=True)
            if has_sink:
                m = jnp.maximum(m, sink_ref[h])
            p = jnp.exp(s - m)
            l = jnp.sum(p, axis=-1, keepdims=True)
            if has_sink:
                l = l + jnp.exp(sink_ref[h] - m)
            ps.append(p.astype(jnp.bfloat16))
            ms.append(m)
            ls.append(l)
        o_all = jnp.dot(jnp.concatenate(ps, axis=0), v_kv,
                        preferred_element_type=jnp.float32)
        for g in range(GQA_GROUP):
            h = kv * GQA_GROUP + g
            o = o_all[g * BLOCK:(g + 1) * BLOCK] / ls[g]
            o_ref[:, h * HEAD_DIM:(h + 1) * HEAD_DIM] = o.astype(o_ref.dtype)
            if not has_sink:
                lse = ms[g] + jnp.log(ls[g])
                lse_ref[:, h * HEAD_DIM:(h + 1) * HEAD_DIM] = jnp.broadcast_to(lse, (BLOCK, HEAD_DIM))


def _band_attention(h, max_j, dilation, sinks=None):
    r, l, _ = h.shape
    nb = l // BLOCK
    kcol, vcol = Q_DIM // KV_DIM, Q_DIM // KV_DIM + 1
    has_sink = sinks is not None
    in_specs = [pl.BlockSpec((None, BLOCK, Q_DIM), lambda s, i: (s, i, 0)),
                pl.BlockSpec((None, BLOCK, KV_DIM), lambda s, i: (s, jnp.maximum(i - 1, 0), kcol)),
                pl.BlockSpec((None, BLOCK, KV_DIM), lambda s, i: (s, i, kcol)),
                pl.BlockSpec((None, BLOCK, KV_DIM), lambda s, i: (s, jnp.maximum(i - 1, 0), vcol)),
                pl.BlockSpec((None, BLOCK, KV_DIM), lambda s, i: (s, i, vcol))]
    o_spec = pl.BlockSpec((None, BLOCK, Q_DIM), lambda s, i: (s, i, 0))
    args = [h, h, h, h, h]
    if has_sink:
        in_specs = [pl.BlockSpec(memory_space=pltpu.SMEM)] + in_specs
        args = [sinks.astype(jnp.float32)] + args
        out_specs = o_spec
        out_shape = jax.ShapeDtypeStruct((r, l, Q_DIM), jnp.bfloat16)
    else:
        out_specs = [o_spec, o_spec]
        out_shape = [jax.ShapeDtypeStruct((r, l, Q_DIM), jnp.float32),
                     jax.ShapeDtypeStruct((r, l, Q_DIM), jnp.float32)]
    return pl.pallas_call(
        functools.partial(_band_kernel, max_j=max_j, dilation=dilation, has_sink=has_sink),
        grid=(r, nb),
        in_specs=in_specs,
        out_specs=out_specs,
        out_shape=out_shape,
        compiler_params=_params("parallel", "parallel"),
        name="band_attention",
    )(*args)


def _merge_kernel(o0, o1, o2, l0, l1, l2, out_ref):
    a, b, c = l0[...], l1[...], l2[...]
    m = jnp.maximum(jnp.maximum(a, b), c)
    ea, eb, ec = jnp.exp(a - m), jnp.exp(b - m), jnp.exp(c - m)
    num = ea * o0[...] + eb * o1[...] + ec * o2[...]
    out_ref[...] = (num / (ea + eb + ec)).astype(out_ref.dtype)


def _merge_groups(outs, lses, tm):
    m, n = outs[0].shape
    tm = min(tm, m)
    spec = pl.BlockSpec((tm, n), lambda i: (i, 0))
    return pl.pallas_call(
        _merge_kernel,
        grid=(m // tm,),
        in_specs=[spec] * 6,
        out_specs=spec,
        out_shape=jax.ShapeDtypeStruct((m, n), jnp.bfloat16),
        compiler_params=_params("parallel"),
        name="merge_groups",
    )(*outs, *lses)


def _count_ge(key_ref, cand, n_tiles, tk):
    def body(t, acc):
        kt = key_ref[:, pl.ds(pl.multiple_of(t * tk, tk), tk)]
        return acc + (kt >= cand).astype(jnp.float32)
    acc = lax.fori_loop(0, n_tiles, body, jnp.zeros((BLOCK, tk), jnp.float32))
    return jnp.sum(acc, axis=-1, keepdims=True)


def _dsa_kernel(q_ref, k_ref, v_ref, qi_ref, ki_ref, wi_ref, o_ref,
                key_ref, m_ref, l_ref, acc_ref, *, top_k, tk):
    i = pl.program_id(1)
    n_tiles = (i * BLOCK + BLOCK + tk - 1) // tk
    row = lax.broadcasted_iota(jnp.int32, (BLOCK, tk), 0) + i * BLOCK
    col0 = lax.broadcasted_iota(jnp.int32, (BLOCK, tk), 1)

    qis = jnp.concatenate([qi_ref[:, h * IDX_DIM:(h + 1) * IDX_DIM] for h in range(IDX_HEADS)],
                          axis=0)
    wi = wi_ref[...]

    def score_body(t, carry):
        start = pl.multiple_of(t * tk, tk)
        ki = ki_ref[pl.ds(start, tk), :][:, :IDX_DIM].astype(jnp.bfloat16)
        rel = lax.dot_general(qis, ki, (((1,), (1,)), ((), ())),
                              preferred_element_type=jnp.float32)
        sc = jnp.zeros((BLOCK, tk), jnp.float32)
        for h in range(IDX_HEADS):
            w_h = wi[:, IDX_DIM + h:IDX_DIM + h + 1]
            sc = sc + jnp.maximum(rel[h * BLOCK:(h + 1) * BLOCK], 0.0) * w_h
        sc = sc + 0.0
        bits = pltpu.bitcast(sc, jnp.int32)
        keys = jnp.where(bits < 0, bits ^ jnp.int32(0x7FFFFFFF), bits)
        keys = jnp.where(col0 + start <= row, keys, jnp.int32(INT_MIN))
        key_ref[:, pl.ds(start, tk)] = keys
        return carry

    lax.fori_loop(0, n_tiles, score_body, 0)

    kf = jnp.float32(top_k)
    cnt0 = _count_ge(key_ref, jnp.zeros((BLOCK, 1), jnp.int32), n_tiles, tk)
    prefix = jnp.where(cnt0 >= kf, jnp.int32(0), jnp.int32(INT_MIN))

    def bit_body(b, prefix):
        cand = prefix | lax.shift_left(jnp.int32(1), jnp.int32(30) - b)
        cnt = _count_ge(key_ref, cand, n_tiles, tk)
        return jnp.where(cnt >= kf, cand, prefix)

    thr = lax.fori_loop(0, 31, bit_body, prefix)
    n_gt = _count_ge(key_ref, thr + 1, n_tiles, tk)
    n_gt = jnp.where(thr == jnp.int32(2 ** 31 - 1), 0.0, n_gt)
    need = kf - n_gt

    def count_tie_below(x):
        def body(t, acc):
            start = pl.multiple_of(t * tk, tk)
            kt = key_ref[:, pl.ds(start, tk)]
            hit = (kt == thr) & (col0 + start < x)
            return acc + hit.astype(jnp.float32)
        acc = lax.fori_loop(0, n_tiles, body, jnp.zeros((BLOCK, tk), jnp.float32))
        return jnp.sum(acc, axis=-1, keepdims=True)

    n_idx_bits = max(1, int(math.ceil(math.log2(key_ref.shape[1]))))

    def idx_body(b, p):
        cand = p | lax.shift_left(jnp.int32(1), jnp.int32(n_idx_bits - 1) - b)
        return jnp.where(count_tie_below(cand) <= need - 1.0, cand, p)

    last_tie = lax.fori_loop(0, n_idx_bits, idx_body, jnp.zeros((BLOCK, 1), jnp.int32))

    m_ref[...] = jnp.full(m_ref.shape, MASKED, jnp.float32)
    l_ref[...] = jnp.zeros(l_ref.shape, jnp.float32)
    acc_ref[...] = jnp.zeros(acc_ref.shape, jnp.float32)
    qss = [jnp.concatenate(
        [q_ref[:, (kv * GQA_GROUP + g) * HEAD_DIM:(kv * GQA_GROUP + g + 1) * HEAD_DIM]
         for g in range(GQA_GROUP)], axis=0) for kv in range(N_KV)]

    def attn_body(t, carry):
        start = pl.multiple_of(t * tk, tk)
        kt = key_ref[:, pl.ds(start, tk)]
        col = col0 + start
        sel = ((kt > thr) | ((kt == thr) & (col <= last_tie))) & (col <= row)
        dist = (row - col).astype(jnp.float32)
        k_t = k_ref[pl.ds(start, tk), :]
        v_t = v_ref[pl.ds(start, tk), :]
        for kv in range(N_KV):
            k_kv = k_t[:, kv * HEAD_DIM:(kv + 1) * HEAD_DIM]
            v_kv = v_t[:, kv * HEAD_DIM:(kv + 1) * HEAD_DIM]
            s_all = lax.dot_general(qss[kv], k_kv, (((1,), (1,)), ((), ())),
                                    preferred_element_type=jnp.float32)
            m_old = m_ref[kv]
            ps, m_news = [], []
            for g in range(GQA_GROUP):
                h = kv * GQA_GROUP + g
                s = s_all[g * BLOCK:(g + 1) * BLOCK]
                s = jnp.where(sel, s - _SLOPES[h] * dist, MASKED)
                m_new = jnp.maximum(m_old[g * BLOCK:(g + 1) * BLOCK],
                                    jnp.max(s, axis=-1, keepdims=True))
                ps.append(jnp.exp(s - m_new))
                m_news.append(m_new)
            m_new = jnp.concatenate(m_news, axis=0)
            p = jnp.concatenate(ps, axis=0)
            alpha = jnp.exp(m_old - m_new)
            l_ref[kv] = alpha * l_ref[kv] + jnp.sum(p, axis=-1, keepdims=True)
            acc_ref[kv] = alpha * acc_ref[kv] + jnp.dot(
                p.astype(jnp.bfloat16), v_kv, preferred_element_type=jnp.float32)
            m_ref[kv] = m_new
        return carry

    lax.fori_loop(0, n_tiles, attn_body, 0)

    for kv in range(N_KV):
        o_all = acc_ref[kv] / l_ref[kv]
        for g in range(GQA_GROUP):
            h = kv * GQA_GROUP + g
            o_ref[:, h * HEAD_DIM:(h + 1) * HEAD_DIM] = (
                o_all[g * BLOCK:(g + 1) * BLOCK].astype(o_ref.dtype))


def _dsa_attention(hm, kiwi, tk):
    b, s, _ = hm.shape
    nb = s // BLOCK
    tk = min(tk, s)
    top_k = min(TOPK_MAX, s // 4)
    qi_dim = IDX_HEADS * IDX_DIM
    qicol = Q_DIM // qi_dim
    kcol = (Q_DIM + qi_dim) // KV_DIM
    vcol = kcol + 1
    rows = GQA_GROUP * BLOCK
    return pl.pallas_call(
        functools.partial(_dsa_kernel, top_k=top_k, tk=tk),
        grid=(b, nb),
        in_specs=[pl.BlockSpec((None, BLOCK, Q_DIM), lambda bb, i: (bb, i, 0)),
                  pl.BlockSpec((None, s, KV_DIM), lambda bb, i: (bb, 0, kcol)),
                  pl.BlockSpec((None, s, KV_DIM), lambda bb, i: (bb, 0, vcol)),
                  pl.BlockSpec((None, BLOCK, IDX_HEADS * IDX_DIM), lambda bb, i: (bb, i, qicol)),
                  pl.BlockSpec((None, s, LANES), lambda bb, i: (bb, 0, 0)),
                  pl.BlockSpec((None, BLOCK, LANES), lambda bb, i: (bb, i, 0))],
        out_specs=pl.BlockSpec((None, BLOCK, Q_DIM), lambda bb, i: (bb, i, 0)),
        out_shape=jax.ShapeDtypeStruct((b, s, Q_DIM), jnp.bfloat16),
        scratch_shapes=[pltpu.VMEM((BLOCK, s), jnp.int32),
                        pltpu.VMEM((N_KV, rows, 1), jnp.float32),
                        pltpu.VMEM((N_KV, rows, 1), jnp.float32),
                        pltpu.VMEM((N_KV, rows, HEAD_DIM), jnp.float32)],
        compiler_params=_params("parallel", "parallel"),
        name="dsa_attention",
    )(hm, hm, hm, hm, kiwi, kiwi)


def _scale_q_cols(w, n_q, scale):
    return jnp.concatenate([w[:, :n_q] * scale, w[:, n_q:]], axis=1)


def _swa_layer(xb, w_in, sinks, b, s):
    w = _scale_q_cols(w_in, Q_DIM, ATTN_SCALE).astype(jnp.bfloat16)
    h = _matmul(xb, w, jnp.bfloat16, 1024, 1280)
    o = _band_attention(h.reshape(b, s, GRP_IN), SWA_WINDOW - 1, 1, sinks)
    return o.reshape(b * s, Q_DIM)


def _dilated_layer(xb, w_in, b, s):
    outs, lses = [], []
    for g, (window, dil) in enumerate(DIL_PATTERNS):
        w = _scale_q_cols(w_in[:, g * GRP_IN:(g + 1) * GRP_IN], Q_DIM, ATTN_SCALE).astype(jnp.bfloat16)
        xg = xb.reshape(b, s // dil, dil, D_MODEL).transpose(0, 2, 1, 3).reshape(b * s, D_MODEL)
        h = _matmul(xg, w, jnp.bfloat16, 1024, 1280)
        o, lse = _band_attention(h.reshape(b * dil, s // dil, GRP_IN), window // dil, dil)
        unperm = lambda t: t.reshape(b, dil, s // dil, Q_DIM).transpose(0, 2, 1, 3).reshape(b * s, Q_DIM)
        outs.append(unperm(o))
        lses.append(unperm(lse))
    return _merge_groups(outs, lses, 256)


def _dsa_layer(xb, w_in, b, s):
    w_main = jnp.concatenate([w_in[:, :Q_DIM] * ATTN_SCALE, w_in[:, GRP_IN:C_MAIN] * IDX_DIM ** -0.5,
                              w_in[:, Q_DIM:GRP_IN]], axis=1)
    w_small = jnp.concatenate([w_in[:, C_MAIN:C_MAIN + IDX_DIM],
                               w_in[:, C_MAIN + IDX_DIM:] * IDX_HEADS ** -0.5,
                               jnp.zeros((D_MODEL, LANES - IDX_DIM - IDX_HEADS), w_in.dtype)], axis=1)
    hm = _matmul(xb, w_main.astype(jnp.bfloat16), jnp.bfloat16, 1024, 896)
    kiwi = _matmul(xb, w_small.astype(jnp.bfloat16), jnp.float32, 1024, LANES)
    o = _dsa_attention(hm.reshape(b, s, C_MAIN), kiwi.reshape(b, s, LANES), 512)
    return o.reshape(b * s, Q_DIM)


def kernel(x, a_w_in, a_sinks, a_w_out, b_w_in, b_w_out, c_w_in, c_w_out, ln_g, ln_b,
           ffn_w_gate_up, ffn_w_down):
    b, s, d = x.shape
    xf = x.reshape(b * s, d)
    xb = xf.astype(jnp.bfloat16)
    for i in range(DEPTH):
        kind, j = i % N_MIXERS, i // N_MIXERS
        if kind == 0:
            o = _swa_layer(xb, a_w_in[j], a_sinks[j], b, s)
            w_out = a_w_out[j]
        elif kind == 1:
            o = _dilated_layer(xb, b_w_in[j], b, s)
            w_out = b_w_out[j]
        else:
            o = _dsa_layer(xb, c_w_in[j], b, s)
            w_out = c_w_out[j]
        xf, xb = _proj_residual_ln(o, w_out.astype(jnp.bfloat16), xf, ln_g[i, 0], ln_b[i, 0], 512, 2048)
        hmid = _ffn_gate_up(xb, ffn_w_gate_up[i].astype(jnp.bfloat16), 1024, 512)
        xf, xb = _proj_residual_ln(hmid, ffn_w_down[i].astype(jnp.bfloat16), xf,
                                   ln_g[i, 1], ln_b[i, 1], 512, 1408)
    return xf.reshape(b, s, d)
```

```python
import functools
import math

import numpy as np
import jax
import jax.numpy as jnp
from jax import lax
from jax.experimental import pallas as pl
from jax.experimental.pallas import tpu as pltpu

D_MODEL = 2048
DEPTH = 4
N_MIXERS = 3
HEAD_DIM = 64
N_HEADS = D_MODEL // HEAD_DIM
N_KV = N_HEADS // 8
GQA_GROUP = N_HEADS // N_KV
Q_DIM = N_HEADS * HEAD_DIM
KV_DIM = N_KV * HEAD_DIM
ATTN_SCALE = HEAD_DIM ** -0.5
BLOCK = 128
SWA_WINDOW = 128
DIL_PATTERNS = ((128, 1), (512, 4), (2048, 16))
N_DIL = len(DIL_PATTERNS)
IDX_HEADS = 16
IDX_DIM = 64
TOPK_MAX = 256
D_FF = 256 * math.ceil(8 * D_MODEL / (3 * 256))
DEEPNORM_ALPHA = (2 * DEPTH) ** 0.25
LN_EPS = 1e-5
GRP_IN = Q_DIM + 2 * KV_DIM
C_MAIN = Q_DIM + 2 * KV_DIM + IDX_HEADS * IDX_DIM
C_IN = C_MAIN + IDX_DIM + IDX_HEADS
LANES = 128
MASKED = -1e30
INT_MIN = -2 ** 31
LOG2E = 1.4426950408889634
VMEM_LIMIT = 56 * 1024 * 1024

_SLOPES = [float(np.float32(2.0 ** (-8.0 * (i + 1) / N_HEADS))) for i in range(N_HEADS)]
N_SLOPE_PIECES = 3
POS_SPLIT = 64


def _params(*sem):
    return pltpu.CompilerParams(dimension_semantics=sem, vmem_limit_bytes=VMEM_LIMIT)


def _mm_kernel(x_ref, w_ref, o_ref):
    o_ref[...] = jnp.dot(x_ref[...], w_ref[...],
                         preferred_element_type=jnp.float32).astype(o_ref.dtype)


def _matmul(x, w, out_dtype, tm, tn):
    m, k = x.shape
    n = w.shape[1]
    tm, tn = min(tm, m), min(tn, n)
    assert m % tm == 0 and n % tn == 0
    return pl.pallas_call(
        _mm_kernel,
        grid=(m // tm, n // tn),
        in_specs=[pl.BlockSpec((tm, k), lambda i, j: (i, 0)),
                  pl.BlockSpec((k, tn), lambda i, j: (0, j))],
        out_specs=pl.BlockSpec((tm, tn), lambda i, j: (i, j)),
        out_shape=jax.ShapeDtypeStruct((m, n), out_dtype),
        compiler_params=_params("parallel", "parallel"),
        name="proj_matmul",
    )(x, w)


def _gate_up_kernel(x_ref, wg_ref, wu_ref, o_ref):
    x = x_ref[...]
    g = jnp.dot(x, wg_ref[...], preferred_element_type=jnp.float32)
    u = jnp.dot(x, wu_ref[...], preferred_element_type=jnp.float32)
    o_ref[...] = (g * (1.0 / (1.0 + jnp.exp(-g))) * u).astype(o_ref.dtype)


def _ffn_gate_up(x, w, tm, tn):
    m, k = x.shape
    d_ff = w.shape[1] // 2
    tm = min(tm, m)
    assert m % tm == 0 and d_ff % tn == 0
    nj = d_ff // tn
    return pl.pallas_call(
        _gate_up_kernel,
        grid=(m // tm, nj),
        in_specs=[pl.BlockSpec((tm, k), lambda i, j: (i, 0)),
                  pl.BlockSpec((k, tn), lambda i, j: (0, j)),
                  pl.BlockSpec((k, tn), lambda i, j: (0, j + nj))],
        out_specs=pl.BlockSpec((tm, tn), lambda i, j: (i, j)),
        out_shape=jax.ShapeDtypeStruct((m, d_ff), jnp.bfloat16),
        compiler_params=_params("parallel", "parallel"),
        name="ffn_gate_up",
    )(x, w, w)


def _proj_ln_kernel(a_ref, w_ref, r_ref, g_ref, b_ref, o_ref, ob_ref, acc_ref):
    kk = pl.program_id(1)
    part = jnp.dot(a_ref[...], w_ref[...], preferred_element_type=jnp.float32)

    @pl.when(kk == 0)
    def _():
        acc_ref[...] = part

    @pl.when(kk > 0)
    def _():
        acc_ref[...] += part

    @pl.when(kk == pl.num_programs(1) - 1)
    def _():
        z = DEEPNORM_ALPHA * r_ref[...] + acc_ref[...]
        mu = jnp.mean(z, axis=-1, keepdims=True)
        zc = z - mu
        var = jnp.mean(zc * zc, axis=-1, keepdims=True)
        y = zc * lax.rsqrt(var + LN_EPS) * g_ref[...] + b_ref[...]
        o_ref[...] = y
        ob_ref[...] = y.astype(jnp.bfloat16)


def _proj_residual_ln(a, w, resid, g, b, tm, tk):
    m, k = a.shape
    n = w.shape[1]
    tm, tk = min(tm, m), min(tk, k)
    assert m % tm == 0 and k % tk == 0
    return pl.pallas_call(
        _proj_ln_kernel,
        grid=(m // tm, k // tk),
        in_specs=[pl.BlockSpec((tm, tk), lambda i, kk: (i, kk)),
                  pl.BlockSpec((tk, n), lambda i, kk: (kk, 0)),
                  pl.BlockSpec((tm, n), lambda i, kk: (i, 0)),
                  pl.BlockSpec((1, n), lambda i, kk: (0, 0)),
                  pl.BlockSpec((1, n), lambda i, kk: (0, 0))],
        out_specs=[pl.BlockSpec((tm, n), lambda i, kk: (i, 0)),
                   pl.BlockSpec((tm, n), lambda i, kk: (i, 0))],
        out_shape=[jax.ShapeDtypeStruct((m, n), jnp.float32),
                   jax.ShapeDtypeStruct((m, n), jnp.bfloat16)],
        scratch_shapes=[pltpu.VMEM((tm, n), jnp.float32)],
        compiler_params=_params("parallel", "arbitrary"),
        name="proj_residual_ln",
    )(a, w, resid, g.reshape(1, n), b.reshape(1, n))


def _band_kernel(*refs, max_j, dilation, has_sink):
    if has_sink:
        sink_ref, q_ref, kp_ref, kc_ref, vp_ref, vc_ref, o_ref = refs
    else:
        q_ref, kp_ref, kc_ref, vp_ref, vc_ref, o_ref, lse_ref = refs
    i = pl.program_id(1)
    k = jnp.concatenate([kp_ref[...], kc_ref[...]], axis=0)
    v = jnp.concatenate([vp_ref[...], vc_ref[...]], axis=0)
    qpos = lax.broadcasted_iota(jnp.int32, (BLOCK, 2 * BLOCK), 0) + BLOCK
    kpos = lax.broadcasted_iota(jnp.int32, (BLOCK, 2 * BLOCK), 1)
    j = qpos - kpos
    valid = (j >= 0) & (j <= max_j) & ((kpos >= BLOCK) | (i > 0))
    dist = (j * dilation).astype(jnp.float32)
    for kv in range(N_KV):
        k_kv = k[:, kv * HEAD_DIM:(kv + 1) * HEAD_DIM]
        v_kv = v[:, kv * HEAD_DIM:(kv + 1) * HEAD_DIM]
        qs = jnp.concatenate(
            [q_ref[:, (kv * GQA_GROUP + g) * HEAD_DIM:(kv * GQA_GROUP + g + 1) * HEAD_DIM]
             for g in range(GQA_GROUP)], axis=0)
        s_all = lax.dot_general(qs, k_kv, (((1,), (1,)), ((), ())),
                                preferred_element_type=jnp.float32)
        ps, ms, ls = [], [], []
        for g in range(GQA_GROUP):
            h = kv * GQA_GROUP + g
            s = s_all[g * BLOCK:(g + 1) * BLOCK]
            s = jnp.where(valid, s - _SLOPES[h] * dist, MASKED)
            m = jnp.max(s, axis=-1, keepdims=True)
            if has_sink:
                m = jnp.maximum(m, sink_ref[h])
            p = jnp.exp(s - m)
            l = jnp.sum(p, axis=-1, keepdims=True)
            if has_sink:
                l = l + jnp.exp(sink_ref[h] - m)
            ps.append(p.astype(jnp.bfloat16))
            ms.append(m)
            ls.append(l)
        o_all = jnp.dot(jnp.concatenate(ps, axis=0), v_kv,
                        preferred_element_type=jnp.float32)
        for g in range(GQA_GROUP):
            h = kv * GQA_GROUP + g
            o = o_all[g * BLOCK:(g + 1) * BLOCK] / ls[g]
            o_ref[:, h * HEAD_DIM:(h + 1) * HEAD_DIM] = o.astype(o_ref.dtype)
            if not has_sink:
                lse = ms[g] + jnp.log(ls[g])
                lse_ref[:, h * HEAD_DIM:(h + 1) * HEAD_DIM] = jnp.broadcast_to(lse, (BLOCK, HEAD_DIM))


def _band_attention(h, max_j, dilation, sinks=None):
    r, l, _ = h.shape
    nb = l // BLOCK
    kcol, vcol = Q_DIM // KV_DIM, Q_DIM // KV_DIM + 1
    has_sink = sinks is not None
    in_specs = [pl.BlockSpec((None, BLOCK, Q_DIM), lambda s, i: (s, i, 0)),
                pl.BlockSpec((None, BLOCK, KV_DIM), lambda s, i: (s, jnp.maximum(i - 1, 0), kcol)),
                pl.BlockSpec((None, BLOCK, KV_DIM), lambda s, i: (s, i, kcol)),
                pl.BlockSpec((None, BLOCK, KV_DIM), lambda s, i: (s, jnp.maximum(i - 1, 0), vcol)),
                pl.BlockSpec((None, BLOCK, KV_DIM), lambda s, i: (s, i, vcol))]
    o_spec = pl.BlockSpec((None, BLOCK, Q_DIM), lambda s, i: (s, i, 0))
    args = [h, h, h, h, h]
    if has_sink:
        in_specs = [pl.BlockSpec(memory_space=pltpu.SMEM)] + in_specs
        args = [sinks.astype(jnp.float32)] + args
        out_specs = o_spec
        out_shape = jax.ShapeDtypeStruct((r, l, Q_DIM), jnp.bfloat16)
    else:
        out_specs = [o_spec, o_spec]
        out_shape = [jax.ShapeDtypeStruct((r, l, Q_DIM), jnp.float32),
                     jax.ShapeDtypeStruct((r, l, Q_DIM), jnp.float32)]
    return pl.pallas_call(
        functools.partial(_band_kernel, max_j=max_j, dilation=dilation, has_sink=has_sink),
        grid=(r, nb),
        in_specs=in_specs,
        out_specs=out_specs,
        out_shape=out_shape,
        compiler_params=_params("parallel", "parallel"),
        name="band_attention",
    )(*args)


def _merge_kernel(o0, o1, o2, l0, l1, l2, out_ref):
    a, b, c = l0[...], l1[...], l2[...]
    m = jnp.maximum(jnp.maximum(a, b), c)
    ea, eb, ec = jnp.exp(a - m), jnp.exp(b - m), jnp.exp(c - m)
    num = ea * o0[...] + eb * o1[...] + ec * o2[...]
    out_ref[...] = (num / (ea + eb + ec)).astype(out_ref.dtype)


def _merge_groups(outs, lses, tm):
    m, n = outs[0].shape
    tm = min(tm, m)
    spec = pl.BlockSpec((tm, n), lambda i: (i, 0))
    return pl.pallas_call(
        _merge_kernel,
        grid=(m // tm,),
        in_specs=[spec] * 6,
        out_specs=spec,
        out_shape=jax.ShapeDtypeStruct((m, n), jnp.bfloat16),
        compiler_params=_params("parallel"),
        name="merge_groups",
    )(*outs, *lses)


def _count_tiles(n_tiles, tile_hits, tq, tcnt):
    def body(t, acc):
        hit = tile_hits(pl.multiple_of(t * tcnt, tcnt)).astype(jnp.float32)
        part = hit[:, :LANES]
        for c in range(1, tcnt // LANES):
            part = part + hit[:, c * LANES:(c + 1) * LANES]
        return acc + part
    acc = lax.fori_loop(0, n_tiles, body, jnp.zeros((tq, LANES), jnp.float32))
    return jnp.sum(acc, axis=-1, keepdims=True)


def _dsa_select_kernel(qi_ref, ki_ref, wi_ref, bias_ref, key_ref, *, top_k, tq, tsc, tcnt):
    i = pl.program_id(1)
    s_len = key_ref.shape[1]
    n_keys = (i + 1) * tq
    n_sc = n_keys // tsc
    n_cnt = (n_keys + tcnt - 1) // tcnt
    rowpos = lax.broadcasted_iota(jnp.int32, (tq, 1), 0) + i * tq
    col_sc = lax.broadcasted_iota(jnp.int32, (tq, tsc), 1)
    col_cnt = lax.broadcasted_iota(jnp.int32, (tq, tcnt), 1)

    key_ref[:, pl.ds(pl.multiple_of(n_cnt * tcnt - tsc, tsc), tsc)] = jnp.full((tq, tsc), INT_MIN, jnp.int32)

    qis = jnp.concatenate([qi_ref[:, h * IDX_DIM:(h + 1) * IDX_DIM] for h in range(IDX_HEADS)],
                          axis=0)
    wi = wi_ref[...]

    def score_body(t, carry):
        start = pl.multiple_of(t * tsc, tsc)
        ki = ki_ref[pl.ds(start, tsc), :][:, :IDX_DIM].astype(jnp.bfloat16)
        rel = lax.dot_general(qis, ki, (((1,), (1,)), ((), ())),
                              preferred_element_type=jnp.float32)
        sc = jnp.zeros((tq, tsc), jnp.float32)
        for h in range(IDX_HEADS):
            w_h = wi[:, IDX_DIM + h:IDX_DIM + h + 1]
            sc = sc + jnp.maximum(rel[h * tq:(h + 1) * tq], 0.0) * w_h
        sc = sc + 0.0
        bits = pltpu.bitcast(sc, jnp.int32)
        keys = jnp.where(bits < 0, bits ^ jnp.int32(0x7FFFFFFF), bits)
        keys = jnp.where(col_sc + start <= rowpos, keys, jnp.int32(INT_MIN))
        key_ref[:, pl.ds(start, tsc)] = keys
        return carry

    lax.fori_loop(0, n_sc, score_body, 0)

    kf = jnp.float32(top_k)

    def count_ge(cand):
        return _count_tiles(n_cnt, lambda st: key_ref[:, pl.ds(st, tcnt)] >= cand, tq, tcnt)

    short = (rowpos + 1 < top_k).astype(jnp.float32)

    def bit_cond(c):
        b, _, done, _ = c
        return (b < 32) & (jnp.min(done) < 0.5)

    def bit_body(c):
        b, prefix, done, thr = c
        cand = prefix ^ lax.shift_left(jnp.int32(1), jnp.int32(31) - b)
        cnt = count_ge(cand)
        prefix = jnp.where(cnt >= kf, cand, prefix)
        hit = (cnt == kf) & (done < 0.5)
        thr = jnp.where(hit, cand - 1, thr)
        done = jnp.where(hit, 1.0, done)
        return b + 1, prefix, done, thr

    init = (jnp.int32(0), jnp.full((tq, 1), INT_MIN, jnp.int32), short,
            jnp.full((tq, 1), INT_MIN, jnp.int32))
    _, prefix, done, thr = lax.while_loop(bit_cond, bit_body, init)
    is_done = done > 0.5
    thr = jnp.where(is_done, thr, prefix)

    def tie_search(_):
        n_gt = _count_tiles(n_cnt, lambda st: key_ref[:, pl.ds(st, tcnt)] > thr, tq, tcnt)
        need = kf - n_gt
        n_idx_bits = max(1, int(math.ceil(math.log2(s_len))))

        def idx_body(b, p):
            cand = p | lax.shift_left(jnp.int32(1), jnp.int32(n_idx_bits - 1) - b)
            below = _count_tiles(
                n_cnt, lambda st: (key_ref[:, pl.ds(st, tcnt)] == thr) & (col_cnt + st < cand), tq, tcnt)
            return jnp.where(below <= need - 1.0, cand, p)

        return lax.fori_loop(0, n_idx_bits, idx_body, jnp.zeros((tq, 1), jnp.int32))

    last_tie = lax.cond(jnp.min(done) < 0.5, tie_search,
                        lambda _: jnp.full((tq, 1), -1, jnp.int32), 0)
    last_tie = jnp.where(is_done, -1, last_tie)

    def write_body(t, carry):
        start = pl.multiple_of(t * tcnt, tcnt)
        kt = key_ref[:, pl.ds(start, tcnt)]
        col = col_cnt + start
        sel = ((kt > thr) | ((kt == thr) & (col <= last_tie))) & (col <= rowpos)
        bias_ref[:, pl.ds(start, tcnt)] = jnp.where(sel, 0.0, MASKED).astype(bias_ref.dtype)
        return carry

    lax.fori_loop(0, n_cnt, write_body, 0)

    def fill_body(t, carry):
        bias_ref[:, pl.ds(pl.multiple_of(t * tcnt, tcnt), tcnt)] = jnp.full(
            (tq, tcnt), MASKED, bias_ref.dtype)
        return carry

    lax.fori_loop(n_cnt, s_len // tcnt, fill_body, 0)


def _dsa_select(hm, kiwi, tq, tsc, tcnt):
    b, s, _ = hm.shape
    tq, tsc, tcnt = min(tq, s), min(tsc, s), min(tcnt, s)
    assert tq % tsc == 0 and tcnt % tsc == 0 and s % tcnt == 0 and s % tq == 0
    top_k = min(TOPK_MAX, s // 4)
    qi_dim = IDX_HEADS * IDX_DIM
    return pl.pallas_call(
        functools.partial(_dsa_select_kernel, top_k=top_k, tq=tq, tsc=tsc, tcnt=tcnt),
        grid=(b, s // tq),
        in_specs=[pl.BlockSpec((None, tq, qi_dim), lambda bb, i: (bb, i, Q_DIM // qi_dim)),
                  pl.BlockSpec((None, s, LANES), lambda bb, i: (bb, 0, 0)),
                  pl.BlockSpec((None, tq, LANES), lambda bb, i: (bb, i, 0))],
        out_specs=pl.BlockSpec((None, tq, s), lambda bb, i: (bb, i, 0)),
        out_shape=jax.ShapeDtypeStruct((b, s, s), jnp.bfloat16),
        scratch_shapes=[pltpu.VMEM((tq, s), jnp.int32)],
        compiler_params=_params("parallel", "parallel"),
        name="dsa_select",
    )(hm, kiwi, kiwi)


def _slope_columns():
    out = np.zeros((N_KV, GQA_GROUP * BLOCK, LANES), np.float32)
    for h in range(N_HEADS):
        rest = np.float64(_SLOPES[h]) * LOG2E
        pieces = []
        for _ in range(N_SLOPE_PIECES):
            p = np.float64(np.asarray(rest, np.float32).astype(jnp.bfloat16).astype(np.float32))
            pieces.append(p)
            rest = rest - p
        kv, g = divmod(h, GQA_GROUP)
        out[kv, g * BLOCK:(g + 1) * BLOCK, HEAD_DIM:HEAD_DIM + 2 * N_SLOPE_PIECES] = np.asarray(
            pieces + pieces, np.float32)
    return jnp.asarray(out, jnp.bfloat16)


def _dsa_attn_kernel(q_ref, k_ref, v_ref, bias_ref, sl_ref, o_ref,
                     kaug_ref, vaug_ref, m_ref, acc_ref, *, tk, chunk):
    i = pl.program_id(1)
    s_len = k_ref.shape[0]

    @pl.when(i == 0)
    def _():
        lane = lax.broadcasted_iota(jnp.int32, (chunk, HEAD_DIM), 1)
        ones_col = (lane == 0).astype(jnp.bfloat16)
        for c in range(s_len // chunk):
            pos = lax.broadcasted_iota(jnp.int32, (chunk, HEAD_DIM), 0) + c * chunk
            hi = pos - (pos & (POS_SPLIT - 1))
            lo = pos & (POS_SPLIT - 1)
            pcols = jnp.where(lane < N_SLOPE_PIECES, hi, jnp.where(lane < 2 * N_SLOPE_PIECES, lo, 0))
            pcols = pcols.astype(jnp.float32).astype(jnp.bfloat16)
            rows = pl.ds(c * chunk, chunk)
            for kv in range(N_KV):
                kaug_ref[kv, rows, :] = jnp.concatenate(
                    [k_ref[rows, kv * HEAD_DIM:(kv + 1) * HEAD_DIM], pcols], axis=1)
                vaug_ref[kv, rows, :] = jnp.concatenate(
                    [v_ref[rows, kv * HEAD_DIM:(kv + 1) * HEAD_DIM], ones_col], axis=1)

    n_tiles = (i * BLOCK + BLOCK + tk - 1) // tk
    m_ref[...] = jnp.full(m_ref.shape, MASKED, jnp.float32)
    acc_ref[...] = jnp.zeros(acc_ref.shape, jnp.float32)
    qa = [jnp.concatenate(
        [jnp.concatenate(
            [q_ref[:, (kv * GQA_GROUP + g) * HEAD_DIM:(kv * GQA_GROUP + g + 1) * HEAD_DIM]
             for g in range(GQA_GROUP)], axis=0), sl_ref[kv][:, HEAD_DIM:]], axis=1)
        for kv in range(N_KV)]
    n_rep = tk // LANES

    def attn_body(t, carry):
        start = pl.multiple_of(t * tk, tk)
        bias = bias_ref[:, pl.ds(start, tk)].astype(jnp.float32)
        for kv in range(N_KV):
            s_all = lax.dot_general(qa[kv], kaug_ref[kv, pl.ds(start, tk), :],
                                    (((1,), (1,)), ((), ())),
                                    preferred_element_type=jnp.float32)
            m_old = m_ref[kv]
            ps, m_news = [], []
            for g in range(GQA_GROUP):
                s = s_all[g * BLOCK:(g + 1) * BLOCK] + bias
                m_new = jnp.maximum(m_old[g * BLOCK:(g + 1) * BLOCK],
                                    jnp.max(s, axis=-1, keepdims=True))
                ps.append(jnp.exp2(s - jnp.concatenate([m_new] * n_rep, axis=1)).astype(jnp.bfloat16))
                m_news.append(m_new)
            m_new = jnp.concatenate(m_news, axis=0)
            pv = jnp.dot(jnp.concatenate(ps, axis=0), vaug_ref[kv, pl.ds(start, tk), :],
                         preferred_element_type=jnp.float32)
            acc_ref[kv] = jnp.exp2(m_old - m_new) * acc_ref[kv] + pv
            m_ref[kv] = m_new
        return carry

    lax.fori_loop(0, n_tiles, attn_body, 0)

    for kv in range(N_KV):
        acc = acc_ref[kv]
        o_all = (acc[:, :HEAD_DIM] / acc[:, HEAD_DIM:HEAD_DIM + 1]).astype(o_ref.dtype)
        for g in range(0, GQA_GROUP, 2):
            h = kv * GQA_GROUP + g
            o_ref[:, h * HEAD_DIM:(h + 2) * HEAD_DIM] = jnp.concatenate(
                [o_all[g * BLOCK:(g + 1) * BLOCK], o_all[(g + 1) * BLOCK:(g + 2) * BLOCK]], axis=1)


def _dsa_attention(hm, bias, tk):
    b, s, _ = hm.shape
    tk = min(tk, s)
    chunk = min(512, s)
    kcol = (Q_DIM + IDX_HEADS * IDX_DIM) // KV_DIM
    rows = GQA_GROUP * BLOCK
    return pl.pallas_call(
        functools.partial(_dsa_attn_kernel, tk=tk, chunk=chunk),
        grid=(b, s // BLOCK),
        in_specs=[pl.BlockSpec((None, BLOCK, Q_DIM), lambda bb, i: (bb, i, 0)),
                  pl.BlockSpec((None, s, KV_DIM), lambda bb, i: (bb, 0, kcol)),
                  pl.BlockSpec((None, s, KV_DIM), lambda bb, i: (bb, 0, kcol + 1)),
                  pl.BlockSpec((None, BLOCK, s), lambda bb, i: (bb, i, 0)),
                  pl.BlockSpec((N_KV, rows, LANES), lambda bb, i: (0, 0, 0))],
        out_specs=pl.BlockSpec((None, BLOCK, Q_DIM), lambda bb, i: (bb, i, 0)),
        out_shape=jax.ShapeDtypeStruct((b, s, Q_DIM), jnp.bfloat16),
        scratch_shapes=[pltpu.VMEM((N_KV, s, LANES), jnp.bfloat16),
                        pltpu.VMEM((N_KV, s, LANES), jnp.bfloat16),
                        pltpu.VMEM((N_KV, rows, LANES), jnp.float32),
                        pltpu.VMEM((N_KV, rows, LANES), jnp.float32)],
        compiler_params=_params("parallel", "arbitrary"),
        name="dsa_attention",
    )(hm, hm, hm, bias, _slope_columns())


def _scale_q_cols(w, n_q, scale):
    return jnp.concatenate([w[:, :n_q] * scale, w[:, n_q:]], axis=1)


def _swa_layer(xb, w_in, sinks, b, s):
    w = _scale_q_cols(w_in, Q_DIM, ATTN_SCALE).astype(jnp.bfloat16)
    h = _matmul(xb, w, jnp.bfloat16, 1024, 1280)
    o = _band_attention(h.reshape(b, s, GRP_IN), SWA_WINDOW - 1, 1, sinks)
    return o.reshape(b * s, Q_DIM)


def _dilated_layer(xb, w_in, b, s):
    outs, lses = [], []
    for g, (window, dil) in enumerate(DIL_PATTERNS):
        w = _scale_q_cols(w_in[:, g * GRP_IN:(g + 1) * GRP_IN], Q_DIM, ATTN_SCALE).astype(jnp.bfloat16)
        xg = xb.reshape(b, s // dil, dil, D_MODEL).transpose(0, 2, 1, 3).reshape(b * s, D_MODEL)
        h = _matmul(xg, w, jnp.bfloat16, 1024, 1280)
        o, lse = _band_attention(h.reshape(b * dil, s // dil, GRP_IN), window // dil, dil)
        unperm = lambda t: t.reshape(b, dil, s // dil, Q_DIM).transpose(0, 2, 1, 3).reshape(b * s, Q_DIM)
        outs.append(unperm(o))
        lses.append(unperm(lse))
    return _merge_groups(outs, lses, 256)


def _dsa_layer(xb, w_in, b, s):
    w_main = jnp.concatenate([w_in[:, :Q_DIM] * (ATTN_SCALE * LOG2E),
                              w_in[:, GRP_IN:C_MAIN] * IDX_DIM ** -0.5,
                              w_in[:, Q_DIM:GRP_IN]], axis=1)
    w_small = jnp.concatenate([w_in[:, C_MAIN:C_MAIN + IDX_DIM],
                               w_in[:, C_MAIN + IDX_DIM:] * IDX_HEADS ** -0.5,
                               jnp.zeros((D_MODEL, LANES - IDX_DIM - IDX_HEADS), w_in.dtype)], axis=1)
    hm = _matmul(xb, w_main.astype(jnp.bfloat16), jnp.bfloat16, 1024, 896).reshape(b, s, C_MAIN)
    kiwi = _matmul(xb, w_small.astype(jnp.bfloat16), jnp.float32, 1024, LANES).reshape(b, s, LANES)
    bias = _dsa_select(hm, kiwi, 256, 256, 512)
    o = _dsa_attention(hm, bias, 256)
    return o.reshape(b * s, Q_DIM)


def kernel(x, a_w_in, a_sinks, a_w_out, b_w_in, b_w_out, c_w_in, c_w_out, ln_g, ln_b,
           ffn_w_gate_up, ffn_w_down):
    b, s, d = x.shape
    xf = x.reshape(b * s, d)
    xb = xf.astype(jnp.bfloat16)
    for i in range(DEPTH):
        kind, j = i % N_MIXERS, i // N_MIXERS
        if kind == 0:
            o = _swa_layer(xb, a_w_in[j], a_sinks[j], b, s)
            w_out = a_w_out[j]
        elif kind == 1:
            o = _dilated_layer(xb, b_w_in[j], b, s)
            w_out = b_w_out[j]
        else:
            o = _dsa_layer(xb, c_w_in[j], b, s)
            w_out = c_w_out[j]
        xf, xb = _proj_residual_ln(o, w_out.astype(jnp.bfloat16), xf, ln_g[i, 0], ln_b[i, 0], 512, 2048)
        hmid = _ffn_gate_up(xb, ffn_w_gate_up[i].astype(jnp.bfloat16), 1024, 512)
        xf, xb = _proj_residual_ln(hmid, ffn_w_down[i].astype(jnp.bfloat16), xf,
                                   ln_g[i, 1], ln_b[i, 1], 512, 1408)
    return xf.reshape(b, s, d)
```

```python
import functools
import math

import numpy as np
import jax
import jax.numpy as jnp
from jax import lax
from jax.experimental import pallas as pl
from jax.experimental.pallas import tpu as pltpu

D_MODEL = 2048
DEPTH = 4
N_MIXERS = 3
HEAD_DIM = 64
N_HEADS = D_MODEL // HEAD_DIM
N_KV = N_HEADS // 8
GQA_GROUP = N_HEADS // N_KV
Q_DIM = N_HEADS * HEAD_DIM
KV_DIM = N_KV * HEAD_DIM
ATTN_SCALE = HEAD_DIM ** -0.5
BLOCK = 128
SWA_WINDOW = 128
DIL_PATTERNS = ((128, 1), (512, 4), (2048, 16))
N_DIL = len(DIL_PATTERNS)
IDX_HEADS = 16
IDX_DIM = 64
TOPK_MAX = 256
D_FF = 256 * math.ceil(8 * D_MODEL / (3 * 256))
DEEPNORM_ALPHA = (2 * DEPTH) ** 0.25
LN_EPS = 1e-5
GRP_IN = Q_DIM + 2 * KV_DIM
C_MAIN = Q_DIM + 2 * KV_DIM + IDX_HEADS * IDX_DIM
C_IN = C_MAIN + IDX_DIM + IDX_HEADS
LANES = 128
MASKED = -1e30
INT_MIN = -2 ** 31
LOG2E = 1.4426950408889634
VMEM_LIMIT = 56 * 1024 * 1024

_SLOPES = [float(np.float32(2.0 ** (-8.0 * (i + 1) / N_HEADS))) for i in range(N_HEADS)]
N_SLOPE_PIECES = 3
POS_SPLIT = 64
BAND_UNIT_HEADS = 8
BAND_AHEAD = 2

def _params(*sem):
    return pltpu.CompilerParams(dimension_semantics=sem, vmem_limit_bytes=VMEM_LIMIT)


def _mm_kernel(x_ref, w_ref, sc_ref, o_ref, wb_ref):
    @pl.when(pl.program_id(1) == 0)
    def _():
        wb_ref[...] = (w_ref[...] * sc_ref[...]).astype(jnp.bfloat16)

    o_ref[...] = jnp.dot(x_ref[...], wb_ref[...],
                         preferred_element_type=jnp.float32).astype(o_ref.dtype)


def _matmul(x, w, layer, col0, scale, out_dtype, tm, tn):
    m, k = x.shape
    n = scale.shape[0]
    tm, tn = min(tm, m), min(tn, n)
    assert m % tm == 0 and n % tn == 0 and col0 % tn == 0
    j0 = col0 // tn
    return pl.pallas_call(
        _mm_kernel,
        grid=(n // tn, m // tm),
        in_specs=[pl.BlockSpec((tm, k), lambda j, i: (i, 0)),
                  pl.BlockSpec((None, k, tn), lambda j, i: (layer, 0, j0 + j)),
                  pl.BlockSpec((1, tn), lambda j, i: (0, j))],
        out_specs=pl.BlockSpec((tm, tn), lambda j, i: (i, j)),
        out_shape=jax.ShapeDtypeStruct((m, n), out_dtype),
        scratch_shapes=[pltpu.VMEM((k, tn), jnp.bfloat16)],
        compiler_params=_params("parallel", "arbitrary"),
        name="proj_matmul",
    )(x, w, jnp.asarray(scale, jnp.float32).reshape(1, n))


def _gate_up_kernel(x_ref, wg_ref, wu_ref, o_ref, wgb_ref, wub_ref):
    @pl.when(pl.program_id(1) == 0)
    def _():
        wgb_ref[...] = wg_ref[...].astype(jnp.bfloat16)
        wub_ref[...] = wu_ref[...].astype(jnp.bfloat16)

    x = x_ref[...]
    g = jnp.dot(x, wgb_ref[...], preferred_element_type=jnp.float32)
    u = jnp.dot(x, wub_ref[...], preferred_element_type=jnp.float32)
    o_ref[...] = (g * (1.0 / (1.0 + jnp.exp(-g))) * u).astype(o_ref.dtype)


def _ffn_gate_up(x, w, layer, tm, tn):
    m, k = x.shape
    d_ff = w.shape[2] // 2
    tm = min(tm, m)
    assert m % tm == 0 and d_ff % tn == 0
    nj = d_ff // tn
    return pl.pallas_call(
        _gate_up_kernel,
        grid=(nj, m // tm),
        in_specs=[pl.BlockSpec((tm, k), lambda j, i: (i, 0)),
                  pl.BlockSpec((None, k, tn), lambda j, i: (layer, 0, j)),
                  pl.BlockSpec((None, k, tn), lambda j, i: (layer, 0, j + nj))],
        out_specs=pl.BlockSpec((tm, tn), lambda j, i: (i, j)),
        out_shape=jax.ShapeDtypeStruct((m, d_ff), jnp.bfloat16),
        scratch_shapes=[pltpu.VMEM((k, tn), jnp.bfloat16), pltpu.VMEM((k, tn), jnp.bfloat16)],
        compiler_params=_params("parallel", "arbitrary"),
        name="ffn_gate_up",
    )(x, w, w)


def _proj_ln_kernel(a_ref, w_ref, r_ref, g_ref, b_ref, o_ref, ob_ref, acc_ref):
    kk = pl.program_id(1)
    part = jnp.dot(a_ref[...], w_ref[...], preferred_element_type=jnp.float32)

    @pl.when(kk == 0)
    def _():
        acc_ref[...] = part

    @pl.when(kk > 0)
    def _():
        acc_ref[...] += part

    @pl.when(kk == pl.num_programs(1) - 1)
    def _():
        z = DEEPNORM_ALPHA * r_ref[...] + acc_ref[...]
        mu = jnp.mean(z, axis=-1, keepdims=True)
        zc = z - mu
        var = jnp.mean(zc * zc, axis=-1, keepdims=True)
        y = zc * lax.rsqrt(var + LN_EPS) * g_ref[...] + b_ref[...]
        o_ref[...] = y
        ob_ref[...] = y.astype(jnp.bfloat16)


def _proj_residual_ln(a, w, resid, g, b, tm, tk):
    m, k = a.shape
    n = w.shape[1]
    tm, tk = min(tm, m), min(tk, k)
    assert m % tm == 0 and k % tk == 0
    return pl.pallas_call(
        _proj_ln_kernel,
        grid=(m // tm, k // tk),
        in_specs=[pl.BlockSpec((tm, tk), lambda i, kk: (i, kk)),
                  pl.BlockSpec((tk, n), lambda i, kk: (kk, 0)),
                  pl.BlockSpec((tm, n), lambda i, kk: (i, 0)),
                  pl.BlockSpec((1, n), lambda i, kk: (0, 0)),
                  pl.BlockSpec((1, n), lambda i, kk: (0, 0))],
        out_specs=[pl.BlockSpec((tm, n), lambda i, kk: (i, 0)),
                   pl.BlockSpec((tm, n), lambda i, kk: (i, 0))],
        out_shape=[jax.ShapeDtypeStruct((m, n), jnp.float32),
                   jax.ShapeDtypeStruct((m, n), jnp.bfloat16)],
        scratch_shapes=[pltpu.VMEM((tm, n), jnp.float32)],
        compiler_params=_params("parallel", "arbitrary"),
        name="proj_residual_ln",
    )(a, w, resid, g.reshape(1, n), b.reshape(1, n))


def _slope_columns(mult, n_parts):
    out = np.zeros((N_KV, GQA_GROUP * BLOCK, LANES), np.float32)
    for h in range(N_HEADS):
        pieces = [float(p) for p in _bf16_pieces(np.float64(_SLOPES[h]) * mult * LOG2E, N_SLOPE_PIECES)]
        kv, g = divmod(h, GQA_GROUP)
        out[kv, g * BLOCK:(g + 1) * BLOCK, HEAD_DIM:HEAD_DIM + n_parts * N_SLOPE_PIECES] = np.asarray(
            pieces * n_parts, np.float32)
    return jnp.asarray(out, jnp.bfloat16)


def _bf16_pieces(x, n):
    rest = np.asarray(x, np.float64)
    pieces = []
    for _ in range(n):
        p = rest.astype(np.float32).astype(jnp.bfloat16).astype(np.float64)
        pieces.append(p)
        rest = rest - p
    return pieces


def _band_q_columns(dilation):
    out = np.zeros((N_KV, GQA_GROUP * BLOCK, LANES), np.float32)
    qpos = BLOCK + np.arange(BLOCK, dtype=np.float64)
    for h in range(N_HEADS):
        s2 = np.float64(_SLOPES[h]) * dilation * LOG2E
        kv, g = divmod(h, GQA_GROUP)
        rows = slice(g * BLOCK, (g + 1) * BLOCK)
        for c, p in enumerate(_bf16_pieces(s2, N_SLOPE_PIECES)):
            out[kv, rows, HEAD_DIM + c] = p
        for c, p in enumerate(_bf16_pieces(-s2 * qpos, N_SLOPE_PIECES)):
            out[kv, rows, HEAD_DIM + N_SLOPE_PIECES + c] = p
    return jnp.asarray(out, jnp.bfloat16)


def _band_k_columns():
    out = np.zeros((2 * BLOCK, LANES), np.float32)
    out[:, HEAD_DIM:HEAD_DIM + N_SLOPE_PIECES] = np.arange(2 * BLOCK, dtype=np.float32)[:, None]
    out[:, HEAD_DIM + N_SLOPE_PIECES:HEAD_DIM + 2 * N_SLOPE_PIECES] = 1.0
    return jnp.asarray(out, jnp.bfloat16)


def _band_kernel(*refs, max_j, has_sink, unit_heads, ahead):
    if has_sink:
        sink_ref, q_ref, kp_ref, kc_ref, vp_ref, vc_ref, qc_ref, kc_cols_ref, o_ref = refs
    else:
        q_ref, kp_ref, kc_ref, vp_ref, vc_ref, qc_ref, kc_cols_ref, o_ref, lse_ref = refs
    i = pl.program_id(1)
    k = jnp.concatenate([kp_ref[...], kc_ref[...]], axis=0)
    v = jnp.concatenate([vp_ref[...], vc_ref[...]], axis=0)
    qpos = lax.broadcasted_iota(jnp.int32, (BLOCK, 2 * BLOCK), 0) + BLOCK
    kpos = lax.broadcasted_iota(jnp.int32, (BLOCK, 2 * BLOCK), 1)
    j = qpos - kpos
    valid = (j >= 0) & (j <= max_j) & ((kpos >= BLOCK) | (i > 0))
    bias = jnp.where(valid, 0.0, MASKED)
    low = lax.broadcasted_iota(jnp.int32, (BLOCK, LANES), 1) < HEAD_DIM
    k_cols = kc_cols_ref[:, HEAD_DIM:]
    ones64 = jnp.ones((2 * BLOCK, HEAD_DIM), jnp.bfloat16)
    ones128 = jnp.ones((2 * BLOCK, LANES), jnp.bfloat16)
    kas = [jnp.concatenate([k[:, kv * HEAD_DIM:(kv + 1) * HEAD_DIM], k_cols], axis=1)
           for kv in range(N_KV)]
    n_units = N_HEADS // unit_heads

    def scores(u):
        h0 = u * unit_heads
        kv, r0 = divmod(h0, GQA_GROUP)
        qa = jnp.concatenate(
            [jnp.concatenate([q_ref[:, (h0 + g) * HEAD_DIM:(h0 + g + 1) * HEAD_DIM]
                              for g in range(unit_heads)], axis=0),
             qc_ref[kv][r0 * BLOCK:(r0 + unit_heads) * BLOCK, HEAD_DIM:]], axis=1)
        return lax.dot_general(qa, kas[kv], (((1,), (1,)), ((), ())),
                               preferred_element_type=jnp.float32)

    pending = [scores(u) for u in range(min(ahead, n_units))]
    for u in range(n_units):
        s_all = pending.pop(0)
        if u + ahead < n_units:
            pending.append(scores(u + ahead))
        h0 = u * unit_heads
        kv = h0 // GQA_GROUP
        v_kv = v[:, kv * HEAD_DIM:(kv + 1) * HEAD_DIM]
        v_even = jnp.concatenate([v_kv, ones64, ones128], axis=1)
        v_odd = jnp.concatenate([ones64, v_kv, ones128], axis=1)
        ps, ms = [], []
        for g in range(unit_heads):
            s = s_all[g * BLOCK:(g + 1) * BLOCK] + bias
            m = jnp.max(s, axis=-1, keepdims=True)
            if has_sink:
                m = jnp.maximum(m, sink_ref[h0 + g] * LOG2E)
            ps.append(jnp.exp2(s - m).astype(jnp.bfloat16))
            ms.append(m)
        acc_e = jnp.dot(jnp.concatenate(ps[0::2], axis=0), v_even,
                        preferred_element_type=jnp.float32)
        acc_o = jnp.dot(jnp.concatenate(ps[1::2], axis=0), v_odd,
                        preferred_element_type=jnp.float32)
        for t in range(unit_heads // 2):
            h = h0 + 2 * t
            rows = slice(t * BLOCK, (t + 1) * BLOCK)
            num = jnp.where(low, acc_e[rows, :LANES], acc_o[rows, :LANES])
            den = jnp.where(low, acc_e[rows, LANES:], acc_o[rows, LANES:])
            m2 = jnp.where(low, jnp.broadcast_to(ms[2 * t], (BLOCK, LANES)),
                           jnp.broadcast_to(ms[2 * t + 1], (BLOCK, LANES)))
            if has_sink:
                sink2 = jnp.where(low, sink_ref[h] * LOG2E, sink_ref[h + 1] * LOG2E)
                den = den + jnp.exp2(sink2 - m2)
            cols = slice(h * HEAD_DIM, (h + 2) * HEAD_DIM)
            o_ref[:, cols] = (num / den).astype(o_ref.dtype)
            if not has_sink:
                lse_ref[:, cols] = m2 * (1.0 / LOG2E) + jnp.log(den)


def _band_attention(h, max_j, dilation, sinks=None):
    r, l, _ = h.shape
    nb = l // BLOCK
    kcol, vcol = Q_DIM // KV_DIM, Q_DIM // KV_DIM + 1
    has_sink = sinks is not None
    rows = GQA_GROUP * BLOCK
    in_specs = [pl.BlockSpec((None, BLOCK, Q_DIM), lambda s, i: (s, i, 0)),
                pl.BlockSpec((None, BLOCK, KV_DIM), lambda s, i: (s, jnp.maximum(i - 1, 0), kcol)),
                pl.BlockSpec((None, BLOCK, KV_DIM), lambda s, i: (s, i, kcol)),
                pl.BlockSpec((None, BLOCK, KV_DIM), lambda s, i: (s, jnp.maximum(i - 1, 0), vcol)),
                pl.BlockSpec((None, BLOCK, KV_DIM), lambda s, i: (s, i, vcol)),
                pl.BlockSpec((N_KV, rows, LANES), lambda s, i: (0, 0, 0)),
                pl.BlockSpec((2 * BLOCK, LANES), lambda s, i: (0, 0))]
    o_spec = pl.BlockSpec((None, BLOCK, Q_DIM), lambda s, i: (s, i, 0))
    args = [h, h, h, h, h, _band_q_columns(dilation), _band_k_columns()]
    if has_sink:
        in_specs = [pl.BlockSpec(memory_space=pltpu.SMEM)] + in_specs
        args = [sinks.astype(jnp.float32)] + args
        out_specs = o_spec
        out_shape = jax.ShapeDtypeStruct((r, l, Q_DIM), jnp.bfloat16)
    else:
        out_specs = [o_spec, o_spec]
        out_shape = [jax.ShapeDtypeStruct((r, l, Q_DIM), jnp.float32),
                     jax.ShapeDtypeStruct((r, l, Q_DIM), jnp.float32)]
    return pl.pallas_call(
        functools.partial(_band_kernel, max_j=max_j, has_sink=has_sink, unit_heads=BAND_UNIT_HEADS,
                          ahead=BAND_AHEAD),
        grid=(r, nb),
        in_specs=in_specs,
        out_specs=out_specs,
        out_shape=out_shape,
        compiler_params=_params("parallel", "parallel"),
        name="band_attention",
    )(*args)


def _merge_kernel(o0, o1, o2, l0, l1, l2, out_ref):
    a, b, c = l0[...], l1[...], l2[...]
    m = jnp.maximum(jnp.maximum(a, b), c)
    ea, eb, ec = jnp.exp(a - m), jnp.exp(b - m), jnp.exp(c - m)
    num = ea * o0[...] + eb * o1[...] + ec * o2[...]
    out_ref[...] = (num / (ea + eb + ec)).astype(out_ref.dtype)


def _merge_groups(outs, lses, tm):
    m, n = outs[0].shape
    tm = min(tm, m)
    spec = pl.BlockSpec((tm, n), lambda i: (i, 0))
    return pl.pallas_call(
        _merge_kernel,
        grid=(m // tm,),
        in_specs=[spec] * 6,
        out_specs=spec,
        out_shape=jax.ShapeDtypeStruct((m, n), jnp.bfloat16),
        compiler_params=_params("parallel"),
        name="merge_groups",
    )(*outs, *lses)


def _count_tiles(n_tiles, tile_hits, tq, tcnt):
    def body(t, acc):
        hit = tile_hits(pl.multiple_of(t * tcnt, tcnt)).astype(jnp.float32)
        part = hit[:, :LANES]
        for c in range(1, tcnt // LANES):
            part = part + hit[:, c * LANES:(c + 1) * LANES]
        return acc + part
    acc = lax.fori_loop(0, n_tiles, body, jnp.zeros((tq, LANES), jnp.float32))
    return jnp.sum(acc, axis=-1, keepdims=True)


def _dsa_select_kernel(qia_ref, qib_ref, ki_ref, wi_ref, bias_ref, key_ref, *, top_k, tq, tsc, tcnt):
    i = pl.program_id(1)
    s_len = key_ref.shape[1]
    n_keys = (i + 1) * tq
    n_sc = n_keys // tsc
    n_cnt = (n_keys + tcnt - 1) // tcnt
    rowpos = lax.broadcasted_iota(jnp.int32, (tq, 1), 0) + i * tq
    col_sc = lax.broadcasted_iota(jnp.int32, (tq, tsc), 1)
    col_cnt = lax.broadcasted_iota(jnp.int32, (tq, tcnt), 1)

    key_ref[:, pl.ds(pl.multiple_of(n_cnt * tcnt - tsc, tsc), tsc)] = jnp.full((tq, tsc), INT_MIN, jnp.int32)

    half = IDX_HEADS // 2
    qis = jnp.concatenate([ref[:, h * IDX_DIM:(h + 1) * IDX_DIM]
                           for ref in (qia_ref, qib_ref) for h in range(half)], axis=0)
    wi = wi_ref[...]

    def score_body(t, carry):
        start = pl.multiple_of(t * tsc, tsc)
        ki = ki_ref[pl.ds(start, tsc), :][:, :IDX_DIM].astype(jnp.bfloat16)
        rel = lax.dot_general(qis, ki, (((1,), (1,)), ((), ())),
                              preferred_element_type=jnp.float32)
        sc = jnp.zeros((tq, tsc), jnp.float32)
        for h in range(IDX_HEADS):
            w_h = wi[:, IDX_DIM + h:IDX_DIM + h + 1]
            sc = sc + jnp.maximum(rel[h * tq:(h + 1) * tq], 0.0) * w_h
        sc = sc + 0.0
        bits = pltpu.bitcast(sc, jnp.int32)
        keys = jnp.where(bits < 0, bits ^ jnp.int32(0x7FFFFFFF), bits)
        keys = jnp.where(col_sc + start <= rowpos, keys, jnp.int32(INT_MIN))
        key_ref[:, pl.ds(start, tsc)] = keys
        return carry

    lax.fori_loop(0, n_sc, score_body, 0)

    kf = jnp.float32(top_k)

    def count_ge(cand):
        return _count_tiles(n_cnt, lambda st: key_ref[:, pl.ds(st, tcnt)] >= cand, tq, tcnt)

    short = (rowpos + 1 < top_k).astype(jnp.float32)

    def bit_cond(c):
        b, _, done, _ = c
        return (b < 32) & (jnp.min(done) < 0.5)

    def bit_body(c):
        b, prefix, done, thr = c
        cand = prefix ^ lax.shift_left(jnp.int32(1), jnp.int32(31) - b)
        cnt = count_ge(cand)
        prefix = jnp.where(cnt >= kf, cand, prefix)
        hit = (cnt == kf) & (done < 0.5)
        thr = jnp.where(hit, cand - 1, thr)
        done = jnp.where(hit, 1.0, done)
        return b + 1, prefix, done, thr

    init = (jnp.int32(0), jnp.full((tq, 1), INT_MIN, jnp.int32), short,
            jnp.full((tq, 1), INT_MIN, jnp.int32))
    _, prefix, done, thr = lax.while_loop(bit_cond, bit_body, init)
    is_done = done > 0.5
    thr = jnp.where(is_done, thr, prefix)

    def tie_search(_):
        n_gt = _count_tiles(n_cnt, lambda st: key_ref[:, pl.ds(st, tcnt)] > thr, tq, tcnt)
        need = kf - n_gt
        n_idx_bits = max(1, int(math.ceil(math.log2(s_len))))

        def idx_body(b, p):
            cand = p | lax.shift_left(jnp.int32(1), jnp.int32(n_idx_bits - 1) - b)
            below = _count_tiles(
                n_cnt, lambda st: (key_ref[:, pl.ds(st, tcnt)] == thr) & (col_cnt + st < cand), tq, tcnt)
            return jnp.where(below <= need - 1.0, cand, p)

        return lax.fori_loop(0, n_idx_bits, idx_body, jnp.zeros((tq, 1), jnp.int32))

    last_tie = lax.cond(jnp.min(done) < 0.5, tie_search,
                        lambda _: jnp.full((tq, 1), -1, jnp.int32), 0)
    last_tie = jnp.where(is_done, -1, last_tie)

    def write_body(t, carry):
        start = pl.multiple_of(t * tcnt, tcnt)
        kt = key_ref[:, pl.ds(start, tcnt)]
        col = col_cnt + start
        sel = ((kt > thr) | ((kt == thr) & (col <= last_tie))) & (col <= rowpos)
        bias_ref[:, pl.ds(start, tcnt)] = jnp.where(sel, 0.0, MASKED).astype(bias_ref.dtype)
        return carry

    lax.fori_loop(0, n_cnt, write_body, 0)

    def fill_body(t, carry):
        bias_ref[:, pl.ds(pl.multiple_of(t * tcnt, tcnt), tcnt)] = jnp.full(
            (tq, tcnt), MASKED, bias_ref.dtype)
        return carry

    lax.fori_loop(n_cnt, s_len // tcnt, fill_body, 0)


def _dsa_select(hm, kiwi, tq, tsc, tcnt):
    b, s, _ = hm.shape
    tq, tsc, tcnt = min(tq, s), min(tsc, s), min(tcnt, s)
    assert tq % tsc == 0 and tcnt % tsc == 0 and s % tcnt == 0 and s % tq == 0
    top_k = min(TOPK_MAX, s // 4)
    qi_half = IDX_HEADS * IDX_DIM // 2
    qicol = GRP_IN // qi_half
    return pl.pallas_call(
        functools.partial(_dsa_select_kernel, top_k=top_k, tq=tq, tsc=tsc, tcnt=tcnt),
        grid=(b, s // tq),
        in_specs=[pl.BlockSpec((None, tq, qi_half), lambda bb, i: (bb, i, qicol)),
                  pl.BlockSpec((None, tq, qi_half), lambda bb, i: (bb, i, qicol + 1)),
                  pl.BlockSpec((None, s, LANES), lambda bb, i: (bb, 0, 0)),
                  pl.BlockSpec((None, tq, LANES), lambda bb, i: (bb, i, 0))],
        out_specs=pl.BlockSpec((None, tq, s), lambda bb, i: (bb, i, 0)),
        out_shape=jax.ShapeDtypeStruct((b, s, s), jnp.bfloat16),
        scratch_shapes=[pltpu.VMEM((tq, s), jnp.int32)],
        compiler_params=_params("parallel", "parallel"),
        name="dsa_select",
    )(hm, hm, kiwi, kiwi)


def _dsa_attn_kernel(q_ref, k_ref, v_ref, bias_ref, sl_ref, o_ref,
                     kaug_ref, vaug_ref, m_ref, acc_ref, *, tk, chunk):
    i = pl.program_id(1)
    s_len = k_ref.shape[0]

    @pl.when(i == 0)
    def _():
        lane = lax.broadcasted_iota(jnp.int32, (chunk, HEAD_DIM), 1)
        ones_col = (lane == 0).astype(jnp.bfloat16)
        for c in range(s_len // chunk):
            pos = lax.broadcasted_iota(jnp.int32, (chunk, HEAD_DIM), 0) + c * chunk
            hi = pos - (pos & (POS_SPLIT - 1))
            lo = pos & (POS_SPLIT - 1)
            pcols = jnp.where(lane < N_SLOPE_PIECES, hi, jnp.where(lane < 2 * N_SLOPE_PIECES, lo, 0))
            pcols = pcols.astype(jnp.float32).astype(jnp.bfloat16)
            rows = pl.ds(c * chunk, chunk)
            for kv in range(N_KV):
                kaug_ref[kv, rows, :] = jnp.concatenate(
                    [k_ref[rows, kv * HEAD_DIM:(kv + 1) * HEAD_DIM], pcols], axis=1)
                vaug_ref[kv, rows, :] = jnp.concatenate(
                    [v_ref[rows, kv * HEAD_DIM:(kv + 1) * HEAD_DIM], ones_col], axis=1)

    n_tiles = (i * BLOCK + BLOCK + tk - 1) // tk
    m_ref[...] = jnp.full(m_ref.shape, MASKED, jnp.float32)
    acc_ref[...] = jnp.zeros(acc_ref.shape, jnp.float32)
    qa = [jnp.concatenate(
        [jnp.concatenate(
            [q_ref[:, (kv * GQA_GROUP + g) * HEAD_DIM:(kv * GQA_GROUP + g + 1) * HEAD_DIM]
             for g in range(GQA_GROUP)], axis=0), sl_ref[kv][:, HEAD_DIM:]], axis=1)
        for kv in range(N_KV)]
    n_rep = tk // LANES

    def attn_body(t, carry):
        start = pl.multiple_of(t * tk, tk)
        bias = bias_ref[:, pl.ds(start, tk)].astype(jnp.float32)
        for kv in range(N_KV):
            s_all = lax.dot_general(qa[kv], kaug_ref[kv, pl.ds(start, tk), :],
                                    (((1,), (1,)), ((), ())),
                                    preferred_element_type=jnp.float32)
            m_old = m_ref[kv]
            ps, m_news = [], []
            for g in range(GQA_GROUP):
                s = s_all[g * BLOCK:(g + 1) * BLOCK] + bias
                m_new = jnp.maximum(m_old[g * BLOCK:(g + 1) * BLOCK],
                                    jnp.max(s, axis=-1, keepdims=True))
                ps.append(jnp.exp2(s - jnp.concatenate([m_new] * n_rep, axis=1)).astype(jnp.bfloat16))
                m_news.append(m_new)
            m_new = jnp.concatenate(m_news, axis=0)
            pv = jnp.dot(jnp.concatenate(ps, axis=0), vaug_ref[kv, pl.ds(start, tk), :],
                         preferred_element_type=jnp.float32)
            acc_ref[kv] = jnp.exp2(m_old - m_new) * acc_ref[kv] + pv
            m_ref[kv] = m_new
        return carry

    lax.fori_loop(0, n_tiles, attn_body, 0)

    for kv in range(N_KV):
        acc = acc_ref[kv]
        o_all = (acc[:, :HEAD_DIM] / acc[:, HEAD_DIM:HEAD_DIM + 1]).astype(o_ref.dtype)
        for g in range(0, GQA_GROUP, 2):
            h = kv * GQA_GROUP + g
            o_ref[:, h * HEAD_DIM:(h + 2) * HEAD_DIM] = jnp.concatenate(
                [o_all[g * BLOCK:(g + 1) * BLOCK], o_all[(g + 1) * BLOCK:(g + 2) * BLOCK]], axis=1)


def _dsa_attention(hm, bias, tk):
    b, s, _ = hm.shape
    tk = min(tk, s)
    chunk = min(512, s)
    kcol = Q_DIM // KV_DIM
    rows = GQA_GROUP * BLOCK
    return pl.pallas_call(
        functools.partial(_dsa_attn_kernel, tk=tk, chunk=chunk),
        grid=(b, s // BLOCK),
        in_specs=[pl.BlockSpec((None, BLOCK, Q_DIM), lambda bb, i: (bb, i, 0)),
                  pl.BlockSpec((None, s, KV_DIM), lambda bb, i: (bb, 0, kcol)),
                  pl.BlockSpec((None, s, KV_DIM), lambda bb, i: (bb, 0, kcol + 1)),
                  pl.BlockSpec((None, BLOCK, s), lambda bb, i: (bb, i, 0)),
                  pl.BlockSpec((N_KV, rows, LANES), lambda bb, i: (0, 0, 0))],
        out_specs=pl.BlockSpec((None, BLOCK, Q_DIM), lambda bb, i: (bb, i, 0)),
        out_shape=jax.ShapeDtypeStruct((b, s, Q_DIM), jnp.bfloat16),
        scratch_shapes=[pltpu.VMEM((N_KV, s, LANES), jnp.bfloat16),
                        pltpu.VMEM((N_KV, s, LANES), jnp.bfloat16),
                        pltpu.VMEM((N_KV, rows, LANES), jnp.float32),
                        pltpu.VMEM((N_KV, rows, LANES), jnp.float32)],
        compiler_params=_params("parallel", "arbitrary"),
        name="dsa_attention",
    )(hm, hm, hm, bias, _slope_columns(1, 2))


_QKV_SCALE = np.concatenate([np.full(Q_DIM, ATTN_SCALE * LOG2E), np.ones(2 * KV_DIM)])
_DSA_SCALE = np.concatenate([_QKV_SCALE, np.full(IDX_HEADS * IDX_DIM, IDX_DIM ** -0.5)])


def _swa_layer(xb, w_in, layer, sinks, b, s):
    h = _matmul(xb, w_in, layer, 0, _QKV_SCALE, jnp.bfloat16, 1024, 1280)
    o = _band_attention(h.reshape(b, s, GRP_IN), SWA_WINDOW - 1, 1, sinks)
    return o.reshape(b * s, Q_DIM)


def _dilated_layer(xb, w_in, layer, b, s):
    outs, lses = [], []
    for g, (window, dil) in enumerate(DIL_PATTERNS):
        xg = xb.reshape(b, s // dil, dil, D_MODEL).transpose(0, 2, 1, 3).reshape(b * s, D_MODEL)
        h = _matmul(xg, w_in, layer, g * GRP_IN, _QKV_SCALE, jnp.bfloat16, 1024, 1280)
        o, lse = _band_attention(h.reshape(b * dil, s // dil, GRP_IN), window // dil, dil)
        unperm = lambda t: t.reshape(b, dil, s // dil, Q_DIM).transpose(0, 2, 1, 3).reshape(b * s, Q_DIM)
        outs.append(unperm(o))
        lses.append(unperm(lse))
    return _merge_groups(outs, lses, 256)


def _dsa_layer(xb, w_in, layer, b, s):
    hm = _matmul(xb, w_in, layer, 0, _DSA_SCALE, jnp.bfloat16, 1024, 896).reshape(b, s, C_MAIN)
    w_small = jnp.concatenate([w_in[layer, :, C_MAIN:C_MAIN + IDX_DIM],
                               w_in[layer, :, C_MAIN + IDX_DIM:] * IDX_HEADS ** -0.5,
                               jnp.zeros((D_MODEL, LANES - IDX_DIM - IDX_HEADS), w_in.dtype)], axis=1)
    kiwi = _matmul(xb, w_small[None], 0, 0, np.ones(LANES), jnp.float32, 1024, LANES).reshape(b, s, LANES)
    bias = _dsa_select(hm, kiwi, 256, 256, 512)
    o = _dsa_attention(hm, bias, 256)
    return o.reshape(b * s, Q_DIM)


def kernel(x, a_w_in, a_sinks, a_w_out, b_w_in, b_w_out, c_w_in, c_w_out, ln_g, ln_b,
           ffn_w_gate_up, ffn_w_down):
    b, s, d = x.shape
    xf = x.reshape(b * s, d)
    xb = xf.astype(jnp.bfloat16)
    for i in range(DEPTH):
        kind, j = i % N_MIXERS, i // N_MIXERS
        if kind == 0:
            o = _swa_layer(xb, a_w_in, j, a_sinks[j], b, s)
            w_out = a_w_out[j]
        elif kind == 1:
            o = _dilated_layer(xb, b_w_in, j, b, s)
            w_out = b_w_out[j]
        else:
            o = _dsa_layer(xb, c_w_in, j, b, s)
            w_out = c_w_out[j]
        xf, xb = _proj_residual_ln(o, w_out.astype(jnp.bfloat16), xf, ln_g[i, 0], ln_b[i, 0], 512, 2048)
        hmid = _ffn_gate_up(xb, ffn_w_gate_up, i, 1024, 512)
        xf, xb = _proj_residual_ln(hmid, ffn_w_down[i].astype(jnp.bfloat16), xf,
                                   ln_g[i, 1], ln_b[i, 1], 512, 1408)
    return xf.reshape(b, s, d)
```

```python
import functools
import math

import numpy as np
import jax
import jax.numpy as jnp
from jax import lax
from jax.experimental import pallas as pl
from jax.experimental.pallas import tpu as pltpu

D_MODEL = 2048
DEPTH = 4
N_MIXERS = 3
HEAD_DIM = 64
N_HEADS = D_MODEL // HEAD_DIM
N_KV = N_HEADS // 8
GQA_GROUP = N_HEADS // N_KV
Q_DIM = N_HEADS * HEAD_DIM
KV_DIM = N_KV * HEAD_DIM
ATTN_SCALE = HEAD_DIM ** -0.5
BLOCK = 128
SWA_WINDOW = 128
DIL_PATTERNS = ((128, 1), (512, 4), (2048, 16))
N_DIL = len(DIL_PATTERNS)
IDX_HEADS = 16
IDX_DIM = 64
TOPK_MAX = 256
D_FF = 256 * math.ceil(8 * D_MODEL / (3 * 256))
DEEPNORM_ALPHA = (2 * DEPTH) ** 0.25
LN_EPS = 1e-5
GRP_IN = Q_DIM + 2 * KV_DIM
C_MAIN = Q_DIM + 2 * KV_DIM + IDX_HEADS * IDX_DIM
C_IN = C_MAIN + IDX_DIM + IDX_HEADS
LANES = 128
MASKED = -1e30
INT_MIN = -2 ** 31
LOG2E = 1.4426950408889634
VMEM_LIMIT = 56 * 1024 * 1024

_SLOPES = [float(np.float32(2.0 ** (-8.0 * (i + 1) / N_HEADS))) for i in range(N_HEADS)]
N_SLOPE_PIECES = 3
POS_SPLIT = 64
BAND_UNIT_HEADS = 8
BAND_AHEAD = 2

def _params(*sem):
    return pltpu.CompilerParams(dimension_semantics=sem, vmem_limit_bytes=VMEM_LIMIT)


def _mm_kernel(x_ref, w_ref, sc_ref, o_ref, wb_ref, *acc, dil):
    @pl.when(pl.program_id(1) == 0)
    def _():
        wb_ref[...] = (w_ref[...] * sc_ref[...]).astype(jnp.bfloat16)

    y = jnp.dot(x_ref[...], wb_ref[...], preferred_element_type=jnp.float32)
    if dil == 1:
        o_ref[...] = y.astype(o_ref.dtype)
    else:
        acc_ref, = acc
        per = acc_ref.shape[1] // dil
        for c in range(acc_ref.shape[0]):
            cols = slice(c * LANES, (c + 1) * LANES)
            acc_ref[c] = y[:, cols]
            for r in range(dil):
                o_ref[r, :, cols] = acc_ref[c, pl.ds(r, per, stride=dil), :].astype(o_ref.dtype)


def _matmul(x, w, layer, col0, scale, out_dtype, tm, tn, batch=1, dil=1):
    m, k = x.shape
    n = scale.shape[0]
    tm, tn = min(tm, m // batch), min(tn, n)
    assert m % (batch * tm) == 0 and n % tn == 0 and col0 % tn == 0 and tm % (16 * dil) == 0
    j0 = col0 // tn
    if dil == 1:
        out_specs = pl.BlockSpec((tm, tn), lambda j, i: (i, j))
        out_shape = jax.ShapeDtypeStruct((m, n), out_dtype)
        scratch = []
    else:
        per_batch = m // batch // tm
        out_specs = pl.BlockSpec((None, dil, tm // dil, tn),
                                 lambda j, i: (i // per_batch, 0, i % per_batch, j))
        out_shape = jax.ShapeDtypeStruct((batch, dil, m // batch // dil, n), out_dtype)
        scratch = [pltpu.VMEM((tn // LANES, tm, LANES), jnp.float32)]
    return pl.pallas_call(
        functools.partial(_mm_kernel, dil=dil),
        grid=(n // tn, m // tm),
        in_specs=[pl.BlockSpec((tm, k), lambda j, i: (i, 0)),
                  pl.BlockSpec((None, k, tn), lambda j, i: (layer, 0, j0 + j)),
                  pl.BlockSpec((1, tn), lambda j, i: (0, j))],
        out_specs=out_specs,
        out_shape=out_shape,
        scratch_shapes=[pltpu.VMEM((k, tn), jnp.bfloat16)] + scratch,
        compiler_params=_params("parallel", "arbitrary"),
        name="proj_matmul",
    )(x, w, jnp.asarray(scale, jnp.float32).reshape(1, n))


def _gate_up_kernel(x_ref, wg_ref, wu_ref, o_ref, wgb_ref, wub_ref):
    @pl.when(pl.program_id(1) == 0)
    def _():
        wgb_ref[...] = wg_ref[...].astype(jnp.bfloat16)
        wub_ref[...] = wu_ref[...].astype(jnp.bfloat16)

    x = x_ref[...]
    g = jnp.dot(x, wgb_ref[...], preferred_element_type=jnp.float32)
    u = jnp.dot(x, wub_ref[...], preferred_element_type=jnp.float32)
    o_ref[...] = (g * (1.0 / (1.0 + jnp.exp(-g))) * u).astype(o_ref.dtype)


def _ffn_gate_up(x, w, layer, tm, tn):
    m, k = x.shape
    d_ff = w.shape[2] // 2
    tm = min(tm, m)
    assert m % tm == 0 and d_ff % tn == 0
    nj = d_ff // tn
    return pl.pallas_call(
        _gate_up_kernel,
        grid=(nj, m // tm),
        in_specs=[pl.BlockSpec((tm, k), lambda j, i: (i, 0)),
                  pl.BlockSpec((None, k, tn), lambda j, i: (layer, 0, j)),
                  pl.BlockSpec((None, k, tn), lambda j, i: (layer, 0, j + nj))],
        out_specs=pl.BlockSpec((tm, tn), lambda j, i: (i, j)),
        out_shape=jax.ShapeDtypeStruct((m, d_ff), jnp.bfloat16),
        scratch_shapes=[pltpu.VMEM((k, tn), jnp.bfloat16), pltpu.VMEM((k, tn), jnp.bfloat16)],
        compiler_params=_params("parallel", "arbitrary"),
        name="ffn_gate_up",
    )(x, w, w)


def _proj_ln_kernel(a_ref, w_ref, r_ref, g_ref, b_ref, o_ref, ob_ref, wb_ref, *, n_w, tkc):
    s = pl.program_id(0)

    @pl.when(s < n_w)
    def _():
        wb_ref[pl.ds(pl.multiple_of(s * tkc, tkc), tkc), :] = w_ref[...].astype(jnp.bfloat16)

    @pl.when(s >= n_w)
    def _():
        z = DEEPNORM_ALPHA * r_ref[...] + jnp.dot(a_ref[...], wb_ref[...],
                                                  preferred_element_type=jnp.float32)
        mu = jnp.mean(z, axis=-1, keepdims=True)
        zc = z - mu
        var = jnp.mean(zc * zc, axis=-1, keepdims=True)
        y = zc * lax.rsqrt(var + LN_EPS) * g_ref[...] + b_ref[...]
        o_ref[...] = y
        ob_ref[...] = y.astype(jnp.bfloat16)


def _proj_residual_ln(a, w, layer, resid, g, b, tm, tkc):
    m, k = a.shape
    n = w.shape[2]
    tm, tkc = min(tm, m), min(tkc, k)
    assert m % tm == 0 and k % tkc == 0
    n_w = k // tkc
    row = lambda s: (jnp.maximum(s - n_w, 0), 0)
    return pl.pallas_call(
        functools.partial(_proj_ln_kernel, n_w=n_w, tkc=tkc),
        grid=(n_w + m // tm,),
        in_specs=[pl.BlockSpec((tm, k), row),
                  pl.BlockSpec((None, tkc, n), lambda s: (layer, jnp.minimum(s, n_w - 1), 0)),
                  pl.BlockSpec((tm, n), row),
                  pl.BlockSpec((1, n), lambda s: (0, 0)),
                  pl.BlockSpec((1, n), lambda s: (0, 0))],
        out_specs=[pl.BlockSpec((tm, n), row), pl.BlockSpec((tm, n), row)],
        out_shape=[jax.ShapeDtypeStruct((m, n), jnp.float32),
                   jax.ShapeDtypeStruct((m, n), jnp.bfloat16)],
        scratch_shapes=[pltpu.VMEM((k, n), jnp.bfloat16)],
        compiler_params=_params("arbitrary"),
        name="proj_residual_ln",
    )(a, w, resid, g.reshape(1, n), b.reshape(1, n))


def _slope_columns(mult, n_parts):
    out = np.zeros((N_KV, GQA_GROUP * BLOCK, LANES), np.float32)
    for h in range(N_HEADS):
        pieces = [float(p) for p in _bf16_pieces(np.float64(_SLOPES[h]) * mult * LOG2E, N_SLOPE_PIECES)]
        kv, g = divmod(h, GQA_GROUP)
        out[kv, g * BLOCK:(g + 1) * BLOCK, HEAD_DIM:HEAD_DIM + n_parts * N_SLOPE_PIECES] = np.asarray(
            pieces * n_parts, np.float32)
    return jnp.asarray(out, jnp.bfloat16)


def _bf16_pieces(x, n):
    rest = np.asarray(x, np.float64)
    pieces = []
    for _ in range(n):
        p = rest.astype(np.float32).astype(jnp.bfloat16).astype(np.float64)
        pieces.append(p)
        rest = rest - p
    return pieces


def _band_q_columns(dilation):
    out = np.zeros((N_KV, GQA_GROUP * BLOCK, LANES), np.float32)
    qpos = BLOCK + np.arange(BLOCK, dtype=np.float64)
    for h in range(N_HEADS):
        s2 = np.float64(_SLOPES[h]) * dilation * LOG2E
        kv, g = divmod(h, GQA_GROUP)
        rows = slice(g * BLOCK, (g + 1) * BLOCK)
        for c, p in enumerate(_bf16_pieces(s2, N_SLOPE_PIECES)):
            out[kv, rows, HEAD_DIM + c] = p
        for c, p in enumerate(_bf16_pieces(-s2 * qpos, N_SLOPE_PIECES)):
            out[kv, rows, HEAD_DIM + N_SLOPE_PIECES + c] = p
    return jnp.asarray(out, jnp.bfloat16)


def _band_k_columns():
    out = np.zeros((2 * BLOCK, LANES), np.float32)
    out[:, HEAD_DIM:HEAD_DIM + N_SLOPE_PIECES] = np.arange(2 * BLOCK, dtype=np.float32)[:, None]
    out[:, HEAD_DIM + N_SLOPE_PIECES:HEAD_DIM + 2 * N_SLOPE_PIECES] = 1.0
    return jnp.asarray(out, jnp.bfloat16)


def _band_kernel(*refs, max_j, has_sink, unit_heads, ahead):
    if has_sink:
        sink_ref, q_ref, kp_ref, kc_ref, vp_ref, vc_ref, qc_ref, kc_cols_ref, o_ref = refs
    else:
        q_ref, kp_ref, kc_ref, vp_ref, vc_ref, qc_ref, kc_cols_ref, o_ref, lse_ref = refs
    i = pl.program_id(1)
    k = jnp.concatenate([kp_ref[...], kc_ref[...]], axis=0)
    v = jnp.concatenate([vp_ref[...], vc_ref[...]], axis=0)
    qpos = lax.broadcasted_iota(jnp.int32, (BLOCK, 2 * BLOCK), 0) + BLOCK
    kpos = lax.broadcasted_iota(jnp.int32, (BLOCK, 2 * BLOCK), 1)
    j = qpos - kpos
    valid = (j >= 0) & (j <= max_j) & ((kpos >= BLOCK) | (i > 0))
    bias = jnp.where(valid, 0.0, MASKED)
    low = lax.broadcasted_iota(jnp.int32, (BLOCK, LANES), 1) < HEAD_DIM
    k_cols = kc_cols_ref[:, HEAD_DIM:]
    ones64 = jnp.ones((2 * BLOCK, HEAD_DIM), jnp.bfloat16)
    ones128 = jnp.ones((2 * BLOCK, LANES), jnp.bfloat16)
    kas = [jnp.concatenate([k[:, kv * HEAD_DIM:(kv + 1) * HEAD_DIM], k_cols], axis=1)
           for kv in range(N_KV)]
    n_units = N_HEADS // unit_heads

    def scores(u):
        h0 = u * unit_heads
        kv, r0 = divmod(h0, GQA_GROUP)
        qa = jnp.concatenate(
            [jnp.concatenate([q_ref[:, (h0 + g) * HEAD_DIM:(h0 + g + 1) * HEAD_DIM]
                              for g in range(unit_heads)], axis=0),
             qc_ref[kv][r0 * BLOCK:(r0 + unit_heads) * BLOCK, HEAD_DIM:]], axis=1)
        return lax.dot_general(qa, kas[kv], (((1,), (1,)), ((), ())),
                               preferred_element_type=jnp.float32)

    pending = [scores(u) for u in range(min(ahead, n_units))]
    for u in range(n_units):
        s_all = pending.pop(0)
        if u + ahead < n_units:
            pending.append(scores(u + ahead))
        h0 = u * unit_heads
        kv = h0 // GQA_GROUP
        v_kv = v[:, kv * HEAD_DIM:(kv + 1) * HEAD_DIM]
        v_even = jnp.concatenate([v_kv, ones64, ones128], axis=1)
        v_odd = jnp.concatenate([ones64, v_kv, ones128], axis=1)
        ps, ms = [], []
        for g in range(unit_heads):
            s = s_all[g * BLOCK:(g + 1) * BLOCK] + bias
            m = jnp.max(s, axis=-1, keepdims=True)
            if has_sink:
                m = jnp.maximum(m, sink_ref[h0 + g] * LOG2E)
            ps.append(jnp.exp2(s - m).astype(jnp.bfloat16))
            ms.append(m)
        acc_e = jnp.dot(jnp.concatenate(ps[0::2], axis=0), v_even,
                        preferred_element_type=jnp.float32)
        acc_o = jnp.dot(jnp.concatenate(ps[1::2], axis=0), v_odd,
                        preferred_element_type=jnp.float32)
        for t in range(unit_heads // 2):
            h = h0 + 2 * t
            rows = slice(t * BLOCK, (t + 1) * BLOCK)
            num = jnp.where(low, acc_e[rows, :LANES], acc_o[rows, :LANES])
            den = jnp.where(low, acc_e[rows, LANES:], acc_o[rows, LANES:])
            m2 = jnp.where(low, jnp.broadcast_to(ms[2 * t], (BLOCK, LANES)),
                           jnp.broadcast_to(ms[2 * t + 1], (BLOCK, LANES)))
            if has_sink:
                sink2 = jnp.where(low, sink_ref[h] * LOG2E, sink_ref[h + 1] * LOG2E)
                den = den + jnp.exp2(sink2 - m2)
            cols = slice(h * HEAD_DIM, (h + 2) * HEAD_DIM)
            o_ref[:, cols] = (num / den).astype(o_ref.dtype)
            if not has_sink:
                lse_ref[:, cols] = m2 * (1.0 / LOG2E) + jnp.log(den)


def _band_attention(h, max_j, dilation, sinks=None):
    r, l, _ = h.shape
    nb = l // BLOCK
    kcol, vcol = Q_DIM // KV_DIM, Q_DIM // KV_DIM + 1
    has_sink = sinks is not None
    rows = GQA_GROUP * BLOCK
    in_specs = [pl.BlockSpec((None, BLOCK, Q_DIM), lambda s, i: (s, i, 0)),
                pl.BlockSpec((None, BLOCK, KV_DIM), lambda s, i: (s, jnp.maximum(i - 1, 0), kcol)),
                pl.BlockSpec((None, BLOCK, KV_DIM), lambda s, i: (s, i, kcol)),
                pl.BlockSpec((None, BLOCK, KV_DIM), lambda s, i: (s, jnp.maximum(i - 1, 0), vcol)),
                pl.BlockSpec((None, BLOCK, KV_DIM), lambda s, i: (s, i, vcol)),
                pl.BlockSpec((N_KV, rows, LANES), lambda s, i: (0, 0, 0)),
                pl.BlockSpec((2 * BLOCK, LANES), lambda s, i: (0, 0))]
    o_spec = pl.BlockSpec((None, BLOCK, Q_DIM), lambda s, i: (s, i, 0))
    args = [h, h, h, h, h, _band_q_columns(dilation), _band_k_columns()]
    if has_sink:
        in_specs = [pl.BlockSpec(memory_space=pltpu.SMEM)] + in_specs
        args = [sinks.astype(jnp.float32)] + args
        out_specs = o_spec
        out_shape = jax.ShapeDtypeStruct((r, l, Q_DIM), jnp.bfloat16)
    else:
        out_specs = [o_spec, o_spec]
        out_shape = [jax.ShapeDtypeStruct((r, l, Q_DIM), jnp.float32),
                     jax.ShapeDtypeStruct((r, l, Q_DIM), jnp.float32)]
    return pl.pallas_call(
        functools.partial(_band_kernel, max_j=max_j, has_sink=has_sink, unit_heads=BAND_UNIT_HEADS,
                          ahead=BAND_AHEAD),
        grid=(r, nb),
        in_specs=in_specs,
        out_specs=out_specs,
        out_shape=out_shape,
        compiler_params=_params("parallel", "parallel"),
        name="band_attention",
    )(*args)


def _merge_kernel(*refs, dils):
    n_g = len(dils)
    o_refs, l_refs, out_ref, bufs = refs[:n_g], refs[n_g:2 * n_g], refs[2 * n_g], list(refs[2 * n_g + 1:])

    def natural(ref, d):
        if d == 1:
            return ref[0]
        buf = bufs.pop()
        per = ref.shape[1]
        for c in range(buf.shape[0]):
            for r in range(d):
                buf[c, pl.ds(r, per, stride=d), :] = ref[r, :, c * LANES:(c + 1) * LANES]
        return jnp.concatenate([buf[c] for c in range(buf.shape[0])], axis=1)

    os = [natural(ref, d) for ref, d in zip(o_refs, dils)]
    ls = [natural(ref, d) for ref, d in zip(l_refs, dils)]
    m = functools.reduce(jnp.maximum, ls)
    es = [jnp.exp(l - m) for l in ls]
    num = sum(e * o for e, o in zip(es, os))
    out_ref[...] = (num / sum(es)).astype(out_ref.dtype)


def _merge_groups(outs, lses, dils, tm):
    b, _, s, n = outs[0].shape
    tm = min(tm, s)
    assert s % tm == 0 and all(tm % (8 * d) == 0 for d in dils)
    per_batch = s // tm
    specs = [pl.BlockSpec((None, d, tm // d, n), lambda i: (i // per_batch, 0, i % per_batch, 0))
             for d in dils]
    n_buf = 2 * sum(d > 1 for d in dils)
    return pl.pallas_call(
        functools.partial(_merge_kernel, dils=tuple(dils)),
        grid=(b * per_batch,),
        in_specs=specs + specs,
        out_specs=pl.BlockSpec((tm, n), lambda i: (i, 0)),
        out_shape=jax.ShapeDtypeStruct((b * s, n), jnp.bfloat16),
        scratch_shapes=[pltpu.VMEM((n // LANES, tm, LANES), jnp.float32)] * n_buf,
        compiler_params=_params("parallel"),
        name="merge_groups",
    )(*outs, *lses)


def _count_tiles(n_tiles, tile_hits, tq, tcnt):
    def body(t, acc):
        hit = tile_hits(pl.multiple_of(t * tcnt, tcnt)).astype(jnp.float32)
        part = hit[:, :LANES]
        for c in range(1, tcnt // LANES):
            part = part + hit[:, c * LANES:(c + 1) * LANES]
        return acc + part
    acc = lax.fori_loop(0, n_tiles, body, jnp.zeros((tq, LANES), jnp.float32))
    return jnp.sum(acc, axis=-1, keepdims=True)


def _dsa_select_kernel(qia_ref, qib_ref, ki_ref, wi_ref, bias_ref, key_ref, *, top_k, tq, tsc, tcnt):
    i = pl.program_id(1)
    s_len = key_ref.shape[1]
    n_keys = (i + 1) * tq
    n_sc = n_keys // tsc
    n_cnt = (n_keys + tcnt - 1) // tcnt
    rowpos = lax.broadcasted_iota(jnp.int32, (tq, 1), 0) + i * tq
    col_sc = lax.broadcasted_iota(jnp.int32, (tq, tsc), 1)
    col_cnt = lax.broadcasted_iota(jnp.int32, (tq, tcnt), 1)

    key_ref[:, pl.ds(pl.multiple_of(n_cnt * tcnt - tsc, tsc), tsc)] = jnp.full((tq, tsc), INT_MIN, jnp.int32)

    half = IDX_HEADS // 2
    qis = jnp.concatenate([ref[:, h * IDX_DIM:(h + 1) * IDX_DIM]
                           for ref in (qia_ref, qib_ref) for h in range(half)], axis=0)
    wi = wi_ref[...]

    def score_body(t, carry):
        start = pl.multiple_of(t * tsc, tsc)
        ki = ki_ref[pl.ds(start, tsc), :][:, :IDX_DIM].astype(jnp.bfloat16)
        rel = lax.dot_general(qis, ki, (((1,), (1,)), ((), ())),
                              preferred_element_type=jnp.float32)
        sc = jnp.zeros((tq, tsc), jnp.float32)
        for h in range(IDX_HEADS):
            w_h = wi[:, IDX_DIM + h:IDX_DIM + h + 1]
            sc = sc + jnp.maximum(rel[h * tq:(h + 1) * tq], 0.0) * w_h
        sc = sc + 0.0
        bits = pltpu.bitcast(sc, jnp.int32)
        keys = jnp.where(bits < 0, bits ^ jnp.int32(0x7FFFFFFF), bits)
        keys = jnp.where(col_sc + start <= rowpos, keys, jnp.int32(INT_MIN))
        key_ref[:, pl.ds(start, tsc)] = keys
        return carry

    lax.fori_loop(0, n_sc, score_body, 0)

    kf = jnp.float32(top_k)

    def count_ge(cand):
        return _count_tiles(n_cnt, lambda st: key_ref[:, pl.ds(st, tcnt)] >= cand, tq, tcnt)

    short = (rowpos + 1 < top_k).astype(jnp.float32)

    def bit_cond(c):
        b, _, done, _ = c
        return (b < 32) & (jnp.min(done) < 0.5)

    def bit_body(c):
        b, prefix, done, thr = c
        cand = prefix ^ lax.shift_left(jnp.int32(1), jnp.int32(31) - b)
        cnt = count_ge(cand)
        prefix = jnp.where(cnt >= kf, cand, prefix)
        hit = (cnt == kf) & (done < 0.5)
        thr = jnp.where(hit, cand - 1, thr)
        done = jnp.where(hit, 1.0, done)
        return b + 1, prefix, done, thr

    init = (jnp.int32(0), jnp.full((tq, 1), INT_MIN, jnp.int32), short,
            jnp.full((tq, 1), INT_MIN, jnp.int32))
    _, prefix, done, thr = lax.while_loop(bit_cond, bit_body, init)
    is_done = done > 0.5
    thr = jnp.where(is_done, thr, prefix)

    def tie_search(_):
        n_gt = _count_tiles(n_cnt, lambda st: key_ref[:, pl.ds(st, tcnt)] > thr, tq, tcnt)
        need = kf - n_gt
        n_idx_bits = max(1, int(math.ceil(math.log2(s_len))))

        def idx_body(b, p):
            cand = p | lax.shift_left(jnp.int32(1), jnp.int32(n_idx_bits - 1) - b)
            below = _count_tiles(
                n_cnt, lambda st: (key_ref[:, pl.ds(st, tcnt)] == thr) & (col_cnt + st < cand), tq, tcnt)
            return jnp.where(below <= need - 1.0, cand, p)

        return lax.fori_loop(0, n_idx_bits, idx_body, jnp.zeros((tq, 1), jnp.int32))

    last_tie = lax.cond(jnp.min(done) < 0.5, tie_search,
                        lambda _: jnp.full((tq, 1), -1, jnp.int32), 0)
    last_tie = jnp.where(is_done, -1, last_tie)

    def write_body(t, carry):
        start = pl.multiple_of(t * tcnt, tcnt)
        kt = key_ref[:, pl.ds(start, tcnt)]
        col = col_cnt + start
        sel = ((kt > thr) | ((kt == thr) & (col <= last_tie))) & (col <= rowpos)
        bias_ref[:, pl.ds(start, tcnt)] = jnp.where(sel, 0.0, MASKED).astype(bias_ref.dtype)
        return carry

    lax.fori_loop(0, n_cnt, write_body, 0)

    def fill_body(t, carry):
        bias_ref[:, pl.ds(pl.multiple_of(t * tcnt, tcnt), tcnt)] = jnp.full(
            (tq, tcnt), MASKED, bias_ref.dtype)
        return carry

    lax.fori_loop(n_cnt, s_len // tcnt, fill_body, 0)


def _dsa_select(hm, kiwi, tq, tsc, tcnt):
    b, s, _ = hm.shape
    tq, tsc, tcnt = min(tq, s), min(tsc, s), min(tcnt, s)
    assert tq % tsc == 0 and tcnt % tsc == 0 and s % tcnt == 0 and s % tq == 0
    top_k = min(TOPK_MAX, s // 4)
    qi_half = IDX_HEADS * IDX_DIM // 2
    qicol = GRP_IN // qi_half
    return pl.pallas_call(
        functools.partial(_dsa_select_kernel, top_k=top_k, tq=tq, tsc=tsc, tcnt=tcnt),
        grid=(b, s // tq),
        in_specs=[pl.BlockSpec((None, tq, qi_half), lambda bb, i: (bb, i, qicol)),
                  pl.BlockSpec((None, tq, qi_half), lambda bb, i: (bb, i, qicol + 1)),
                  pl.BlockSpec((None, s, LANES), lambda bb, i: (bb, 0, 0)),
                  pl.BlockSpec((None, tq, LANES), lambda bb, i: (bb, i, 0))],
        out_specs=pl.BlockSpec((None, tq, s), lambda bb, i: (bb, i, 0)),
        out_shape=jax.ShapeDtypeStruct((b, s, s), jnp.bfloat16),
        scratch_shapes=[pltpu.VMEM((tq, s), jnp.int32)],
        compiler_params=_params("parallel", "parallel"),
        name="dsa_select",
    )(hm, hm, kiwi, kiwi)


def _dsa_attn_kernel(q_ref, k_ref, v_ref, bias_ref, sl_ref, o_ref,
                     kaug_ref, vaug_ref, m_ref, acc_ref, *, tk, chunk):
    i = pl.program_id(1)
    s_len = k_ref.shape[0]

    @pl.when(i == 0)
    def _():
        lane = lax.broadcasted_iota(jnp.int32, (chunk, HEAD_DIM), 1)
        ones_col = (lane == 0).astype(jnp.bfloat16)
        for c in range(s_len // chunk):
            pos = lax.broadcasted_iota(jnp.int32, (chunk, HEAD_DIM), 0) + c * chunk
            hi = pos - (pos & (POS_SPLIT - 1))
            lo = pos & (POS_SPLIT - 1)
            pcols = jnp.where(lane < N_SLOPE_PIECES, hi, jnp.where(lane < 2 * N_SLOPE_PIECES, lo, 0))
            pcols = pcols.astype(jnp.float32).astype(jnp.bfloat16)
            rows = pl.ds(c * chunk, chunk)
            for kv in range(N_KV):
                kaug_ref[kv, rows, :] = jnp.concatenate(
                    [k_ref[rows, kv * HEAD_DIM:(kv + 1) * HEAD_DIM], pcols], axis=1)
                vaug_ref[kv, rows, :] = jnp.concatenate(
                    [v_ref[rows, kv * HEAD_DIM:(kv + 1) * HEAD_DIM], ones_col], axis=1)

    n_tiles = (i * BLOCK + BLOCK + tk - 1) // tk
    m_ref[...] = jnp.full(m_ref.shape, MASKED, jnp.float32)
    acc_ref[...] = jnp.zeros(acc_ref.shape, jnp.float32)
    qa = [jnp.concatenate(
        [jnp.concatenate(
            [q_ref[:, (kv * GQA_GROUP + g) * HEAD_DIM:(kv * GQA_GROUP + g + 1) * HEAD_DIM]
             for g in range(GQA_GROUP)], axis=0), sl_ref[kv][:, HEAD_DIM:]], axis=1)
        for kv in range(N_KV)]
    n_rep = tk // LANES

    def attn_body(t, carry):
        start = pl.multiple_of(t * tk, tk)
        bias = bias_ref[:, pl.ds(start, tk)].astype(jnp.float32)
        for kv in range(N_KV):
            s_all = lax.dot_general(qa[kv], kaug_ref[kv, pl.ds(start, tk), :],
                                    (((1,), (1,)), ((), ())),
                                    preferred_element_type=jnp.float32)
            m_old = m_ref[kv]
            ps, m_news = [], []
            for g in range(GQA_GROUP):
                s = s_all[g * BLOCK:(g + 1) * BLOCK] + bias
                m_new = jnp.maximum(m_old[g * BLOCK:(g + 1) * BLOCK],
                                    jnp.max(s, axis=-1, keepdims=True))
                ps.append(jnp.exp2(s - jnp.concatenate([m_new] * n_rep, axis=1)).astype(jnp.bfloat16))
                m_news.append(m_new)
            m_new = jnp.concatenate(m_news, axis=0)
            pv = jnp.dot(jnp.concatenate(ps, axis=0), vaug_ref[kv, pl.ds(start, tk), :],
                         preferred_element_type=jnp.float32)
            acc_ref[kv] = jnp.exp2(m_old - m_new) * acc_ref[kv] + pv
            m_ref[kv] = m_new
        return carry

    lax.fori_loop(0, n_tiles, attn_body, 0)

    for kv in range(N_KV):
        acc = acc_ref[kv]
        o_all = (acc[:, :HEAD_DIM] / acc[:, HEAD_DIM:HEAD_DIM + 1]).astype(o_ref.dtype)
        for g in range(0, GQA_GROUP, 2):
            h = kv * GQA_GROUP + g
            o_ref[:, h * HEAD_DIM:(h + 2) * HEAD_DIM] = jnp.concatenate(
                [o_all[g * BLOCK:(g + 1) * BLOCK], o_all[(g + 1) * BLOCK:(g + 2) * BLOCK]], axis=1)


def _dsa_attention(hm, bias, tk):
    b, s, _ = hm.shape
    tk = min(tk, s)
    chunk = min(512, s)
    kcol = Q_DIM // KV_DIM
    rows = GQA_GROUP * BLOCK
    return pl.pallas_call(
        functools.partial(_dsa_attn_kernel, tk=tk, chunk=chunk),
        grid=(b, s // BLOCK),
        in_specs=[pl.BlockSpec((None, BLOCK, Q_DIM), lambda bb, i: (bb, i, 0)),
                  pl.BlockSpec((None, s, KV_DIM), lambda bb, i: (bb, 0, kcol)),
                  pl.BlockSpec((None, s, KV_DIM), lambda bb, i: (bb, 0, kcol + 1)),
                  pl.BlockSpec((None, BLOCK, s), lambda bb, i: (bb, i, 0)),
                  pl.BlockSpec((N_KV, rows, LANES), lambda bb, i: (0, 0, 0))],
        out_specs=pl.BlockSpec((None, BLOCK, Q_DIM), lambda bb, i: (bb, i, 0)),
        out_shape=jax.ShapeDtypeStruct((b, s, Q_DIM), jnp.bfloat16),
        scratch_shapes=[pltpu.VMEM((N_KV, s, LANES), jnp.bfloat16),
                        pltpu.VMEM((N_KV, s, LANES), jnp.bfloat16),
                        pltpu.VMEM((N_KV, rows, LANES), jnp.float32),
                        pltpu.VMEM((N_KV, rows, LANES), jnp.float32)],
        compiler_params=_params("parallel", "arbitrary"),
        name="dsa_attention",
    )(hm, hm, hm, bias, _slope_columns(1, 2))


_QKV_SCALE = np.concatenate([np.full(Q_DIM, ATTN_SCALE * LOG2E), np.ones(2 * KV_DIM)])
_DSA_SCALE = np.concatenate([_QKV_SCALE, np.full(IDX_HEADS * IDX_DIM, IDX_DIM ** -0.5)])


def _swa_layer(xb, w_in, layer, sinks, b, s):
    h = _matmul(xb, w_in, layer, 0, _QKV_SCALE, jnp.bfloat16, 1024, 1280)
    o = _band_attention(h.reshape(b, s, GRP_IN), SWA_WINDOW - 1, 1, sinks)
    return o.reshape(b * s, Q_DIM)


def _dilated_layer(xb, w_in, layer, b, s):
    outs, lses, dils = [], [], []
    for g, (window, dil) in enumerate(DIL_PATTERNS):
        h = _matmul(xb, w_in, layer, g * GRP_IN, _QKV_SCALE, jnp.bfloat16, 1024, 1280, batch=b, dil=dil)
        o, lse = _band_attention(h.reshape(b * dil, s // dil, GRP_IN), window // dil, dil)
        outs.append(o.reshape(b, dil, s // dil, Q_DIM))
        lses.append(lse.reshape(b, dil, s // dil, Q_DIM))
        dils.append(dil)
    return _merge_groups(outs, lses, dils, 256)


def _dsa_layer(xb, w_in, layer, b, s):
    hm = _matmul(xb, w_in, layer, 0, _DSA_SCALE, jnp.bfloat16, 1024, 896).reshape(b, s, C_MAIN)
    w_small = jnp.concatenate([w_in[layer, :, C_MAIN:C_MAIN + IDX_DIM],
                               w_in[layer, :, C_MAIN + IDX_DIM:] * IDX_HEADS ** -0.5,
                               jnp.zeros((D_MODEL, LANES - IDX_DIM - IDX_HEADS), w_in.dtype)], axis=1)
    kiwi = _matmul(xb, w_small[None], 0, 0, np.ones(LANES), jnp.float32, 1024, LANES).reshape(b, s, LANES)
    bias = _dsa_select(hm, kiwi, 256, 256, 512)
    o = _dsa_attention(hm, bias, 256)
    return o.reshape(b * s, Q_DIM)


def kernel(x, a_w_in, a_sinks, a_w_out, b_w_in, b_w_out, c_w_in, c_w_out, ln_g, ln_b,
           ffn_w_gate_up, ffn_w_down):
    b, s, d = x.shape
    xf = x.reshape(b * s, d)
    xb = xf.astype(jnp.bfloat16)
    for i in range(DEPTH):
        kind, j = i % N_MIXERS, i // N_MIXERS
        if kind == 0:
            o = _swa_layer(xb, a_w_in, j, a_sinks[j], b, s)
            w_out = a_w_out
        elif kind == 1:
            o = _dilated_layer(xb, b_w_in, j, b, s)
            w_out = b_w_out
        else:
            o = _dsa_layer(xb, c_w_in, j, b, s)
            w_out = c_w_out
        xf, xb = _proj_residual_ln(o, w_out, j, xf, ln_g[i, 0], ln_b[i, 0], 512, 512)
        hmid = _ffn_gate_up(xb, ffn_w_gate_up, i, 1024, 512)
        xf, xb = _proj_residual_ln(hmid, ffn_w_down, i, xf, ln_g[i, 1], ln_b[i, 1], 256, 512)
    return xf.reshape(b, s, d)
```

```python
import functools
import math

import numpy as np
import jax
import jax.numpy as jnp
from jax import lax
from jax.experimental import pallas as pl
from jax.experimental.pallas import tpu as pltpu

D_MODEL = 2048
DEPTH = 4
N_MIXERS = 3
HEAD_DIM = 64
N_HEADS = D_MODEL // HEAD_DIM
N_KV = N_HEADS // 8
GQA_GROUP = N_HEADS // N_KV
Q_DIM = N_HEADS * HEAD_DIM
KV_DIM = N_KV * HEAD_DIM
ATTN_SCALE = HEAD_DIM ** -0.5
BLOCK = 128
SWA_WINDOW = 128
DIL_PATTERNS = ((128, 1), (512, 4), (2048, 16))
N_DIL = len(DIL_PATTERNS)
IDX_HEADS = 16
IDX_DIM = 64
TOPK_MAX = 256
D_FF = 256 * math.ceil(8 * D_MODEL / (3 * 256))
DEEPNORM_ALPHA = (2 * DEPTH) ** 0.25
LN_EPS = 1e-5
GRP_IN = Q_DIM + 2 * KV_DIM
C_MAIN = Q_DIM + 2 * KV_DIM + IDX_HEADS * IDX_DIM
C_IN = C_MAIN + IDX_DIM + IDX_HEADS
LANES = 128
MASKED = -1e30
INT_MIN = -2 ** 31
LOG2E = 1.4426950408889634
VMEM_LIMIT = 56 * 1024 * 1024

_SLOPES = [float(np.float32(2.0 ** (-8.0 * (i + 1) / N_HEADS))) for i in range(N_HEADS)]
N_SLOPE_PIECES = 3
POS_SPLIT = 64
BAND_AHEAD = 2


def _params(*sem):
    return pltpu.CompilerParams(dimension_semantics=sem, vmem_limit_bytes=VMEM_LIMIT)


def _mm_kernel(x_ref, w_ref, sc_ref, o_ref, wb_ref, *acc, dil):
    @pl.when(pl.program_id(1) == 0)
    def _():
        wb_ref[...] = (w_ref[...] * sc_ref[...]).astype(jnp.bfloat16)

    y = jnp.dot(x_ref[...], wb_ref[...], preferred_element_type=jnp.float32)
    if dil == 1:
        o_ref[...] = y.astype(o_ref.dtype)
    else:
        acc_ref, = acc
        per = acc_ref.shape[1] // dil
        for c in range(acc_ref.shape[0]):
            cols = slice(c * LANES, (c + 1) * LANES)
            acc_ref[c] = y[:, cols]
            for r in range(dil):
                o_ref[r, :, cols] = acc_ref[c, pl.ds(r, per, stride=dil), :].astype(o_ref.dtype)


def _matmul(x, w, layer, col0, scale, out_dtype, tm, tn, batch=1, dil=1):
    m, k = x.shape
    n = scale.shape[0]
    tm, tn = min(tm, m // batch), min(tn, n)
    assert m % (batch * tm) == 0 and n % tn == 0 and col0 % tn == 0 and tm % (16 * dil) == 0
    j0 = col0 // tn
    if dil == 1:
        out_specs = pl.BlockSpec((tm, tn), lambda j, i: (i, j))
        out_shape = jax.ShapeDtypeStruct((m, n), out_dtype)
        scratch = []
    else:
        per_batch = m // batch // tm
        out_specs = pl.BlockSpec((None, dil, tm // dil, tn),
                                 lambda j, i: (i // per_batch, 0, i % per_batch, j))
        out_shape = jax.ShapeDtypeStruct((batch, dil, m // batch // dil, n), out_dtype)
        scratch = [pltpu.VMEM((tn // LANES, tm, LANES), jnp.float32)]
    return pl.pallas_call(
        functools.partial(_mm_kernel, dil=dil),
        grid=(n // tn, m // tm),
        in_specs=[pl.BlockSpec((tm, k), lambda j, i: (i, 0)),
                  pl.BlockSpec((None, k, tn), lambda j, i: (layer, 0, j0 + j)),
                  pl.BlockSpec((1, tn), lambda j, i: (0, j))],
        out_specs=out_specs,
        out_shape=out_shape,
        scratch_shapes=[pltpu.VMEM((k, tn), jnp.bfloat16)] + scratch,
        compiler_params=_params("parallel", "arbitrary"),
        name="proj_matmul",
    )(x, w, jnp.asarray(scale, jnp.float32).reshape(1, n))


def _gate_up_kernel(x_ref, wg_ref, wu_ref, o_ref, wgb_ref, wub_ref):
    @pl.when(pl.program_id(1) == 0)
    def _():
        wgb_ref[...] = wg_ref[...].astype(jnp.bfloat16)
        wub_ref[...] = wu_ref[...].astype(jnp.bfloat16)

    x = x_ref[...]
    g = jnp.dot(x, wgb_ref[...], preferred_element_type=jnp.float32)
    u = jnp.dot(x, wub_ref[...], preferred_element_type=jnp.float32)
    o_ref[...] = (g * (1.0 / (1.0 + jnp.exp(-g))) * u).astype(o_ref.dtype)


def _ffn_gate_up(x, w, layer, tm, tn):
    m, k = x.shape
    d_ff = w.shape[2] // 2
    tm = min(tm, m)
    assert m % tm == 0 and d_ff % tn == 0
    nj = d_ff // tn
    return pl.pallas_call(
        _gate_up_kernel,
        grid=(nj, m // tm),
        in_specs=[pl.BlockSpec((tm, k), lambda j, i: (i, 0)),
                  pl.BlockSpec((None, k, tn), lambda j, i: (layer, 0, j)),
                  pl.BlockSpec((None, k, tn), lambda j, i: (layer, 0, j + nj))],
        out_specs=pl.BlockSpec((tm, tn), lambda j, i: (i, j)),
        out_shape=jax.ShapeDtypeStruct((m, d_ff), jnp.bfloat16),
        scratch_shapes=[pltpu.VMEM((k, tn), jnp.bfloat16), pltpu.VMEM((k, tn), jnp.bfloat16)],
        compiler_params=_params("parallel", "arbitrary"),
        name="ffn_gate_up",
    )(x, w, w)


def _proj_ln_kernel(a_ref, w_ref, r_ref, g_ref, b_ref, o_ref, ob_ref, wb_ref, *, n_w, tkc):
    s = pl.program_id(0)

    @pl.when(s < n_w)
    def _():
        wb_ref[pl.ds(pl.multiple_of(s * tkc, tkc), tkc), :] = w_ref[...].astype(jnp.bfloat16)

    @pl.when(s >= n_w)
    def _():
        z = DEEPNORM_ALPHA * r_ref[...] + jnp.dot(a_ref[...], wb_ref[...],
                                                  preferred_element_type=jnp.float32)
        mu = jnp.mean(z, axis=-1, keepdims=True)
        zc = z - mu
        var = jnp.mean(zc * zc, axis=-1, keepdims=True)
        y = zc * lax.rsqrt(var + LN_EPS) * g_ref[...] + b_ref[...]
        o_ref[...] = y
        ob_ref[...] = y.astype(jnp.bfloat16)


def _proj_residual_ln(a, w, layer, resid, g, b, tm, tkc):
    m, k = a.shape
    n = w.shape[2]
    tm, tkc = min(tm, m), min(tkc, k)
    assert m % tm == 0 and k % tkc == 0
    n_w = k // tkc
    row = lambda s: (jnp.maximum(s - n_w, 0), 0)
    return pl.pallas_call(
        functools.partial(_proj_ln_kernel, n_w=n_w, tkc=tkc),
        grid=(n_w + m // tm,),
        in_specs=[pl.BlockSpec((tm, k), row),
                  pl.BlockSpec((None, tkc, n), lambda s: (layer, jnp.minimum(s, n_w - 1), 0)),
                  pl.BlockSpec((tm, n), row),
                  pl.BlockSpec((1, n), lambda s: (0, 0)),
                  pl.BlockSpec((1, n), lambda s: (0, 0))],
        out_specs=[pl.BlockSpec((tm, n), row), pl.BlockSpec((tm, n), row)],
        out_shape=[jax.ShapeDtypeStruct((m, n), jnp.float32),
                   jax.ShapeDtypeStruct((m, n), jnp.bfloat16)],
        scratch_shapes=[pltpu.VMEM((k, n), jnp.bfloat16)],
        compiler_params=_params("arbitrary"),
        name="proj_residual_ln",
    )(a, w, resid, g.reshape(1, n), b.reshape(1, n))


def _bf16_pieces(x, n):
    rest = np.asarray(x, np.float64)
    pieces = []
    for _ in range(n):
        p = rest.astype(np.float32).astype(jnp.bfloat16).astype(np.float64)
        pieces.append(p)
        rest = rest - p
    return pieces


def _slope_columns(mult, n_parts):
    out = np.zeros((N_KV, GQA_GROUP * BLOCK, LANES), np.float32)
    for h in range(N_HEADS):
        pieces = [float(p) for p in _bf16_pieces(np.float64(_SLOPES[h]) * mult * LOG2E, N_SLOPE_PIECES)]
        kv, g = divmod(h, GQA_GROUP)
        out[kv, g * BLOCK:(g + 1) * BLOCK, HEAD_DIM:HEAD_DIM + n_parts * N_SLOPE_PIECES] = np.asarray(
            pieces * n_parts, np.float32)
    return jnp.asarray(out, jnp.bfloat16)


def _band_q_columns(dilation):
    out = np.zeros((N_KV, GQA_GROUP * BLOCK, LANES), np.float32)
    qpos = BLOCK + np.arange(BLOCK, dtype=np.float64)
    for h in range(N_HEADS):
        s2 = np.float64(_SLOPES[h]) * dilation * LOG2E
        kv, g = divmod(h, GQA_GROUP)
        rows = slice(g * BLOCK, (g + 1) * BLOCK)
        for c, p in enumerate(_bf16_pieces(s2, N_SLOPE_PIECES)):
            out[kv, rows, HEAD_DIM + c] = p
        for c, p in enumerate(_bf16_pieces(-s2 * qpos, N_SLOPE_PIECES)):
            out[kv, rows, HEAD_DIM + N_SLOPE_PIECES + c] = p
    return jnp.asarray(out, jnp.bfloat16)


def _band_k_columns():
    out = np.zeros((2 * BLOCK, LANES), np.float32)
    out[:, HEAD_DIM:HEAD_DIM + N_SLOPE_PIECES] = np.arange(2 * BLOCK, dtype=np.float32)[:, None]
    out[:, HEAD_DIM + N_SLOPE_PIECES:HEAD_DIM + 2 * N_SLOPE_PIECES] = 1.0
    return jnp.asarray(out, jnp.bfloat16)


def _band_kernel(*refs, max_j, has_sink, ahead):
    if has_sink:
        sink_ref, q_ref, kp_ref, kc_ref, vp_ref, vc_ref, qc_ref, kc_cols_ref, o_ref = refs
    else:
        q_ref, kp_ref, kc_ref, vp_ref, vc_ref, qc_ref, kc_cols_ref, o_ref, lse_ref = refs
    i = pl.program_id(1)
    k = jnp.concatenate([kp_ref[...], kc_ref[...]], axis=0)
    v = jnp.concatenate([vp_ref[...], vc_ref[...]], axis=0)
    qpos = lax.broadcasted_iota(jnp.int32, (BLOCK, 2 * BLOCK), 0) + BLOCK
    kpos = lax.broadcasted_iota(jnp.int32, (BLOCK, 2 * BLOCK), 1)
    j = qpos - kpos
    valid = (j >= 0) & (j <= max_j) & ((kpos >= BLOCK) | (i > 0))
    bias = jnp.where(valid, 0.0, MASKED)
    low = lax.broadcasted_iota(jnp.int32, (BLOCK, LANES), 1) < HEAD_DIM
    k_cols = kc_cols_ref[:, HEAD_DIM:]
    ones64 = jnp.ones((2 * BLOCK, HEAD_DIM), jnp.bfloat16)
    ones128 = jnp.ones((2 * BLOCK, LANES), jnp.bfloat16)

    def scores(kv):
        ka = jnp.concatenate([k[:, kv * HEAD_DIM:(kv + 1) * HEAD_DIM], k_cols], axis=1)
        qa = jnp.concatenate(
            [jnp.concatenate([q_ref[:, (kv * GQA_GROUP + g) * HEAD_DIM:(kv * GQA_GROUP + g + 1) * HEAD_DIM]
                              for g in range(GQA_GROUP)], axis=0),
             qc_ref[kv][:, HEAD_DIM:]], axis=1)
        return lax.dot_general(qa, ka, (((1,), (1,)), ((), ())),
                               preferred_element_type=jnp.float32)

    pending = [scores(kv) for kv in range(min(ahead, N_KV))]
    for kv in range(N_KV):
        s_all = pending.pop(0)
        if kv + ahead < N_KV:
            pending.append(scores(kv + ahead))
        h0 = kv * GQA_GROUP
        v_kv = v[:, kv * HEAD_DIM:(kv + 1) * HEAD_DIM]
        v_even = jnp.concatenate([v_kv, ones64, ones128], axis=1)
        v_odd = jnp.concatenate([ones64, v_kv, ones128], axis=1)
        ps, ms = [], []
        for g in range(GQA_GROUP):
            s = s_all[g * BLOCK:(g + 1) * BLOCK] + bias
            m = jnp.max(s, axis=-1, keepdims=True)
            if has_sink:
                m = jnp.maximum(m, sink_ref[h0 + g] * LOG2E)
            ps.append(jnp.exp2(s - m).astype(jnp.bfloat16))
            ms.append(m)
        acc_e = jnp.dot(jnp.concatenate(ps[0::2], axis=0), v_even,
                        preferred_element_type=jnp.float32)
        acc_o = jnp.dot(jnp.concatenate(ps[1::2], axis=0), v_odd,
                        preferred_element_type=jnp.float32)
        for t in range(GQA_GROUP // 2):
            h = h0 + 2 * t
            rows = slice(t * BLOCK, (t + 1) * BLOCK)
            num = jnp.where(low, acc_e[rows, :LANES], acc_o[rows, :LANES])
            den = jnp.where(low, acc_e[rows, LANES:], acc_o[rows, LANES:])
            m2 = jnp.where(low, jnp.broadcast_to(ms[2 * t], (BLOCK, LANES)),
                           jnp.broadcast_to(ms[2 * t + 1], (BLOCK, LANES)))
            if has_sink:
                sink2 = jnp.where(low, sink_ref[h] * LOG2E, sink_ref[h + 1] * LOG2E)
                den = den + jnp.exp2(sink2 - m2)
            cols = slice(h * HEAD_DIM, (h + 2) * HEAD_DIM)
            o_ref[:, cols] = (num / den).astype(o_ref.dtype)
            if not has_sink:
                lse_ref[:, cols] = m2 * (1.0 / LOG2E) + jnp.log(den)


def _band_attention(h, max_j, dilation, sinks=None):
    r, l, _ = h.shape
    nb = l // BLOCK
    kcol, vcol = Q_DIM // KV_DIM, Q_DIM // KV_DIM + 1
    has_sink = sinks is not None
    rows = GQA_GROUP * BLOCK
    in_specs = [pl.BlockSpec((None, BLOCK, Q_DIM), lambda s, i: (s, i, 0)),
                pl.BlockSpec((None, BLOCK, KV_DIM), lambda s, i: (s, jnp.maximum(i - 1, 0), kcol)),
                pl.BlockSpec((None, BLOCK, KV_DIM), lambda s, i: (s, i, kcol)),
                pl.BlockSpec((None, BLOCK, KV_DIM), lambda s, i: (s, jnp.maximum(i - 1, 0), vcol)),
                pl.BlockSpec((None, BLOCK, KV_DIM), lambda s, i: (s, i, vcol)),
                pl.BlockSpec((N_KV, rows, LANES), lambda s, i: (0, 0, 0)),
                pl.BlockSpec((2 * BLOCK, LANES), lambda s, i: (0, 0))]
    o_spec = pl.BlockSpec((None, BLOCK, Q_DIM), lambda s, i: (s, i, 0))
    args = [h, h, h, h, h, _band_q_columns(dilation), _band_k_columns()]
    if has_sink:
        in_specs = [pl.BlockSpec(memory_space=pltpu.SMEM)] + in_specs
        args = [sinks.astype(jnp.float32)] + args
        out_specs = o_spec
        out_shape = jax.ShapeDtypeStruct((r, l, Q_DIM), jnp.bfloat16)
    else:
        out_specs = [o_spec, o_spec]
        out_shape = [jax.ShapeDtypeStruct((r, l, Q_DIM), jnp.float32),
                     jax.ShapeDtypeStruct((r, l, Q_DIM), jnp.float32)]
    return pl.pallas_call(
        functools.partial(_band_kernel, max_j=max_j, has_sink=has_sink, ahead=BAND_AHEAD),
        grid=(r, nb),
        in_specs=in_specs,
        out_specs=out_specs,
        out_shape=out_shape,
        compiler_params=_params("parallel", "parallel"),
        name="band_attention",
    )(*args)


def _merge_kernel(*refs, dils):
    n_g = len(dils)
    o_refs, l_refs, out_ref, bufs = refs[:n_g], refs[n_g:2 * n_g], refs[2 * n_g], list(refs[2 * n_g + 1:])

    def natural(ref, d):
        if d == 1:
            return ref[0]
        buf = bufs.pop()
        per = ref.shape[1]
        for c in range(buf.shape[0]):
            for r in range(d):
                buf[c, pl.ds(r, per, stride=d), :] = ref[r, :, c * LANES:(c + 1) * LANES]
        return jnp.concatenate([buf[c] for c in range(buf.shape[0])], axis=1)

    os = [natural(ref, d) for ref, d in zip(o_refs, dils)]
    ls = [natural(ref, d) for ref, d in zip(l_refs, dils)]
    m = functools.reduce(jnp.maximum, ls)
    es = [jnp.exp(l - m) for l in ls]
    num = sum(e * o for e, o in zip(es, os))
    out_ref[...] = (num / sum(es)).astype(out_ref.dtype)


def _merge_groups(outs, lses, dils, tm):
    b, _, s, n = outs[0].shape
    tm = min(tm, s)
    assert s % tm == 0 and all(tm % (8 * d) == 0 for d in dils)
    per_batch = s // tm
    specs = [pl.BlockSpec((None, d, tm // d, n), lambda i: (i // per_batch, 0, i % per_batch, 0))
             for d in dils]
    n_buf = 2 * sum(d > 1 for d in dils)
    return pl.pallas_call(
        functools.partial(_merge_kernel, dils=tuple(dils)),
        grid=(b * per_batch,),
        in_specs=specs + specs,
        out_specs=pl.BlockSpec((tm, n), lambda i: (i, 0)),
        out_shape=jax.ShapeDtypeStruct((b * s, n), jnp.bfloat16),
        scratch_shapes=[pltpu.VMEM((n // LANES, tm, LANES), jnp.float32)] * n_buf,
        compiler_params=_params("parallel"),
        name="merge_groups",
    )(*outs, *lses)


def _count_tiles(n_tiles, tile_hits, tq, tcnt):
    def body(t, acc):
        hit = tile_hits(pl.multiple_of(t * tcnt, tcnt)).astype(jnp.float32)
        part = hit[:, :LANES]
        for c in range(1, tcnt // LANES):
            part = part + hit[:, c * LANES:(c + 1) * LANES]
        return acc + part
    acc = lax.fori_loop(0, n_tiles, body, jnp.zeros((tq, LANES), jnp.float32))
    return jnp.sum(acc, axis=-1, keepdims=True)


def _dsa_select_kernel(qia_ref, qib_ref, ki_ref, wi_ref, bias_ref, key_ref, *, top_k, tq, tsc, tcnt):
    i = pl.program_id(1)
    s_len = key_ref.shape[1]
    n_keys = (i + 1) * tq
    n_sc = (n_keys + tsc - 1) // tsc
    n_cnt = (n_keys + tcnt - 1) // tcnt
    rowpos = lax.broadcasted_iota(jnp.int32, (tq, 1), 0) + i * tq
    col_sc = lax.broadcasted_iota(jnp.int32, (tq, tsc), 1)
    col_cnt = lax.broadcasted_iota(jnp.int32, (tq, tcnt), 1)

    key_ref[:, pl.ds(pl.multiple_of(n_cnt * tcnt - tsc, tsc), tsc)] = jnp.full((tq, tsc), INT_MIN, jnp.int32)

    half = IDX_HEADS // 2
    qis = [jnp.concatenate([ref[:, h * IDX_DIM:(h + 1) * IDX_DIM] for h in range(half)], axis=0)
           for ref in (qia_ref, qib_ref)]
    wi = wi_ref[...]

    def score_body(t, carry):
        start = pl.multiple_of(t * tsc, tsc)
        ki = ki_ref[pl.ds(start, tsc), :][:, :IDX_DIM].astype(jnp.bfloat16)
        rels = [lax.dot_general(q, ki, (((1,), (1,)), ((), ())), preferred_element_type=jnp.float32)
                for q in qis]
        sc = jnp.zeros((tq, tsc), jnp.float32)
        for h in range(IDX_HEADS):
            w_h = wi[:, IDX_DIM + h:IDX_DIM + h + 1]
            r0 = (h % half) * tq
            sc = sc + jnp.maximum(rels[h // half][r0:r0 + tq], 0.0) * w_h
        sc = sc + 0.0
        bits = pltpu.bitcast(sc, jnp.int32)
        keys = jnp.where(bits < 0, bits ^ jnp.int32(0x7FFFFFFF), bits)
        keys = jnp.where(col_sc + start <= rowpos, keys, jnp.int32(INT_MIN))
        key_ref[:, pl.ds(start, tsc)] = keys
        return carry

    lax.fori_loop(0, n_sc, score_body, 0)

    kf = jnp.float32(top_k)

    def count_ge(cand):
        return _count_tiles(n_cnt, lambda st: key_ref[:, pl.ds(st, tcnt)] >= cand, tq, tcnt)

    short = (rowpos + 1 < top_k).astype(jnp.float32)

    def bit_cond(c):
        b, _, done, _ = c
        return (b < 32) & (jnp.min(done) < 0.5)

    def bit_body(c):
        b, prefix, done, thr = c
        cand = prefix ^ lax.shift_left(jnp.int32(1), jnp.int32(31) - b)
        cnt = count_ge(cand)
        prefix = jnp.where(cnt >= kf, cand, prefix)
        hit = (cnt == kf) & (done < 0.5)
        thr = jnp.where(hit, cand - 1, thr)
        done = jnp.where(hit, 1.0, done)
        return b + 1, prefix, done, thr

    init = (jnp.int32(0), jnp.full((tq, 1), INT_MIN, jnp.int32), short,
            jnp.full((tq, 1), INT_MIN, jnp.int32))
    _, prefix, done, thr = lax.while_loop(bit_cond, bit_body, init)
    is_done = done > 0.5
    thr = jnp.where(is_done, thr, prefix)

    def tie_search(_):
        n_gt = _count_tiles(n_cnt, lambda st: key_ref[:, pl.ds(st, tcnt)] > thr, tq, tcnt)
        need = kf - n_gt
        n_idx_bits = max(1, int(math.ceil(math.log2(s_len))))

        def idx_body(b, p):
            cand = p | lax.shift_left(jnp.int32(1), jnp.int32(n_idx_bits - 1) - b)
            below = _count_tiles(
                n_cnt, lambda st: (key_ref[:, pl.ds(st, tcnt)] == thr) & (col_cnt + st < cand), tq, tcnt)
            return jnp.where(below <= need - 1.0, cand, p)

        return lax.fori_loop(0, n_idx_bits, idx_body, jnp.zeros((tq, 1), jnp.int32))

    last_tie = lax.cond(jnp.min(done) < 0.5, tie_search,
                        lambda _: jnp.full((tq, 1), -1, jnp.int32), 0)
    last_tie = jnp.where(is_done, -1, last_tie)

    def write_body(t, carry):
        start = pl.multiple_of(t * tcnt, tcnt)
        kt = key_ref[:, pl.ds(start, tcnt)]
        col = col_cnt + start
        sel = ((kt > thr) | ((kt == thr) & (col <= last_tie))) & (col <= rowpos)
        bias_ref[:, pl.ds(start, tcnt)] = jnp.where(sel, 0.0, MASKED).astype(bias_ref.dtype)
        return carry

    lax.fori_loop(0, n_cnt, write_body, 0)

    def fill_body(t, carry):
        bias_ref[:, pl.ds(pl.multiple_of(t * tcnt, tcnt), tcnt)] = jnp.full(
            (tq, tcnt), MASKED, bias_ref.dtype)
        return carry

    lax.fori_loop(n_cnt, s_len // tcnt, fill_body, 0)


def _dsa_select(hm, kiwi, tq, tsc, tcnt):
    b, s, _ = hm.shape
    tq, tsc, tcnt = min(tq, s), min(tsc, s), min(tcnt, s)
    assert tcnt % tsc == 0 and s % tcnt == 0 and s % tq == 0
    top_k = min(TOPK_MAX, s // 4)
    qi_half = IDX_HEADS * IDX_DIM // 2
    qicol = GRP_IN // qi_half
    return pl.pallas_call(
        functools.partial(_dsa_select_kernel, top_k=top_k, tq=tq, tsc=tsc, tcnt=tcnt),
        grid=(b, s // tq),
        in_specs=[pl.BlockSpec((None, tq, qi_half), lambda bb, i: (bb, i, qicol)),
                  pl.BlockSpec((None, tq, qi_half), lambda bb, i: (bb, i, qicol + 1)),
                  pl.BlockSpec((None, s, LANES), lambda bb, i: (bb, 0, 0)),
                  pl.BlockSpec((None, tq, LANES), lambda bb, i: (bb, i, 0))],
        out_specs=pl.BlockSpec((None, tq, s), lambda bb, i: (bb, i, 0)),
        out_shape=jax.ShapeDtypeStruct((b, s, s), jnp.bfloat16),
        scratch_shapes=[pltpu.VMEM((tq, s), jnp.int32)],
        compiler_params=_params("parallel", "parallel"),
        name="dsa_select",
    )(hm, hm, kiwi, kiwi)


def _dsa_attn_kernel(q_ref, k_ref, v_ref, bias_ref, sl_ref, o_ref,
                     kaug_ref, vaug_ref, m_ref, acc_ref, *, tk, chunk):
    i = pl.program_id(1)
    s_len = k_ref.shape[0]

    @pl.when(i == 0)
    def _():
        lane = lax.broadcasted_iota(jnp.int32, (chunk, HEAD_DIM), 1)
        ones_col = (lane == 0).astype(jnp.bfloat16)
        for c in range(s_len // chunk):
            pos = lax.broadcasted_iota(jnp.int32, (chunk, HEAD_DIM), 0) + c * chunk
            hi = pos - (pos & (POS_SPLIT - 1))
            lo = pos & (POS_SPLIT - 1)
            pcols = jnp.where(lane < N_SLOPE_PIECES, hi, jnp.where(lane < 2 * N_SLOPE_PIECES, lo, 0))
            pcols = pcols.astype(jnp.float32).astype(jnp.bfloat16)
            rows = pl.ds(c * chunk, chunk)
            for kv in range(N_KV):
                kaug_ref[kv, rows, :] = jnp.concatenate(
                    [k_ref[rows, kv * HEAD_DIM:(kv + 1) * HEAD_DIM], pcols], axis=1)
                vaug_ref[kv, rows, :] = jnp.concatenate(
                    [v_ref[rows, kv * HEAD_DIM:(kv + 1) * HEAD_DIM], ones_col], axis=1)

    n_tiles = (i * BLOCK + BLOCK + tk - 1) // tk
    m_ref[...] = jnp.full(m_ref.shape, MASKED, jnp.float32)
    acc_ref[...] = jnp.zeros(acc_ref.shape, jnp.float32)
    qa = [jnp.concatenate(
        [jnp.concatenate(
            [q_ref[:, (kv * GQA_GROUP + g) * HEAD_DIM:(kv * GQA_GROUP + g + 1) * HEAD_DIM]
             for g in range(GQA_GROUP)], axis=0), sl_ref[kv][:, HEAD_DIM:]], axis=1)
        for kv in range(N_KV)]
    n_rep = tk // LANES

    def attn_body(t, carry):
        start = pl.multiple_of(t * tk, tk)
        bias = bias_ref[:, pl.ds(start, tk)].astype(jnp.float32)
        for kv in range(N_KV):
            s_all = lax.dot_general(qa[kv], kaug_ref[kv, pl.ds(start, tk), :],
                                    (((1,), (1,)), ((), ())),
                                    preferred_element_type=jnp.float32)
            m_old = m_ref[kv]
            ps, m_news = [], []
            for g in range(GQA_GROUP):
                s = s_all[g * BLOCK:(g + 1) * BLOCK] + bias
                m_new = jnp.maximum(m_old[g * BLOCK:(g + 1) * BLOCK],
                                    jnp.max(s, axis=-1, keepdims=True))
                ps.append(jnp.exp2(s - jnp.concatenate([m_new] * n_rep, axis=1)).astype(jnp.bfloat16))
                m_news.append(m_new)
            m_new = jnp.concatenate(m_news, axis=0)
            pv = jnp.dot(jnp.concatenate(ps, axis=0), vaug_ref[kv, pl.ds(start, tk), :],
                         preferred_element_type=jnp.float32)
            acc_ref[kv] = jnp.exp2(m_old - m_new) * acc_ref[kv] + pv
            m_ref[kv] = m_new
        return carry

    lax.fori_loop(0, n_tiles, attn_body, 0)

    for kv in range(N_KV):
        acc = acc_ref[kv]
        o_all = (acc[:, :HEAD_DIM] / acc[:, HEAD_DIM:HEAD_DIM + 1]).astype(o_ref.dtype)
        for g in range(0, GQA_GROUP, 2):
            h = kv * GQA_GROUP + g
            o_ref[:, h * HEAD_DIM:(h + 2) * HEAD_DIM] = jnp.concatenate(
                [o_all[g * BLOCK:(g + 1) * BLOCK], o_all[(g + 1) * BLOCK:(g + 2) * BLOCK]], axis=1)


def _dsa_attention(hm, bias, tk):
    b, s, _ = hm.shape
    tk = min(tk, s)
    chunk = min(512, s)
    kcol = Q_DIM // KV_DIM
    rows = GQA_GROUP * BLOCK
    return pl.pallas_call(
        functools.partial(_dsa_attn_kernel, tk=tk, chunk=chunk),
        grid=(b, s // BLOCK),
        in_specs=[pl.BlockSpec((None, BLOCK, Q_DIM), lambda bb, i: (bb, i, 0)),
                  pl.BlockSpec((None, s, KV_DIM), lambda bb, i: (bb, 0, kcol)),
                  pl.BlockSpec((None, s, KV_DIM), lambda bb, i: (bb, 0, kcol + 1)),
                  pl.BlockSpec((None, BLOCK, s), lambda bb, i: (bb, i, 0)),
                  pl.BlockSpec((N_KV, rows, LANES), lambda bb, i: (0, 0, 0))],
        out_specs=pl.BlockSpec((None, BLOCK, Q_DIM), lambda bb, i: (bb, i, 0)),
        out_shape=jax.ShapeDtypeStruct((b, s, Q_DIM), jnp.bfloat16),
        scratch_shapes=[pltpu.VMEM((N_KV, s, LANES), jnp.bfloat16),
                        pltpu.VMEM((N_KV, s, LANES), jnp.bfloat16),
                        pltpu.VMEM((N_KV, rows, LANES), jnp.float32),
                        pltpu.VMEM((N_KV, rows, LANES), jnp.float32)],
        compiler_params=_params("parallel", "arbitrary"),
        name="dsa_attention",
    )(hm, hm, hm, bias, _slope_columns(1, 2))


_QKV_SCALE = np.concatenate([np.full(Q_DIM, ATTN_SCALE * LOG2E), np.ones(2 * KV_DIM)])
_DSA_SCALE = np.concatenate([_QKV_SCALE, np.full(IDX_HEADS * IDX_DIM, IDX_DIM ** -0.5)])


def _swa_layer(xb, w_in, layer, sinks, b, s):
    h = _matmul(xb, w_in, layer, 0, _QKV_SCALE, jnp.bfloat16, 1024, 1280)
    o = _band_attention(h.reshape(b, s, GRP_IN), SWA_WINDOW - 1, 1, sinks)
    return o.reshape(b * s, Q_DIM)


def _dilated_layer(xb, w_in, layer, b, s):
    outs, lses, dils = [], [], []
    for g, (window, dil) in enumerate(DIL_PATTERNS):
        h = _matmul(xb, w_in, layer, g * GRP_IN, _QKV_SCALE, jnp.bfloat16, 1024, 1280, batch=b, dil=dil)
        o, lse = _band_attention(h.reshape(b * dil, s // dil, GRP_IN), window // dil, dil)
        outs.append(o.reshape(b, dil, s // dil, Q_DIM))
        lses.append(lse.reshape(b, dil, s // dil, Q_DIM))
        dils.append(dil)
    return _merge_groups(outs, lses, dils, 256)


def _dsa_layer(xb, w_in, layer, b, s):
    hm = _matmul(xb, w_in, layer, 0, _DSA_SCALE, jnp.bfloat16, 1024, 896).reshape(b, s, C_MAIN)
    w_small = jnp.concatenate([w_in[layer, :, C_MAIN:C_MAIN + IDX_DIM],
                               w_in[layer, :, C_MAIN + IDX_DIM:] * IDX_HEADS ** -0.5,
                               jnp.zeros((D_MODEL, LANES - IDX_DIM - IDX_HEADS), w_in.dtype)], axis=1)
    kiwi = _matmul(xb, w_small[None], 0, 0, np.ones(LANES), jnp.float32, 1024, LANES).reshape(b, s, LANES)
    bias = _dsa_select(hm, kiwi, 256, 512, 512)
    o = _dsa_attention(hm, bias, 512)
    return o.reshape(b * s, Q_DIM)


def kernel(x, a_w_in, a_sinks, a_w_out, b_w_in, b_w_out, c_w_in, c_w_out, ln_g, ln_b,
           ffn_w_gate_up, ffn_w_down):
    b, s, d = x.shape
    xf = x.reshape(b * s, d)
    xb = xf.astype(jnp.bfloat16)
    for i in range(DEPTH):
        kind, j = i % N_MIXERS, i // N_MIXERS
        if kind == 0:
            o = _swa_layer(xb, a_w_in, j, a_sinks[j], b, s)
            w_out = a_w_out
        elif kind == 1:
            o = _dilated_layer(xb, b_w_in, j, b, s)
            w_out = b_w_out
        else:
            o = _dsa_layer(xb, c_w_in, j, b, s)
            w_out = c_w_out
        xf, xb = _proj_residual_ln(o, w_out, j, xf, ln_g[i, 0], ln_b[i, 0], 512, 512)
        hmid = _ffn_gate_up(xb, ffn_w_gate_up, i, 1024, 512)
        xf, xb = _proj_residual_ln(hmid, ffn_w_down, i, xf, ln_g[i, 1], ln_b[i, 1], 256, 512)
    return xf.reshape(b, s, d)
```

```python
import functools
import math

import numpy as np
import jax
import jax.numpy as jnp
from jax import lax
from jax.experimental import pallas as pl
from jax.experimental.pallas import tpu as pltpu

D_MODEL = 2048
DEPTH = 4
N_MIXERS = 3
HEAD_DIM = 64
N_HEADS = D_MODEL // HEAD_DIM
N_KV = N_HEADS // 8
GQA_GROUP = N_HEADS // N_KV
Q_DIM = N_HEADS * HEAD_DIM
KV_DIM = N_KV * HEAD_DIM
ATTN_SCALE = HEAD_DIM ** -0.5
BLOCK = 128
SWA_WINDOW = 128
DIL_PATTERNS = ((128, 1), (512, 4), (2048, 16))
N_DIL = len(DIL_PATTERNS)
IDX_HEADS = 16
IDX_DIM = 64
TOPK_MAX = 256
D_FF = 256 * math.ceil(8 * D_MODEL / (3 * 256))
DEEPNORM_ALPHA = (2 * DEPTH) ** 0.25
LN_EPS = 1e-5
GRP_IN = Q_DIM + 2 * KV_DIM
C_MAIN = Q_DIM + 2 * KV_DIM + IDX_HEADS * IDX_DIM
C_IN = C_MAIN + IDX_DIM + IDX_HEADS
LANES = 128
MASKED = -1e30
INT_MIN = -2 ** 31
LOG2E = 1.4426950408889634
VMEM_LIMIT = 56 * 1024 * 1024

_SLOPES = [float(np.float32(2.0 ** (-8.0 * (i + 1) / N_HEADS))) for i in range(N_HEADS)]
N_SLOPE_PIECES = 3
POS_SPLIT = 64
BAND_AHEAD = 4


def _params(*sem):
    return pltpu.CompilerParams(dimension_semantics=sem, vmem_limit_bytes=VMEM_LIMIT)


def _mm_kernel(x_ref, w_ref, sc_ref, o_ref, wb_ref, *acc, dil):
    @pl.when(pl.program_id(1) == 0)
    def _():
        wb_ref[...] = (w_ref[...] * sc_ref[...]).astype(jnp.bfloat16)

    y = jnp.dot(x_ref[...], wb_ref[...], preferred_element_type=jnp.float32)
    if dil == 1:
        o_ref[...] = y.astype(o_ref.dtype)
    else:
        acc_ref, = acc
        per = acc_ref.shape[1] // dil
        for c in range(acc_ref.shape[0]):
            cols = slice(c * LANES, (c + 1) * LANES)
            acc_ref[c] = y[:, cols]
            for r in range(dil):
                o_ref[r, :, cols] = acc_ref[c, pl.ds(r, per, stride=dil), :].astype(o_ref.dtype)


def _matmul(x, w, layer, col0, scale, out_dtype, tm, tn, batch=1, dil=1):
    m, k = x.shape
    n = scale.shape[0]
    tm, tn = min(tm, m // batch), min(tn, n)
    assert m % (batch * tm) == 0 and n % tn == 0 and col0 % tn == 0 and tm % (16 * dil) == 0
    j0 = col0 // tn
    if dil == 1:
        out_specs = pl.BlockSpec((tm, tn), lambda j, i: (i, j))
        out_shape = jax.ShapeDtypeStruct((m, n), out_dtype)
        scratch = []
    else:
        per_batch = m // batch // tm
        out_specs = pl.BlockSpec((None, dil, tm // dil, tn),
                                 lambda j, i: (i // per_batch, 0, i % per_batch, j))
        out_shape = jax.ShapeDtypeStruct((batch, dil, m // batch // dil, n), out_dtype)
        scratch = [pltpu.VMEM((tn // LANES, tm, LANES), jnp.float32)]
    return pl.pallas_call(
        functools.partial(_mm_kernel, dil=dil),
        grid=(n // tn, m // tm),
        in_specs=[pl.BlockSpec((tm, k), lambda j, i: (i, 0)),
                  pl.BlockSpec((None, k, tn), lambda j, i: (layer, 0, j0 + j)),
                  pl.BlockSpec((1, tn), lambda j, i: (0, j))],
        out_specs=out_specs,
        out_shape=out_shape,
        scratch_shapes=[pltpu.VMEM((k, tn), jnp.bfloat16)] + scratch,
        compiler_params=_params("parallel", "arbitrary"),
        name="proj_matmul",
    )(x, w, jnp.asarray(scale, jnp.float32).reshape(1, n))


def _gate_up_kernel(x_ref, wg_ref, wu_ref, o_ref, wgb_ref, wub_ref):
    @pl.when(pl.program_id(1) == 0)
    def _():
        wgb_ref[...] = wg_ref[...].astype(jnp.bfloat16)
        wub_ref[...] = wu_ref[...].astype(jnp.bfloat16)

    x = x_ref[...]
    g = jnp.dot(x, wgb_ref[...], preferred_element_type=jnp.float32)
    u = jnp.dot(x, wub_ref[...], preferred_element_type=jnp.float32)
    o_ref[...] = (g * (1.0 / (1.0 + jnp.exp(-g))) * u).astype(o_ref.dtype)


def _ffn_gate_up(x, w, layer, tm, tn):
    m, k = x.shape
    d_ff = w.shape[2] // 2
    tm = min(tm, m)
    assert m % tm == 0 and d_ff % tn == 0
    nj = d_ff // tn
    return pl.pallas_call(
        _gate_up_kernel,
        grid=(nj, m // tm),
        in_specs=[pl.BlockSpec((tm, k), lambda j, i: (i, 0)),
                  pl.BlockSpec((None, k, tn), lambda j, i: (layer, 0, j)),
                  pl.BlockSpec((None, k, tn), lambda j, i: (layer, 0, j + nj))],
        out_specs=pl.BlockSpec((tm, tn), lambda j, i: (i, j)),
        out_shape=jax.ShapeDtypeStruct((m, d_ff), jnp.bfloat16),
        scratch_shapes=[pltpu.VMEM((k, tn), jnp.bfloat16), pltpu.VMEM((k, tn), jnp.bfloat16)],
        compiler_params=_params("parallel", "arbitrary"),
        name="ffn_gate_up",
    )(x, w, w)


def _proj_ln_kernel(a_ref, w_ref, r_ref, g_ref, b_ref, o_ref, ob_ref, wb_ref, *, n_w, tkc):
    s = pl.program_id(0)

    @pl.when(s < n_w)
    def _():
        wb_ref[pl.ds(pl.multiple_of(s * tkc, tkc), tkc), :] = w_ref[...].astype(jnp.bfloat16)

    @pl.when(s >= n_w)
    def _():
        z = DEEPNORM_ALPHA * r_ref[...] + jnp.dot(a_ref[...], wb_ref[...],
                                                  preferred_element_type=jnp.float32)
        mu = jnp.mean(z, axis=-1, keepdims=True)
        zc = z - mu
        var = jnp.mean(zc * zc, axis=-1, keepdims=True)
        y = zc * lax.rsqrt(var + LN_EPS) * g_ref[...] + b_ref[...]
        o_ref[...] = y
        ob_ref[...] = y.astype(jnp.bfloat16)


def _proj_residual_ln(a, w, layer, resid, g, b, tm, tkc):
    m, k = a.shape
    n = w.shape[2]
    tm, tkc = min(tm, m), min(tkc, k)
    assert m % tm == 0 and k % tkc == 0
    n_w = k // tkc
    row = lambda s: (jnp.maximum(s - n_w, 0), 0)
    return pl.pallas_call(
        functools.partial(_proj_ln_kernel, n_w=n_w, tkc=tkc),
        grid=(n_w + m // tm,),
        in_specs=[pl.BlockSpec((tm, k), row),
                  pl.BlockSpec((None, tkc, n), lambda s: (layer, jnp.minimum(s, n_w - 1), 0)),
                  pl.BlockSpec((tm, n), row),
                  pl.BlockSpec((1, n), lambda s: (0, 0)),
                  pl.BlockSpec((1, n), lambda s: (0, 0))],
        out_specs=[pl.BlockSpec((tm, n), row), pl.BlockSpec((tm, n), row)],
        out_shape=[jax.ShapeDtypeStruct((m, n), jnp.float32),
                   jax.ShapeDtypeStruct((m, n), jnp.bfloat16)],
        scratch_shapes=[pltpu.VMEM((k, n), jnp.bfloat16)],
        compiler_params=_params("arbitrary"),
        name="proj_residual_ln",
    )(a, w, resid, g.reshape(1, n), b.reshape(1, n))


def _bf16_pieces(x, n):
    rest = np.asarray(x, np.float64)
    pieces = []
    for _ in range(n):
        p = rest.astype(np.float32).astype(jnp.bfloat16).astype(np.float64)
        pieces.append(p)
        rest = rest - p
    return pieces


def _slope_columns(mult, n_parts):
    out = np.zeros((N_KV, GQA_GROUP * BLOCK, LANES), np.float32)
    for h in range(N_HEADS):
        pieces = [float(p) for p in _bf16_pieces(np.float64(_SLOPES[h]) * mult * LOG2E, N_SLOPE_PIECES)]
        kv, g = divmod(h, GQA_GROUP)
        out[kv, g * BLOCK:(g + 1) * BLOCK, HEAD_DIM:HEAD_DIM + n_parts * N_SLOPE_PIECES] = np.asarray(
            pieces * n_parts, np.float32)
    return jnp.asarray(out, jnp.bfloat16)


def _band_q_columns(dilation):
    out = np.zeros((N_KV, GQA_GROUP * BLOCK, LANES), np.float32)
    qpos = BLOCK + np.arange(BLOCK, dtype=np.float64)
    for h in range(N_HEADS):
        s2 = np.float64(_SLOPES[h]) * dilation * LOG2E
        kv, g = divmod(h, GQA_GROUP)
        rows = slice(g * BLOCK, (g + 1) * BLOCK)
        for c, p in enumerate(_bf16_pieces(s2, N_SLOPE_PIECES)):
            out[kv, rows, HEAD_DIM + c] = p
        for c, p in enumerate(_bf16_pieces(-s2 * qpos, N_SLOPE_PIECES)):
            out[kv, rows, HEAD_DIM + N_SLOPE_PIECES + c] = p
    return jnp.asarray(out, jnp.bfloat16)


def _band_k_columns():
    out = np.zeros((2 * BLOCK, LANES), np.float32)
    out[:, HEAD_DIM:HEAD_DIM + N_SLOPE_PIECES] = np.arange(2 * BLOCK, dtype=np.float32)[:, None]
    out[:, HEAD_DIM + N_SLOPE_PIECES:HEAD_DIM + 2 * N_SLOPE_PIECES] = 1.0
    return jnp.asarray(out, jnp.bfloat16)


def _band_kernel(*refs, max_j, has_sink, ahead):
    if has_sink:
        sink_ref, q_ref, kp_ref, kc_ref, vp_ref, vc_ref, qc_ref, kc_cols_ref, o_ref = refs
    else:
        q_ref, kp_ref, kc_ref, vp_ref, vc_ref, qc_ref, kc_cols_ref, o_ref, lse_ref = refs
    i = pl.program_id(1)
    k = jnp.concatenate([kp_ref[...], kc_ref[...]], axis=0)
    v = jnp.concatenate([vp_ref[...], vc_ref[...]], axis=0)
    qpos = lax.broadcasted_iota(jnp.int32, (BLOCK, 2 * BLOCK), 0) + BLOCK
    kpos = lax.broadcasted_iota(jnp.int32, (BLOCK, 2 * BLOCK), 1)
    j = qpos - kpos
    valid = (j >= 0) & (j <= max_j) & ((kpos >= BLOCK) | (i > 0))
    bias = jnp.where(valid, 0.0, MASKED)
    low = lax.broadcasted_iota(jnp.int32, (BLOCK, LANES), 1) < HEAD_DIM
    k_cols = kc_cols_ref[:, HEAD_DIM:]
    ones64 = jnp.ones((2 * BLOCK, HEAD_DIM), jnp.bfloat16)
    ones128 = jnp.ones((2 * BLOCK, LANES), jnp.bfloat16)

    def scores(kv):
        ka = jnp.concatenate([k[:, kv * HEAD_DIM:(kv + 1) * HEAD_DIM], k_cols], axis=1)
        qa = jnp.concatenate(
            [jnp.concatenate([q_ref[:, (kv * GQA_GROUP + g) * HEAD_DIM:(kv * GQA_GROUP + g + 1) * HEAD_DIM]
                              for g in range(GQA_GROUP)], axis=0),
             qc_ref[kv][:, HEAD_DIM:]], axis=1)
        return lax.dot_general(qa, ka, (((1,), (1,)), ((), ())),
                               preferred_element_type=jnp.float32)

    pending = [scores(kv) for kv in range(min(ahead, N_KV))]
    for kv in range(N_KV):
        s_all = pending.pop(0)
        if kv + ahead < N_KV:
            pending.append(scores(kv + ahead))
        h0 = kv * GQA_GROUP
        v_kv = v[:, kv * HEAD_DIM:(kv + 1) * HEAD_DIM]
        v_even = jnp.concatenate([v_kv, ones64, ones128], axis=1)
        v_odd = jnp.concatenate([ones64, v_kv, ones128], axis=1)
        ps, ms = [], []
        for g in range(GQA_GROUP):
            s = s_all[g * BLOCK:(g + 1) * BLOCK] + bias
            m = jnp.max(s, axis=-1, keepdims=True)
            if has_sink:
                m = jnp.maximum(m, sink_ref[h0 + g] * LOG2E)
            ps.append(jnp.exp2(s - m).astype(jnp.bfloat16))
            ms.append(m)
        acc_e = jnp.dot(jnp.concatenate(ps[0::2], axis=0), v_even,
                        preferred_element_type=jnp.float32)
        acc_o = jnp.dot(jnp.concatenate(ps[1::2], axis=0), v_odd,
                        preferred_element_type=jnp.float32)
        for t in range(GQA_GROUP // 2):
            h = h0 + 2 * t
            rows = slice(t * BLOCK, (t + 1) * BLOCK)
            num = jnp.where(low, acc_e[rows, :LANES], acc_o[rows, :LANES])
            den = jnp.where(low, acc_e[rows, LANES:], acc_o[rows, LANES:])
            m2 = jnp.where(low, jnp.broadcast_to(ms[2 * t], (BLOCK, LANES)),
                           jnp.broadcast_to(ms[2 * t + 1], (BLOCK, LANES)))
            if has_sink:
                sink2 = jnp.where(low, sink_ref[h] * LOG2E, sink_ref[h + 1] * LOG2E)
                den = den + jnp.exp2(sink2 - m2)
            cols = slice(h * HEAD_DIM, (h + 2) * HEAD_DIM)
            o_ref[:, cols] = (num / den).astype(o_ref.dtype)
            if not has_sink:
                lse_ref[:, cols] = m2 * (1.0 / LOG2E) + jnp.log(den)


def _band_attention(h, max_j, dilation, sinks=None):
    r, l, _ = h.shape
    nb = l // BLOCK
    kcol, vcol = Q_DIM // KV_DIM, Q_DIM // KV_DIM + 1
    has_sink = sinks is not None
    rows = GQA_GROUP * BLOCK
    in_specs = [pl.BlockSpec((None, BLOCK, Q_DIM), lambda s, i: (s, i, 0)),
                pl.BlockSpec((None, BLOCK, KV_DIM), lambda s, i: (s, jnp.maximum(i - 1, 0), kcol)),
                pl.BlockSpec((None, BLOCK, KV_DIM), lambda s, i: (s, i, kcol)),
                pl.BlockSpec((None, BLOCK, KV_DIM), lambda s, i: (s, jnp.maximum(i - 1, 0), vcol)),
                pl.BlockSpec((None, BLOCK, KV_DIM), lambda s, i: (s, i, vcol)),
                pl.BlockSpec((N_KV, rows, LANES), lambda s, i: (0, 0, 0)),
                pl.BlockSpec((2 * BLOCK, LANES), lambda s, i: (0, 0))]
    o_spec = pl.BlockSpec((None, BLOCK, Q_DIM), lambda s, i: (s, i, 0))
    args = [h, h, h, h, h, _band_q_columns(dilation), _band_k_columns()]
    if has_sink:
        in_specs = [pl.BlockSpec(memory_space=pltpu.SMEM)] + in_specs
        args = [sinks.astype(jnp.float32)] + args
        out_specs = o_spec
        out_shape = jax.ShapeDtypeStruct((r, l, Q_DIM), jnp.bfloat16)
    else:
        out_specs = [o_spec, o_spec]
        out_shape = [jax.ShapeDtypeStruct((r, l, Q_DIM), jnp.float32),
                     jax.ShapeDtypeStruct((r, l, Q_DIM), jnp.float32)]
    return pl.pallas_call(
        functools.partial(_band_kernel, max_j=max_j, has_sink=has_sink, ahead=BAND_AHEAD),
        grid=(r, nb),
        in_specs=in_specs,
        out_specs=out_specs,
        out_shape=out_shape,
        compiler_params=_params("parallel", "parallel"),
        name="band_attention",
    )(*args)


def _merge_kernel(*refs, dils):
    n_g = len(dils)
    o_refs, l_refs, out_ref, bufs = refs[:n_g], refs[n_g:2 * n_g], refs[2 * n_g], list(refs[2 * n_g + 1:])

    def natural(ref, d):
        if d == 1:
            return ref[0]
        buf = bufs.pop()
        per = ref.shape[1]
        for c in range(buf.shape[0]):
            for r in range(d):
                buf[c, pl.ds(r, per, stride=d), :] = ref[r, :, c * LANES:(c + 1) * LANES]
        return jnp.concatenate([buf[c] for c in range(buf.shape[0])], axis=1)

    os = [natural(ref, d) for ref, d in zip(o_refs, dils)]
    ls = [natural(ref, d) for ref, d in zip(l_refs, dils)]
    m = functools.reduce(jnp.maximum, ls)
    es = [jnp.exp(l - m) for l in ls]
    num = sum(e * o for e, o in zip(es, os))
    out_ref[...] = (num / sum(es)).astype(out_ref.dtype)


def _merge_groups(outs, lses, dils, tm):
    b, _, s, n = outs[0].shape
    tm = min(tm, s)
    assert s % tm == 0 and all(tm % (8 * d) == 0 for d in dils)
    per_batch = s // tm
    specs = [pl.BlockSpec((None, d, tm // d, n), lambda i: (i // per_batch, 0, i % per_batch, 0))
             for d in dils]
    n_buf = 2 * sum(d > 1 for d in dils)
    return pl.pallas_call(
        functools.partial(_merge_kernel, dils=tuple(dils)),
        grid=(b * per_batch,),
        in_specs=specs + specs,
        out_specs=pl.BlockSpec((tm, n), lambda i: (i, 0)),
        out_shape=jax.ShapeDtypeStruct((b * s, n), jnp.bfloat16),
        scratch_shapes=[pltpu.VMEM((n // LANES, tm, LANES), jnp.float32)] * n_buf,
        compiler_params=_params("parallel"),
        name="merge_groups",
    )(*outs, *lses)


def _count_tiles(n_tiles, tile_hits, tq, tcnt):
    def body(t, acc):
        hit = tile_hits(pl.multiple_of(t * tcnt, tcnt)).astype(jnp.float32)
        part = hit[:, :LANES]
        for c in range(1, tcnt // LANES):
            part = part + hit[:, c * LANES:(c + 1) * LANES]
        return acc + part
    acc = lax.fori_loop(0, n_tiles, body, jnp.zeros((tq, LANES), jnp.float32))
    return jnp.sum(acc, axis=-1, keepdims=True)


def _dsa_select_kernel(qia_ref, qib_ref, ki_ref, wi_ref, bias_ref, key_ref, *, top_k, tq, tsc, tcnt):
    i = pl.program_id(1)
    s_len = key_ref.shape[1]
    n_keys = (i + 1) * tq
    n_sc = (n_keys + tsc - 1) // tsc
    n_cnt = (n_keys + tcnt - 1) // tcnt
    rowpos = lax.broadcasted_iota(jnp.int32, (tq, 1), 0) + i * tq
    col_sc = lax.broadcasted_iota(jnp.int32, (tq, tsc), 1)
    col_cnt = lax.broadcasted_iota(jnp.int32, (tq, tcnt), 1)

    key_ref[:, pl.ds(pl.multiple_of(n_cnt * tcnt - tsc, tsc), tsc)] = jnp.full((tq, tsc), INT_MIN, jnp.int32)

    half = IDX_HEADS // 2
    hpd = min(half, 8 * 256 // tq)
    qis = [jnp.concatenate([ref[:, h * IDX_DIM:(h + 1) * IDX_DIM] for h in range(h0, h0 + hpd)], axis=0)
           for ref in (qia_ref, qib_ref) for h0 in range(0, half, hpd)]
    wi = wi_ref[...]

    def score_body(t, carry):
        start = pl.multiple_of(t * tsc, tsc)
        ki = ki_ref[pl.ds(start, tsc), :][:, :IDX_DIM].astype(jnp.bfloat16)
        sc = jnp.zeros((tq, tsc), jnp.float32)
        for d, q in enumerate(qis):
            rel = lax.dot_general(q, ki, (((1,), (1,)), ((), ())),
                                  preferred_element_type=jnp.float32)
            for g in range(hpd):
                h = d * hpd + g
                w_h = wi[:, IDX_DIM + h:IDX_DIM + h + 1]
                sc = sc + jnp.maximum(rel[g * tq:(g + 1) * tq], 0.0) * w_h
        sc = sc + 0.0
        bits = pltpu.bitcast(sc, jnp.int32)
        keys = jnp.where(bits < 0, bits ^ jnp.int32(0x7FFFFFFF), bits)
        keys = jnp.where(col_sc + start <= rowpos, keys, jnp.int32(INT_MIN))
        key_ref[:, pl.ds(start, tsc)] = keys
        return carry

    lax.fori_loop(0, n_sc, score_body, 0)

    kf = jnp.float32(top_k)

    def count_ge(cand):
        return _count_tiles(n_cnt, lambda st: key_ref[:, pl.ds(st, tcnt)] >= cand, tq, tcnt)

    short = (rowpos + 1 < top_k).astype(jnp.float32)

    def bit_cond(c):
        b, _, done, _ = c
        return (b < 32) & (jnp.min(done) < 0.5)

    def bit_body(c):
        b, prefix, done, thr = c
        cand = prefix ^ lax.shift_left(jnp.int32(1), jnp.int32(31) - b)
        cnt = count_ge(cand)
        prefix = jnp.where(cnt >= kf, cand, prefix)
        hit = (cnt == kf) & (done < 0.5)
        thr = jnp.where(hit, cand - 1, thr)
        done = jnp.where(hit, 1.0, done)
        return b + 1, prefix, done, thr

    init = (jnp.int32(0), jnp.full((tq, 1), INT_MIN, jnp.int32), short,
            jnp.full((tq, 1), INT_MIN, jnp.int32))
    _, prefix, done, thr = lax.while_loop(bit_cond, bit_body, init)
    is_done = done > 0.5
    thr = jnp.where(is_done, thr, prefix)

    def tie_search(_):
        n_gt = _count_tiles(n_cnt, lambda st: key_ref[:, pl.ds(st, tcnt)] > thr, tq, tcnt)
        need = kf - n_gt
        n_idx_bits = max(1, int(math.ceil(math.log2(s_len))))

        def idx_body(b, p):
            cand = p | lax.shift_left(jnp.int32(1), jnp.int32(n_idx_bits - 1) - b)
            below = _count_tiles(
                n_cnt, lambda st: (key_ref[:, pl.ds(st, tcnt)] == thr) & (col_cnt + st < cand), tq, tcnt)
            return jnp.where(below <= need - 1.0, cand, p)

        return lax.fori_loop(0, n_idx_bits, idx_body, jnp.zeros((tq, 1), jnp.int32))

    last_tie = lax.cond(jnp.min(done) < 0.5, tie_search,
                        lambda _: jnp.full((tq, 1), -1, jnp.int32), 0)
    last_tie = jnp.where(is_done, -1, last_tie)

    def write_body(t, carry):
        start = pl.multiple_of(t * tcnt, tcnt)
        kt = key_ref[:, pl.ds(start, tcnt)]
        col = col_cnt + start
        sel = ((kt > thr) | ((kt == thr) & (col <= last_tie))) & (col <= rowpos)
        bias_ref[:, pl.ds(start, tcnt)] = jnp.where(sel, 0.0, MASKED).astype(bias_ref.dtype)
        return carry

    lax.fori_loop(0, n_cnt, write_body, 0)

    def fill_body(t, carry):
        bias_ref[:, pl.ds(pl.multiple_of(t * tcnt, tcnt), tcnt)] = jnp.full(
            (tq, tcnt), MASKED, bias_ref.dtype)
        return carry

    lax.fori_loop(n_cnt, s_len // tcnt, fill_body, 0)


def _dsa_select(hm, kiwi, tq, tsc, tcnt):
    b, s, _ = hm.shape
    tq, tsc, tcnt = min(tq, s), min(tsc, s), min(tcnt, s)
    assert tcnt % tsc == 0 and s % tcnt == 0 and s % tq == 0
    top_k = min(TOPK_MAX, s // 4)
    qi_half = IDX_HEADS * IDX_DIM // 2
    qicol = GRP_IN // qi_half
    return pl.pallas_call(
        functools.partial(_dsa_select_kernel, top_k=top_k, tq=tq, tsc=tsc, tcnt=tcnt),
        grid=(b, s // tq),
        in_specs=[pl.BlockSpec((None, tq, qi_half), lambda bb, i: (bb, i, qicol)),
                  pl.BlockSpec((None, tq, qi_half), lambda bb, i: (bb, i, qicol + 1)),
                  pl.BlockSpec((None, s, LANES), lambda bb, i: (bb, 0, 0)),
                  pl.BlockSpec((None, tq, LANES), lambda bb, i: (bb, i, 0))],
        out_specs=pl.BlockSpec((None, tq, s), lambda bb, i: (bb, i, 0)),
        out_shape=jax.ShapeDtypeStruct((b, s, s), jnp.bfloat16),
        scratch_shapes=[pltpu.VMEM((tq, s), jnp.int32)],
        compiler_params=_params("parallel", "parallel"),
        name="dsa_select",
    )(hm, hm, kiwi, kiwi)


def _dsa_attn_kernel(q_ref, k_ref, v_ref, bias_ref, sl_ref, o_ref,
                     kaug_ref, vaug_ref, m_ref, acc_ref, *, tk, chunk):
    i = pl.program_id(1)
    s_len = k_ref.shape[0]

    @pl.when(i == 0)
    def _():
        lane = lax.broadcasted_iota(jnp.int32, (chunk, HEAD_DIM), 1)
        ones_col = (lane == 0).astype(jnp.bfloat16)
        for c in range(s_len // chunk):
            pos = lax.broadcasted_iota(jnp.int32, (chunk, HEAD_DIM), 0) + c * chunk
            hi = pos - (pos & (POS_SPLIT - 1))
            lo = pos & (POS_SPLIT - 1)
            pcols = jnp.where(lane < N_SLOPE_PIECES, hi, jnp.where(lane < 2 * N_SLOPE_PIECES, lo, 0))
            pcols = pcols.astype(jnp.float32).astype(jnp.bfloat16)
            rows = pl.ds(c * chunk, chunk)
            for kv in range(N_KV):
                kaug_ref[kv, rows, :] = jnp.concatenate(
                    [k_ref[rows, kv * HEAD_DIM:(kv + 1) * HEAD_DIM], pcols], axis=1)
                vaug_ref[kv, rows, :] = jnp.concatenate(
                    [v_ref[rows, kv * HEAD_DIM:(kv + 1) * HEAD_DIM], ones_col], axis=1)

    n_tiles = (i * BLOCK + BLOCK + tk - 1) // tk
    m_ref[...] = jnp.full(m_ref.shape, MASKED, jnp.float32)
    acc_ref[...] = jnp.zeros(acc_ref.shape, jnp.float32)
    qa = [jnp.concatenate(
        [jnp.concatenate(
            [q_ref[:, (kv * GQA_GROUP + g) * HEAD_DIM:(kv * GQA_GROUP + g + 1) * HEAD_DIM]
             for g in range(GQA_GROUP)], axis=0), sl_ref[kv][:, HEAD_DIM:]], axis=1)
        for kv in range(N_KV)]
    n_rep = tk // LANES

    def attn_body(t, carry):
        start = pl.multiple_of(t * tk, tk)
        bias = bias_ref[:, pl.ds(start, tk)].astype(jnp.float32)
        scores = [lax.dot_general(qa[kv], kaug_ref[kv, pl.ds(start, tk), :], (((1,), (1,)), ((), ())),
                                  preferred_element_type=jnp.float32) for kv in range(N_KV)]
        for kv in range(N_KV):
            s_all = scores[kv]
            m_old = m_ref[kv]
            ps, m_news = [], []
            for g in range(GQA_GROUP):
                s = s_all[g * BLOCK:(g + 1) * BLOCK] + bias
                m_new = jnp.maximum(m_old[g * BLOCK:(g + 1) * BLOCK],
                                    jnp.max(s, axis=-1, keepdims=True))
                ps.append(jnp.exp2(s - jnp.concatenate([m_new] * n_rep, axis=1)).astype(jnp.bfloat16))
                m_news.append(m_new)
            m_new = jnp.concatenate(m_news, axis=0)
            pv = jnp.dot(jnp.concatenate(ps, axis=0), vaug_ref[kv, pl.ds(start, tk), :],
                         preferred_element_type=jnp.float32)
            acc_ref[kv] = jnp.exp2(m_old - m_new) * acc_ref[kv] + pv
            m_ref[kv] = m_new
        return carry

    lax.fori_loop(0, n_tiles, attn_body, 0)

    for kv in range(N_KV):
        acc = acc_ref[kv]
        o_all = (acc[:, :HEAD_DIM] / acc[:, HEAD_DIM:HEAD_DIM + 1]).astype(o_ref.dtype)
        for g in range(0, GQA_GROUP, 2):
            h = kv * GQA_GROUP + g
            o_ref[:, h * HEAD_DIM:(h + 2) * HEAD_DIM] = jnp.concatenate(
                [o_all[g * BLOCK:(g + 1) * BLOCK], o_all[(g + 1) * BLOCK:(g + 2) * BLOCK]], axis=1)


def _dsa_attention(hm, bias, tk):
    b, s, _ = hm.shape
    tk = min(tk, s)
    chunk = min(512, s)
    kcol = Q_DIM // KV_DIM
    rows = GQA_GROUP * BLOCK
    return pl.pallas_call(
        functools.partial(_dsa_attn_kernel, tk=tk, chunk=chunk),
        grid=(b, s // BLOCK),
        in_specs=[pl.BlockSpec((None, BLOCK, Q_DIM), lambda bb, i: (bb, i, 0)),
                  pl.BlockSpec((None, s, KV_DIM), lambda bb, i: (bb, 0, kcol)),
                  pl.BlockSpec((None, s, KV_DIM), lambda bb, i: (bb, 0, kcol + 1)),
                  pl.BlockSpec((None, BLOCK, s), lambda bb, i: (bb, i, 0)),
                  pl.BlockSpec((N_KV, rows, LANES), lambda bb, i: (0, 0, 0))],
        out_specs=pl.BlockSpec((None, BLOCK, Q_DIM), lambda bb, i: (bb, i, 0)),
        out_shape=jax.ShapeDtypeStruct((b, s, Q_DIM), jnp.bfloat16),
        scratch_shapes=[pltpu.VMEM((N_KV, s, LANES), jnp.bfloat16),
                        pltpu.VMEM((N_KV, s, LANES), jnp.bfloat16),
                        pltpu.VMEM((N_KV, rows, LANES), jnp.float32),
                        pltpu.VMEM((N_KV, rows, LANES), jnp.float32)],
        compiler_params=_params("parallel", "arbitrary"),
        name="dsa_attention",
    )(hm, hm, hm, bias, _slope_columns(1, 2))


_QKV_SCALE = np.concatenate([np.full(Q_DIM, ATTN_SCALE * LOG2E), np.ones(2 * KV_DIM)])
_DSA_SCALE = np.concatenate([_QKV_SCALE, np.full(IDX_HEADS * IDX_DIM, IDX_DIM ** -0.5)])


def _swa_layer(xb, w_in, layer, sinks, b, s):
    h = _matmul(xb, w_in, layer, 0, _QKV_SCALE, jnp.bfloat16, 1024, 1280)
    o = _band_attention(h.reshape(b, s, GRP_IN), SWA_WINDOW - 1, 1, sinks)
    return o.reshape(b * s, Q_DIM)


def _dilated_layer(xb, w_in, layer, b, s):
    outs, lses, dils = [], [], []
    for g, (window, dil) in enumerate(DIL_PATTERNS):
        h = _matmul(xb, w_in, layer, g * GRP_IN, _QKV_SCALE, jnp.bfloat16, 1024, 1280, batch=b, dil=dil)
        o, lse = _band_attention(h.reshape(b * dil, s // dil, GRP_IN), window // dil, dil)
        outs.append(o.reshape(b, dil, s // dil, Q_DIM))
        lses.append(lse.reshape(b, dil, s // dil, Q_DIM))
        dils.append(dil)
    return _merge_groups(outs, lses, dils, 256)


def _dsa_layer(xb, w_in, layer, b, s):
    hm = _matmul(xb, w_in, layer, 0, _DSA_SCALE, jnp.bfloat16, 2048, 512).reshape(b, s, C_MAIN)
    w_small = jnp.concatenate([w_in[layer, :, C_MAIN:C_MAIN + IDX_DIM],
                               w_in[layer, :, C_MAIN + IDX_DIM:] * IDX_HEADS ** -0.5,
                               jnp.zeros((D_MODEL, LANES - IDX_DIM - IDX_HEADS), w_in.dtype)], axis=1)
    kiwi = _matmul(xb, w_small[None], 0, 0, np.ones(LANES), jnp.float32, 1024, LANES).reshape(b, s, LANES)
    bias = _dsa_select(hm, kiwi, 256, 512, 512)
    o = _dsa_attention(hm, bias, 512)
    return o.reshape(b * s, Q_DIM)


def kernel(x, a_w_in, a_sinks, a_w_out, b_w_in, b_w_out, c_w_in, c_w_out, ln_g, ln_b,
           ffn_w_gate_up, ffn_w_down):
    b, s, d = x.shape
    xf = x.reshape(b * s, d)
    xb = xf.astype(jnp.bfloat16)
    for i in range(DEPTH):
        kind, j = i % N_MIXERS, i // N_MIXERS
        if kind == 0:
            o = _swa_layer(xb, a_w_in, j, a_sinks[j], b, s)
            w_out = a_w_out
        elif kind == 1:
            o = _dilated_layer(xb, b_w_in, j, b, s)
            w_out = b_w_out
        else:
            o = _dsa_layer(xb, c_w_in, j, b, s)
            w_out = c_w_out
        xf, xb = _proj_residual_ln(o, w_out, j, xf, ln_g[i, 0], ln_b[i, 0], 512, 512)
        hmid = _ffn_gate_up(xb, ffn_w_gate_up, i, 1024, 512)
        xf, xb = _proj_residual_ln(hmid, ffn_w_down, i, xf, ln_g[i, 1], ln_b[i, 1], 256, 512)
    return xf.reshape(b, s, d)
```

```python
import functools
import math

import numpy as np
import jax
import jax.numpy as jnp
from jax import lax
from jax.experimental import pallas as pl
from jax.experimental.pallas import tpu as pltpu

D_MODEL = 2048
DEPTH = 4
N_MIXERS = 3
HEAD_DIM = 64
N_HEADS = D_MODEL // HEAD_DIM
N_KV = N_HEADS // 8
GQA_GROUP = N_HEADS // N_KV
Q_DIM = N_HEADS * HEAD_DIM
KV_DIM = N_KV * HEAD_DIM
ATTN_SCALE = HEAD_DIM ** -0.5
BLOCK = 128
SWA_WINDOW = 128
DIL_PATTERNS = ((128, 1), (512, 4), (2048, 16))
N_DIL = len(DIL_PATTERNS)
IDX_HEADS = 16
IDX_DIM = 64
TOPK_MAX = 256
D_FF = 256 * math.ceil(8 * D_MODEL / (3 * 256))
DEEPNORM_ALPHA = (2 * DEPTH) ** 0.25
LN_EPS = 1e-5
GRP_IN = Q_DIM + 2 * KV_DIM
C_MAIN = Q_DIM + 2 * KV_DIM + IDX_HEADS * IDX_DIM
C_IN = C_MAIN + IDX_DIM + IDX_HEADS
LANES = 128
MASKED = -1e30
INT_MIN = -2 ** 31
LOG2E = 1.4426950408889634
VMEM_LIMIT = 56 * 1024 * 1024

_SLOPES = [float(np.float32(2.0 ** (-8.0 * (i + 1) / N_HEADS))) for i in range(N_HEADS)]
N_SLOPE_PIECES = 3
POS_SPLIT = 64
BAND_Q_BLOCKS = 1
BAND_AHEAD = 4


def _params(*sem):
    return pltpu.CompilerParams(dimension_semantics=sem, vmem_limit_bytes=VMEM_LIMIT)


def _mm_kernel(x_ref, w_ref, sc_ref, o_ref, wb_ref, *acc, dil):
    @pl.when(pl.program_id(1) == 0)
    def _():
        wb_ref[...] = (w_ref[...] * sc_ref[...]).astype(jnp.bfloat16)

    y = jnp.dot(x_ref[...].astype(jnp.bfloat16), wb_ref[...], preferred_element_type=jnp.float32)
    if dil == 1:
        o_ref[...] = y.astype(o_ref.dtype)
    else:
        acc_ref, = acc
        per = acc_ref.shape[1] // dil
        for c in range(acc_ref.shape[0]):
            cols = slice(c * LANES, (c + 1) * LANES)
            acc_ref[c] = y[:, cols]
            for r in range(dil):
                o_ref[r, :, cols] = acc_ref[c, pl.ds(r, per, stride=dil), :].astype(o_ref.dtype)


def _matmul(x, w, layer, col0, scale, out_dtype, tm, tn, batch=1, dil=1):
    m, k = x.shape
    n = scale.shape[0]
    tm, tn = min(tm, m // batch), min(tn, n)
    assert m % (batch * tm) == 0 and n % tn == 0 and col0 % tn == 0 and tm % (16 * dil) == 0
    j0 = col0 // tn
    if dil == 1:
        out_specs = pl.BlockSpec((tm, tn), lambda j, i: (i, j))
        out_shape = jax.ShapeDtypeStruct((m, n), out_dtype)
        scratch = []
    else:
        per_batch = m // batch // tm
        out_specs = pl.BlockSpec((None, dil, tm // dil, tn),
                                 lambda j, i: (i // per_batch, 0, i % per_batch, j))
        out_shape = jax.ShapeDtypeStruct((batch, dil, m // batch // dil, n), out_dtype)
        scratch = [pltpu.VMEM((tn // LANES, tm, LANES), jnp.float32)]
    return pl.pallas_call(
        functools.partial(_mm_kernel, dil=dil),
        grid=(n // tn, m // tm),
        in_specs=[pl.BlockSpec((tm, k), lambda j, i: (i, 0)),
                  pl.BlockSpec((None, k, tn), lambda j, i: (layer, 0, j0 + j)),
                  pl.BlockSpec((1, tn), lambda j, i: (0, j))],
        out_specs=out_specs,
        out_shape=out_shape,
        scratch_shapes=[pltpu.VMEM((k, tn), jnp.bfloat16)] + scratch,
        compiler_params=_params("parallel", "arbitrary"),
        name="proj_matmul",
    )(x, w, jnp.asarray(scale, jnp.float32).reshape(1, n))


def _gate_up_kernel(x_ref, wg_ref, wu_ref, o_ref, wgb_ref, wub_ref):
    @pl.when(pl.program_id(1) == 0)
    def _():
        wgb_ref[...] = wg_ref[...].astype(jnp.bfloat16)
        wub_ref[...] = wu_ref[...].astype(jnp.bfloat16)

    x = x_ref[...]
    g = jnp.dot(x, wgb_ref[...], preferred_element_type=jnp.float32)
    u = jnp.dot(x, wub_ref[...], preferred_element_type=jnp.float32)
    o_ref[...] = (g * (1.0 / (1.0 + jnp.exp(-g))) * u).astype(o_ref.dtype)


def _ffn_gate_up(x, w, layer, tm, tn):
    m, k = x.shape
    d_ff = w.shape[2] // 2
    tm = min(tm, m)
    assert m % tm == 0 and d_ff % tn == 0
    nj = d_ff // tn
    return pl.pallas_call(
        _gate_up_kernel,
        grid=(nj, m // tm),
        in_specs=[pl.BlockSpec((tm, k), lambda j, i: (i, 0)),
                  pl.BlockSpec((None, k, tn), lambda j, i: (layer, 0, j)),
                  pl.BlockSpec((None, k, tn), lambda j, i: (layer, 0, j + nj))],
        out_specs=pl.BlockSpec((tm, tn), lambda j, i: (i, j)),
        out_shape=jax.ShapeDtypeStruct((m, d_ff), jnp.bfloat16),
        scratch_shapes=[pltpu.VMEM((k, tn), jnp.bfloat16), pltpu.VMEM((k, tn), jnp.bfloat16)],
        compiler_params=_params("parallel", "arbitrary"),
        name="ffn_gate_up",
    )(x, w, w)


def _proj_ln_kernel(a_ref, w_ref, r_ref, g_ref, b_ref, o_ref, ob_ref, wb_ref, *, n_w, tkc):
    s = pl.program_id(0)

    @pl.when(s < n_w)
    def _():
        wb_ref[pl.ds(pl.multiple_of(s * tkc, tkc), tkc), :] = w_ref[...].astype(jnp.bfloat16)

    @pl.when(s >= n_w)
    def _():
        z = DEEPNORM_ALPHA * r_ref[...] + jnp.dot(a_ref[...], wb_ref[...],
                                                  preferred_element_type=jnp.float32)
        mu = jnp.mean(z, axis=-1, keepdims=True)
        zc = z - mu
        var = jnp.mean(zc * zc, axis=-1, keepdims=True)
        y = zc * lax.rsqrt(var + LN_EPS) * g_ref[...] + b_ref[...]
        o_ref[...] = y
        ob_ref[...] = y.astype(jnp.bfloat16)


def _proj_residual_ln(a, w, layer, resid, g, b, tm, tkc):
    m, k = a.shape
    n = w.shape[2]
    tm, tkc = min(tm, m), min(tkc, k)
    assert m % tm == 0 and k % tkc == 0
    n_w = k // tkc
    row = lambda s: (jnp.maximum(s - n_w, 0), 0)
    return pl.pallas_call(
        functools.partial(_proj_ln_kernel, n_w=n_w, tkc=tkc),
        grid=(n_w + m // tm,),
        in_specs=[pl.BlockSpec((tm, k), row),
                  pl.BlockSpec((None, tkc, n), lambda s: (layer, jnp.minimum(s, n_w - 1), 0)),
                  pl.BlockSpec((tm, n), row),
                  pl.BlockSpec((1, n), lambda s: (0, 0)),
                  pl.BlockSpec((1, n), lambda s: (0, 0))],
        out_specs=[pl.BlockSpec((tm, n), row), pl.BlockSpec((tm, n), row)],
        out_shape=[jax.ShapeDtypeStruct((m, n), jnp.float32),
                   jax.ShapeDtypeStruct((m, n), jnp.bfloat16)],
        scratch_shapes=[pltpu.VMEM((k, n), jnp.bfloat16)],
        compiler_params=_params("arbitrary"),
        name="proj_residual_ln",
    )(a, w, resid, g.reshape(1, n), b.reshape(1, n))


def _bf16_pieces(x, n):
    rest = np.asarray(x, np.float64)
    pieces = []
    for _ in range(n):
        p = rest.astype(np.float32).astype(jnp.bfloat16).astype(np.float64)
        pieces.append(p)
        rest = rest - p
    return pieces


def _slope_columns(mult, n_parts):
    out = np.zeros((N_KV, GQA_GROUP * BLOCK, LANES), np.float32)
    for h in range(N_HEADS):
        pieces = [float(p) for p in _bf16_pieces(np.float64(_SLOPES[h]) * mult * LOG2E, N_SLOPE_PIECES)]
        kv, g = divmod(h, GQA_GROUP)
        out[kv, g * BLOCK:(g + 1) * BLOCK, HEAD_DIM:HEAD_DIM + n_parts * N_SLOPE_PIECES] = np.asarray(
            pieces * n_parts, np.float32)
    return jnp.asarray(out, jnp.bfloat16)


def _band_q_columns(dilation):
    out = np.zeros((N_KV, GQA_GROUP * BLOCK, LANES), np.float32)
    qpos = BLOCK + np.arange(BLOCK, dtype=np.float64)
    for h in range(N_HEADS):
        s2 = np.float64(_SLOPES[h]) * dilation * LOG2E
        kv, g = divmod(h, GQA_GROUP)
        rows = slice(g * BLOCK, (g + 1) * BLOCK)
        for c, p in enumerate(_bf16_pieces(s2, N_SLOPE_PIECES)):
            out[kv, rows, HEAD_DIM + c] = p
        for c, p in enumerate(_bf16_pieces(-s2 * qpos, N_SLOPE_PIECES)):
            out[kv, rows, HEAD_DIM + N_SLOPE_PIECES + c] = p
    return jnp.asarray(out, jnp.bfloat16)


def _band_k_columns():
    out = np.zeros((2 * BLOCK, LANES), np.float32)
    out[:, HEAD_DIM:HEAD_DIM + N_SLOPE_PIECES] = np.arange(2 * BLOCK, dtype=np.float32)[:, None]
    out[:, HEAD_DIM + N_SLOPE_PIECES:HEAD_DIM + 2 * N_SLOPE_PIECES] = 1.0
    return jnp.asarray(out, jnp.bfloat16)


def _band_kernel(*refs, max_j, has_sink, ahead):
    if has_sink:
        sink_ref, q_ref, kp_ref, kc_ref, vp_ref, vc_ref, qc_ref, kc_cols_ref, o_ref = refs
    else:
        q_ref, kp_ref, kc_ref, vp_ref, vc_ref, qc_ref, kc_cols_ref, o_ref, lse_ref = refs
    i = pl.program_id(1)
    n_qb = q_ref.shape[0] // BLOCK
    k = jnp.concatenate([kp_ref[...], kc_ref[...]], axis=0)
    v = jnp.concatenate([vp_ref[...], vc_ref[...]], axis=0)
    qpos = lax.broadcasted_iota(jnp.int32, (BLOCK, 2 * BLOCK), 0) + BLOCK
    kpos = lax.broadcasted_iota(jnp.int32, (BLOCK, 2 * BLOCK), 1)
    j = qpos - kpos
    in_band = (j >= 0) & (j <= max_j)
    bias_rest = jnp.where(in_band, 0.0, MASKED)
    bias_first = jnp.where(in_band & ((kpos >= BLOCK) | (i > 0)), 0.0, MASKED)
    low = lax.broadcasted_iota(jnp.int32, (BLOCK, LANES), 1) < HEAD_DIM
    k_cols = kc_cols_ref[:, HEAD_DIM:]
    ones64 = jnp.ones((2 * BLOCK, HEAD_DIM), jnp.bfloat16)
    ones128 = jnp.ones((2 * BLOCK, LANES), jnp.bfloat16)
    units = [(b, kv) for b in range(n_qb) for kv in range(N_KV)]

    def scores(b, kv):
        ka = jnp.concatenate([k[b * BLOCK:(b + 2) * BLOCK, kv * HEAD_DIM:(kv + 1) * HEAD_DIM], k_cols],
                             axis=1)
        qa = jnp.concatenate(
            [jnp.concatenate([q_ref[b * BLOCK:(b + 1) * BLOCK,
                                    (kv * GQA_GROUP + g) * HEAD_DIM:(kv * GQA_GROUP + g + 1) * HEAD_DIM]
                              for g in range(GQA_GROUP)], axis=0),
             qc_ref[kv][:, HEAD_DIM:]], axis=1)
        return lax.dot_general(qa, ka, (((1,), (1,)), ((), ())),
                               preferred_element_type=jnp.float32)

    pending = [scores(*units[u]) for u in range(min(ahead, len(units)))]
    for u, (b, kv) in enumerate(units):
        s_all = pending.pop(0)
        if u + ahead < len(units):
            pending.append(scores(*units[u + ahead]))
        h0 = kv * GQA_GROUP
        bias = bias_first if b == 0 else bias_rest
        out_rows = slice(b * BLOCK, (b + 1) * BLOCK)
        v_kv = v[b * BLOCK:(b + 2) * BLOCK, kv * HEAD_DIM:(kv + 1) * HEAD_DIM]
        v_even = jnp.concatenate([v_kv, ones64, ones128], axis=1)
        v_odd = jnp.concatenate([ones64, v_kv, ones128], axis=1)
        ps, ms = [], []
        for g in range(GQA_GROUP):
            s = s_all[g * BLOCK:(g + 1) * BLOCK] + bias
            m = jnp.max(s, axis=-1, keepdims=True)
            if has_sink:
                m = jnp.maximum(m, sink_ref[h0 + g] * LOG2E)
            ps.append(jnp.exp2(s - m).astype(jnp.bfloat16))
            ms.append(m)
        acc_e = jnp.dot(jnp.concatenate(ps[0::2], axis=0), v_even,
                        preferred_element_type=jnp.float32)
        acc_o = jnp.dot(jnp.concatenate(ps[1::2], axis=0), v_odd,
                        preferred_element_type=jnp.float32)
        for t in range(GQA_GROUP // 2):
            h = h0 + 2 * t
            rows = slice(t * BLOCK, (t + 1) * BLOCK)
            num = jnp.where(low, acc_e[rows, :LANES], acc_o[rows, :LANES])
            den = jnp.where(low, acc_e[rows, LANES:], acc_o[rows, LANES:])
            m2 = jnp.where(low, jnp.broadcast_to(ms[2 * t], (BLOCK, LANES)),
                           jnp.broadcast_to(ms[2 * t + 1], (BLOCK, LANES)))
            if has_sink:
                sink2 = jnp.where(low, sink_ref[h] * LOG2E, sink_ref[h + 1] * LOG2E)
                den = den + jnp.exp2(sink2 - m2)
            cols = slice(h * HEAD_DIM, (h + 2) * HEAD_DIM)
            o_ref[out_rows, cols] = (num / den).astype(o_ref.dtype)
            if not has_sink:
                lse_ref[out_rows, cols] = m2 * (1.0 / LOG2E) + jnp.log(den)


def _band_attention(h, max_j, dilation, sinks=None):
    r, l, _ = h.shape
    n_qb = min(BAND_Q_BLOCKS, l // BLOCK)
    tq = n_qb * BLOCK
    assert l % tq == 0
    kcol, vcol = Q_DIM // KV_DIM, Q_DIM // KV_DIM + 1
    has_sink = sinks is not None
    rows = GQA_GROUP * BLOCK
    prev = lambda s, i, col: (s, jnp.maximum(i * n_qb - 1, 0), col)
    in_specs = [pl.BlockSpec((None, tq, Q_DIM), lambda s, i: (s, i, 0)),
                pl.BlockSpec((None, BLOCK, KV_DIM), lambda s, i: prev(s, i, kcol)),
                pl.BlockSpec((None, tq, KV_DIM), lambda s, i: (s, i, kcol)),
                pl.BlockSpec((None, BLOCK, KV_DIM), lambda s, i: prev(s, i, vcol)),
                pl.BlockSpec((None, tq, KV_DIM), lambda s, i: (s, i, vcol)),
                pl.BlockSpec((N_KV, rows, LANES), lambda s, i: (0, 0, 0)),
                pl.BlockSpec((2 * BLOCK, LANES), lambda s, i: (0, 0))]
    o_spec = pl.BlockSpec((None, tq, Q_DIM), lambda s, i: (s, i, 0))
    args = [h, h, h, h, h, _band_q_columns(dilation), _band_k_columns()]
    if has_sink:
        in_specs = [pl.BlockSpec(memory_space=pltpu.SMEM)] + in_specs
        args = [sinks.astype(jnp.float32)] + args
        out_specs = o_spec
        out_shape = jax.ShapeDtypeStruct((r, l, Q_DIM), jnp.bfloat16)
    else:
        out_specs = [o_spec, o_spec]
        out_shape = [jax.ShapeDtypeStruct((r, l, Q_DIM), jnp.bfloat16),
                     jax.ShapeDtypeStruct((r, l, Q_DIM), jnp.float32)]
    return pl.pallas_call(
        functools.partial(_band_kernel, max_j=max_j, has_sink=has_sink, ahead=BAND_AHEAD),
        grid=(r, l // tq),
        in_specs=in_specs,
        out_specs=out_specs,
        out_shape=out_shape,
        compiler_params=_params("parallel", "parallel"),
        name="band_attention",
    )(*args)


def _merge_kernel(*refs, dils):
    n_g = len(dils)
    o_refs, l_refs, out_ref, bufs = refs[:n_g], refs[n_g:2 * n_g], refs[2 * n_g], list(refs[2 * n_g + 1:])

    def natural(ref, d):
        if d == 1:
            return ref[0].astype(jnp.float32)
        buf = bufs.pop()
        per = ref.shape[1]
        for c in range(buf.shape[0]):
            for r in range(d):
                buf[c, pl.ds(r, per, stride=d), :] = ref[r, :, c * LANES:(c + 1) * LANES].astype(jnp.float32)
        return jnp.concatenate([buf[c] for c in range(buf.shape[0])], axis=1)

    os = [natural(ref, d) for ref, d in zip(o_refs, dils)]
    ls = [natural(ref, d) for ref, d in zip(l_refs, dils)]
    m = functools.reduce(jnp.maximum, ls)
    es = [jnp.exp(l - m) for l in ls]
    num = sum(e * o for e, o in zip(es, os))
    out_ref[...] = (num / sum(es)).astype(out_ref.dtype)


def _merge_groups(outs, lses, dils, tm):
    b, _, s, n = outs[0].shape
    tm = min(tm, s)
    assert s % tm == 0 and all(tm % (8 * d) == 0 for d in dils)
    per_batch = s // tm
    specs = [pl.BlockSpec((None, d, tm // d, n), lambda i: (i // per_batch, 0, i % per_batch, 0))
             for d in dils]
    n_buf = 2 * sum(d > 1 for d in dils)
    return pl.pallas_call(
        functools.partial(_merge_kernel, dils=tuple(dils)),
        grid=(b * per_batch,),
        in_specs=specs + specs,
        out_specs=pl.BlockSpec((tm, n), lambda i: (i, 0)),
        out_shape=jax.ShapeDtypeStruct((b * s, n), jnp.bfloat16),
        scratch_shapes=[pltpu.VMEM((n // LANES, tm, LANES), jnp.float32)] * n_buf,
        compiler_params=_params("parallel"),
        name="merge_groups",
    )(*outs, *lses)


def _count_tiles(n_tiles, tile_hits, tq, tcnt):
    def body(t, acc):
        hit = tile_hits(pl.multiple_of(t * tcnt, tcnt)).astype(jnp.float32)
        part = hit[:, :LANES]
        for c in range(1, tcnt // LANES):
            part = part + hit[:, c * LANES:(c + 1) * LANES]
        return acc + part
    acc = lax.fori_loop(0, n_tiles, body, jnp.zeros((tq, LANES), jnp.float32))
    return jnp.sum(acc, axis=-1, keepdims=True)


def _dsa_select_kernel(qia_ref, qib_ref, ki_ref, wi_ref, bias_ref, key_ref, *, top_k, tq, tsc, tcnt):
    i = pl.program_id(1)
    s_len = key_ref.shape[1]
    n_keys = (i + 1) * tq
    n_sc = (n_keys + tsc - 1) // tsc
    n_cnt = (n_keys + tcnt - 1) // tcnt
    rowpos = lax.broadcasted_iota(jnp.int32, (tq, 1), 0) + i * tq
    col_sc = lax.broadcasted_iota(jnp.int32, (tq, tsc), 1)
    col_cnt = lax.broadcasted_iota(jnp.int32, (tq, tcnt), 1)

    key_ref[:, pl.ds(pl.multiple_of(n_cnt * tcnt - tsc, tsc), tsc)] = jnp.full((tq, tsc), INT_MIN, jnp.int32)

    half = IDX_HEADS // 2
    hpd = min(half, 8 * 256 // tq)
    qis = [jnp.concatenate([ref[:, h * IDX_DIM:(h + 1) * IDX_DIM] for h in range(h0, h0 + hpd)], axis=0)
           for ref in (qia_ref, qib_ref) for h0 in range(0, half, hpd)]
    wi = wi_ref[...]

    def score_body(t, carry):
        start = pl.multiple_of(t * tsc, tsc)
        ki = ki_ref[pl.ds(start, tsc), :][:, :IDX_DIM].astype(jnp.bfloat16)
        rels = [lax.dot_general(q, ki, (((1,), (1,)), ((), ())), preferred_element_type=jnp.float32)
                for q in qis]
        sc = jnp.zeros((tq, tsc), jnp.float32)
        for d, rel in enumerate(rels):
            for g in range(hpd):
                h = d * hpd + g
                w_h = wi[:, IDX_DIM + h:IDX_DIM + h + 1]
                sc = sc + jnp.maximum(rel[g * tq:(g + 1) * tq], 0.0) * w_h
        sc = sc + 0.0
        bits = pltpu.bitcast(sc, jnp.int32)
        keys = jnp.where(bits < 0, bits ^ jnp.int32(0x7FFFFFFF), bits)
        keys = jnp.where(col_sc + start <= rowpos, keys, jnp.int32(INT_MIN))
        key_ref[:, pl.ds(start, tsc)] = keys
        return carry

    lax.fori_loop(0, n_sc, score_body, 0)

    kf = jnp.float32(top_k)

    def count_ge(cand):
        return _count_tiles(n_cnt, lambda st: key_ref[:, pl.ds(st, tcnt)] >= cand, tq, tcnt)

    short = (rowpos + 1 < top_k).astype(jnp.float32)

    def bit_cond(c):
        b, _, done, _ = c
        return (b < 32) & (jnp.min(done) < 0.5)

    def bit_body(c):
        b, prefix, done, thr = c
        cand = prefix ^ lax.shift_left(jnp.int32(1), jnp.int32(31) - b)
        cnt = count_ge(cand)
        prefix = jnp.where(cnt >= kf, cand, prefix)
        hit = (cnt == kf) & (done < 0.5)
        thr = jnp.where(hit, cand - 1, thr)
        done = jnp.where(hit, 1.0, done)
        return b + 1, prefix, done, thr

    init = (jnp.int32(0), jnp.full((tq, 1), INT_MIN, jnp.int32), short,
            jnp.full((tq, 1), INT_MIN, jnp.int32))
    _, prefix, done, thr = lax.while_loop(bit_cond, bit_body, init)
    is_done = done > 0.5
    thr = jnp.where(is_done, thr, prefix)

    def tie_search(_):
        n_gt = _count_tiles(n_cnt, lambda st: key_ref[:, pl.ds(st, tcnt)] > thr, tq, tcnt)
        need = kf - n_gt
        n_idx_bits = max(1, int(math.ceil(math.log2(s_len))))

        def idx_body(b, p):
            cand = p | lax.shift_left(jnp.int32(1), jnp.int32(n_idx_bits - 1) - b)
            below = _count_tiles(
                n_cnt, lambda st: (key_ref[:, pl.ds(st, tcnt)] == thr) & (col_cnt + st < cand), tq, tcnt)
            return jnp.where(below <= need - 1.0, cand, p)

        return lax.fori_loop(0, n_idx_bits, idx_body, jnp.zeros((tq, 1), jnp.int32))

    last_tie = lax.cond(jnp.min(done) < 0.5, tie_search,
                        lambda _: jnp.full((tq, 1), -1, jnp.int32), 0)
    last_tie = jnp.where(is_done, -1, last_tie)

    def write_body(t, carry):
        start = pl.multiple_of(t * tcnt, tcnt)
        kt = key_ref[:, pl.ds(start, tcnt)]
        col = col_cnt + start
        sel = ((kt > thr) | ((kt == thr) & (col <= last_tie))) & (col <= rowpos)
        bias_ref[:, pl.ds(start, tcnt)] = jnp.where(sel, 0.0, MASKED).astype(bias_ref.dtype)
        return carry

    lax.fori_loop(0, n_cnt, write_body, 0)

    def fill_body(t, carry):
        bias_ref[:, pl.ds(pl.multiple_of(t * tcnt, tcnt), tcnt)] = jnp.full(
            (tq, tcnt), MASKED, bias_ref.dtype)
        return carry

    lax.fori_loop(n_cnt, s_len // tcnt, fill_body, 0)


def _dsa_select(hm, kiwi, tq, tsc, tcnt):
    b, s, _ = hm.shape
    tq, tsc, tcnt = min(tq, s), min(tsc, s), min(tcnt, s)
    assert tcnt % tsc == 0 and s % tcnt == 0 and s % tq == 0
    top_k = min(TOPK_MAX, s // 4)
    qi_half = IDX_HEADS * IDX_DIM // 2
    qicol = GRP_IN // qi_half
    return pl.pallas_call(
        functools.partial(_dsa_select_kernel, top_k=top_k, tq=tq, tsc=tsc, tcnt=tcnt),
        grid=(b, s // tq),
        in_specs=[pl.BlockSpec((None, tq, qi_half), lambda bb, i: (bb, i, qicol)),
                  pl.BlockSpec((None, tq, qi_half), lambda bb, i: (bb, i, qicol + 1)),
                  pl.BlockSpec((None, s, LANES), lambda bb, i: (bb, 0, 0)),
                  pl.BlockSpec((None, tq, LANES), lambda bb, i: (bb, i, 0))],
        out_specs=pl.BlockSpec((None, tq, s), lambda bb, i: (bb, i, 0)),
        out_shape=jax.ShapeDtypeStruct((b, s, s), jnp.bfloat16),
        scratch_shapes=[pltpu.VMEM((tq, s), jnp.int32)],
        compiler_params=_params("parallel", "parallel"),
        name="dsa_select",
    )(hm, hm, kiwi, kiwi)


def _dsa_attn_kernel(q_ref, k_ref, v_ref, bias_ref, sl_ref, o_ref,
                     kaug_ref, vaug_ref, m_ref, acc_ref, *, tk, chunk):
    i = pl.program_id(1)
    s_len = k_ref.shape[0]

    @pl.when(i == 0)
    def _():
        lane = lax.broadcasted_iota(jnp.int32, (chunk, HEAD_DIM), 1)
        ones_col = (lane == 0).astype(jnp.bfloat16)
        for c in range(s_len // chunk):
            pos = lax.broadcasted_iota(jnp.int32, (chunk, HEAD_DIM), 0) + c * chunk
            hi = pos - (pos & (POS_SPLIT - 1))
            lo = pos & (POS_SPLIT - 1)
            pcols = jnp.where(lane < N_SLOPE_PIECES, hi, jnp.where(lane < 2 * N_SLOPE_PIECES, lo, 0))
            pcols = pcols.astype(jnp.float32).astype(jnp.bfloat16)
            rows = pl.ds(c * chunk, chunk)
            for kv in range(N_KV):
                kaug_ref[kv, rows, :] = jnp.concatenate(
                    [k_ref[rows, kv * HEAD_DIM:(kv + 1) * HEAD_DIM], pcols], axis=1)
                vaug_ref[kv, rows, :] = jnp.concatenate(
                    [v_ref[rows, kv * HEAD_DIM:(kv + 1) * HEAD_DIM], ones_col], axis=1)

    n_tiles = (i * BLOCK + BLOCK + tk - 1) // tk
    m_ref[...] = jnp.full(m_ref.shape, MASKED, jnp.float32)
    acc_ref[...] = jnp.zeros(acc_ref.shape, jnp.float32)
    qa = [jnp.concatenate(
        [jnp.concatenate(
            [q_ref[:, (kv * GQA_GROUP + g) * HEAD_DIM:(kv * GQA_GROUP + g + 1) * HEAD_DIM]
             for g in range(GQA_GROUP)], axis=0), sl_ref[kv][:, HEAD_DIM:]], axis=1)
        for kv in range(N_KV)]
    n_rep = tk // LANES

    def attn_body(t, carry):
        start = pl.multiple_of(t * tk, tk)
        bias = bias_ref[:, pl.ds(start, tk)].astype(jnp.float32)
        scores = [lax.dot_general(qa[kv], kaug_ref[kv, pl.ds(start, tk), :], (((1,), (1,)), ((), ())),
                                  preferred_element_type=jnp.float32) for kv in range(N_KV)]
        for kv in range(N_KV):
            s_all = scores[kv]
            m_old = m_ref[kv]
            ps, m_news = [], []
            for g in range(GQA_GROUP):
                s = s_all[g * BLOCK:(g + 1) * BLOCK] + bias
                m_new = jnp.maximum(m_old[g * BLOCK:(g + 1) * BLOCK],
                                    jnp.max(s, axis=-1, keepdims=True))
                ps.append(jnp.exp2(s - jnp.concatenate([m_new] * n_rep, axis=1)).astype(jnp.bfloat16))
                m_news.append(m_new)
            m_new = jnp.concatenate(m_news, axis=0)
            pv = jnp.dot(jnp.concatenate(ps, axis=0), vaug_ref[kv, pl.ds(start, tk), :],
                         preferred_element_type=jnp.float32)
            acc_ref[kv] = jnp.exp2(m_old - m_new) * acc_ref[kv] + pv
            m_ref[kv] = m_new
        return carry

    lax.fori_loop(0, n_tiles, attn_body, 0)

    for kv in range(N_KV):
        acc = acc_ref[kv]
        o_all = (acc[:, :HEAD_DIM] / acc[:, HEAD_DIM:HEAD_DIM + 1]).astype(o_ref.dtype)
        for g in range(0, GQA_GROUP, 2):
            h = kv * GQA_GROUP + g
            o_ref[:, h * HEAD_DIM:(h + 2) * HEAD_DIM] = jnp.concatenate(
                [o_all[g * BLOCK:(g + 1) * BLOCK], o_all[(g + 1) * BLOCK:(g + 2) * BLOCK]], axis=1)


def _dsa_attention(hm, bias, tk):
    b, s, _ = hm.shape
    tk = min(tk, s)
    chunk = min(512, s)
    kcol = Q_DIM // KV_DIM
    rows = GQA_GROUP * BLOCK
    return pl.pallas_call(
        functools.partial(_dsa_attn_kernel, tk=tk, chunk=chunk),
        grid=(b, s // BLOCK),
        in_specs=[pl.BlockSpec((None, BLOCK, Q_DIM), lambda bb, i: (bb, i, 0)),
                  pl.BlockSpec((None, s, KV_DIM), lambda bb, i: (bb, 0, kcol)),
                  pl.BlockSpec((None, s, KV_DIM), lambda bb, i: (bb, 0, kcol + 1)),
                  pl.BlockSpec((None, BLOCK, s), lambda bb, i: (bb, i, 0)),
                  pl.BlockSpec((N_KV, rows, LANES), lambda bb, i: (0, 0, 0))],
        out_specs=pl.BlockSpec((None, BLOCK, Q_DIM), lambda bb, i: (bb, i, 0)),
        out_shape=jax.ShapeDtypeStruct((b, s, Q_DIM), jnp.bfloat16),
        scratch_shapes=[pltpu.VMEM((N_KV, s, LANES), jnp.bfloat16),
                        pltpu.VMEM((N_KV, s, LANES), jnp.bfloat16),
                        pltpu.VMEM((N_KV, rows, LANES), jnp.float32),
                        pltpu.VMEM((N_KV, rows, LANES), jnp.float32)],
        compiler_params=_params("parallel", "arbitrary"),
        name="dsa_attention",
    )(hm, hm, hm, bias, _slope_columns(1, 2))


_QKV_SCALE = np.concatenate([np.full(Q_DIM, ATTN_SCALE * LOG2E), np.ones(2 * KV_DIM)])
_DSA_SCALE = np.concatenate([_QKV_SCALE, np.full(IDX_HEADS * IDX_DIM, IDX_DIM ** -0.5)])


def _swa_layer(xb, w_in, layer, sinks, b, s):
    h = _matmul(xb, w_in, layer, 0, _QKV_SCALE, jnp.bfloat16, 1024, 1280)
    o = _band_attention(h.reshape(b, s, GRP_IN), SWA_WINDOW - 1, 1, sinks)
    return o.reshape(b * s, Q_DIM)


def _dilated_layer(xb, w_in, layer, b, s):
    outs, lses, dils = [], [], []
    for g, (window, dil) in enumerate(DIL_PATTERNS):
        h = _matmul(xb, w_in, layer, g * GRP_IN, _QKV_SCALE, jnp.bfloat16, 1024, 1280, batch=b, dil=dil)
        o, lse = _band_attention(h.reshape(b * dil, s // dil, GRP_IN), window // dil, dil)
        outs.append(o.reshape(b, dil, s // dil, Q_DIM))
        lses.append(lse.reshape(b, dil, s // dil, Q_DIM))
        dils.append(dil)
    return _merge_groups(outs, lses, dils, 256)


def _dsa_layer(xb, w_in, layer, b, s):
    hm = _matmul(xb, w_in, layer, 0, _DSA_SCALE, jnp.bfloat16, 2048, 512).reshape(b, s, C_MAIN)
    w_small = jnp.concatenate([w_in[layer, :, C_MAIN:C_MAIN + IDX_DIM],
                               w_in[layer, :, C_MAIN + IDX_DIM:] * IDX_HEADS ** -0.5,
                               jnp.zeros((D_MODEL, LANES - IDX_DIM - IDX_HEADS), w_in.dtype)], axis=1)
    kiwi = _matmul(xb, w_small[None], 0, 0, np.ones(LANES), jnp.float32, 1024, LANES).reshape(b, s, LANES)
    bias = _dsa_select(hm, kiwi, 256, 512, 512)
    o = _dsa_attention(hm, bias, 512)
    return o.reshape(b * s, Q_DIM)


def kernel(x, a_w_in, a_sinks, a_w_out, b_w_in, b_w_out, c_w_in, c_w_out, ln_g, ln_b,
           ffn_w_gate_up, ffn_w_down):
    b, s, d = x.shape
    xf = x.reshape(b * s, d)
    xb = xf
    for i in range(DEPTH):
        kind, j = i % N_MIXERS, i // N_MIXERS
        if kind == 0:
            o = _swa_layer(xb, a_w_in, j, a_sinks[j], b, s)
            w_out = a_w_out
        elif kind == 1:
            o = _dilated_layer(xb, b_w_in, j, b, s)
            w_out = b_w_out
        else:
            o = _dsa_layer(xb, c_w_in, j, b, s)
            w_out = c_w_out
        xf, xb = _proj_residual_ln(o, w_out, j, xf, ln_g[i, 0], ln_b[i, 0], 512, 512)
        hmid = _ffn_gate_up(xb, ffn_w_gate_up, i, 1024, 512)
        xf, xb = _proj_residual_ln(hmid, ffn_w_down, i, xf, ln_g[i, 1], ln_b[i, 1], 256, 512)
    return xf.reshape(b, s, d)
```

```python
import functools
import math

import numpy as np
import jax
import jax.numpy as jnp
from jax import lax
from jax.experimental import pallas as pl
from jax.experimental.pallas import tpu as pltpu

D_MODEL = 2048
DEPTH = 4
N_MIXERS = 3
HEAD_DIM = 64
N_HEADS = D_MODEL // HEAD_DIM
N_KV = N_HEADS // 8
GQA_GROUP = N_HEADS // N_KV
Q_DIM = N_HEADS * HEAD_DIM
KV_DIM = N_KV * HEAD_DIM
ATTN_SCALE = HEAD_DIM ** -0.5
BLOCK = 128
SWA_WINDOW = 128
DIL_PATTERNS = ((128, 1), (512, 4), (2048, 16))
N_DIL = len(DIL_PATTERNS)
IDX_HEADS = 16
IDX_DIM = 64
TOPK_MAX = 256
D_FF = 256 * math.ceil(8 * D_MODEL / (3 * 256))
DEEPNORM_ALPHA = (2 * DEPTH) ** 0.25
LN_EPS = 1e-5
GRP_IN = Q_DIM + 2 * KV_DIM
C_MAIN = Q_DIM + 2 * KV_DIM + IDX_HEADS * IDX_DIM
C_IN = C_MAIN + IDX_DIM + IDX_HEADS
LANES = 128
MASKED = -1e30
INT_MIN = -2 ** 31
LOG2E = 1.4426950408889634
VMEM_LIMIT = 56 * 1024 * 1024

_SLOPES = [float(np.float32(2.0 ** (-8.0 * (i + 1) / N_HEADS))) for i in range(N_HEADS)]
N_SLOPE_PIECES = 3
POS_SPLIT = 64
BAND_Q_BLOCKS = 1
BAND_AHEAD = 4


def _params(*sem):
    return pltpu.CompilerParams(dimension_semantics=sem, vmem_limit_bytes=VMEM_LIMIT)


def _mm_kernel(x_ref, w_ref, sc_ref, o_ref, wb_ref, *acc, dil):
    @pl.when(pl.program_id(1) == 0)
    def _():
        wb_ref[...] = (w_ref[...] * sc_ref[...]).astype(jnp.bfloat16)

    y = jnp.dot(x_ref[...].astype(jnp.bfloat16), wb_ref[...], preferred_element_type=jnp.float32)
    if dil == 1:
        o_ref[...] = y.astype(o_ref.dtype)
    else:
        acc_ref, = acc
        per = acc_ref.shape[1] // dil
        for c in range(acc_ref.shape[0]):
            cols = slice(c * LANES, (c + 1) * LANES)
            acc_ref[c] = y[:, cols]
            for r in range(dil):
                o_ref[r, :, cols] = acc_ref[c, pl.ds(r, per, stride=dil), :].astype(o_ref.dtype)


def _matmul(x, w, layer, col0, scale, out_dtype, tm, tn, batch=1, dil=1):
    m, k = x.shape
    n = scale.shape[0]
    tm, tn = min(tm, m // batch), min(tn, n)
    assert m % (batch * tm) == 0 and n % tn == 0 and col0 % tn == 0 and tm % (16 * dil) == 0
    j0 = col0 // tn
    if dil == 1:
        out_specs = pl.BlockSpec((tm, tn), lambda j, i: (i, j))
        out_shape = jax.ShapeDtypeStruct((m, n), out_dtype)
        scratch = []
    else:
        per_batch = m // batch // tm
        out_specs = pl.BlockSpec((None, dil, tm // dil, tn),
                                 lambda j, i: (i // per_batch, 0, i % per_batch, j))
        out_shape = jax.ShapeDtypeStruct((batch, dil, m // batch // dil, n), out_dtype)
        scratch = [pltpu.VMEM((tn // LANES, tm, LANES), jnp.float32)]
    return pl.pallas_call(
        functools.partial(_mm_kernel, dil=dil),
        grid=(n // tn, m // tm),
        in_specs=[pl.BlockSpec((tm, k), lambda j, i: (i, 0)),
                  pl.BlockSpec((None, k, tn), lambda j, i: (layer, 0, j0 + j)),
                  pl.BlockSpec((1, tn), lambda j, i: (0, j))],
        out_specs=out_specs,
        out_shape=out_shape,
        scratch_shapes=[pltpu.VMEM((k, tn), jnp.bfloat16)] + scratch,
        compiler_params=_params("parallel", "arbitrary"),
        name="proj_matmul",
    )(x, w, jnp.asarray(scale, jnp.float32).reshape(1, n))


def _gate_up_kernel(x_ref, wg_ref, wu_ref, o_ref, wgb_ref, wub_ref):
    @pl.when(pl.program_id(1) == 0)
    def _():
        wgb_ref[...] = wg_ref[...].astype(jnp.bfloat16)
        wub_ref[...] = wu_ref[...].astype(jnp.bfloat16)

    x = x_ref[...]
    g = jnp.dot(x, wgb_ref[...], preferred_element_type=jnp.float32)
    u = jnp.dot(x, wub_ref[...], preferred_element_type=jnp.float32)
    o_ref[...] = (g * (1.0 / (1.0 + jnp.exp(-g))) * u).astype(o_ref.dtype)


def _ffn_gate_up(x, w, layer, tm, tn):
    m, k = x.shape
    d_ff = w.shape[2] // 2
    tm = min(tm, m)
    assert m % tm == 0 and d_ff % tn == 0
    nj = d_ff // tn
    return pl.pallas_call(
        _gate_up_kernel,
        grid=(nj, m // tm),
        in_specs=[pl.BlockSpec((tm, k), lambda j, i: (i, 0)),
                  pl.BlockSpec((None, k, tn), lambda j, i: (layer, 0, j)),
                  pl.BlockSpec((None, k, tn), lambda j, i: (layer, 0, j + nj))],
        out_specs=pl.BlockSpec((tm, tn), lambda j, i: (i, j)),
        out_shape=jax.ShapeDtypeStruct((m, d_ff), jnp.bfloat16),
        scratch_shapes=[pltpu.VMEM((k, tn), jnp.bfloat16), pltpu.VMEM((k, tn), jnp.bfloat16)],
        compiler_params=_params("parallel", "arbitrary"),
        name="ffn_gate_up",
    )(x, w, w)


def _proj_ln_kernel(a_ref, w_ref, r_ref, g_ref, b_ref, o_ref, ob_ref, wb_ref, *, n_w, tkc):
    s = pl.program_id(0)

    @pl.when(s < n_w)
    def _():
        wb_ref[pl.ds(pl.multiple_of(s * tkc, tkc), tkc), :] = w_ref[...].astype(jnp.bfloat16)

    @pl.when(s >= n_w)
    def _():
        z = DEEPNORM_ALPHA * r_ref[...] + jnp.dot(a_ref[...], wb_ref[...],
                                                  preferred_element_type=jnp.float32)
        mu = jnp.mean(z, axis=-1, keepdims=True)
        zc = z - mu
        var = jnp.mean(zc * zc, axis=-1, keepdims=True)
        y = zc * lax.rsqrt(var + LN_EPS) * g_ref[...] + b_ref[...]
        o_ref[...] = y
        ob_ref[...] = y.astype(jnp.bfloat16)


def _proj_residual_ln(a, w, layer, resid, g, b, tm, tkc):
    m, k = a.shape
    n = w.shape[2]
    tm, tkc = min(tm, m), min(tkc, k)
    assert m % tm == 0 and k % tkc == 0
    n_w = k // tkc
    row = lambda s: (jnp.maximum(s - n_w, 0), 0)
    return pl.pallas_call(
        functools.partial(_proj_ln_kernel, n_w=n_w, tkc=tkc),
        grid=(n_w + m // tm,),
        in_specs=[pl.BlockSpec((tm, k), row),
                  pl.BlockSpec((None, tkc, n), lambda s: (layer, jnp.minimum(s, n_w - 1), 0)),
                  pl.BlockSpec((tm, n), row),
                  pl.BlockSpec((1, n), lambda s: (0, 0)),
                  pl.BlockSpec((1, n), lambda s: (0, 0))],
        out_specs=[pl.BlockSpec((tm, n), row), pl.BlockSpec((tm, n), row)],
        out_shape=[jax.ShapeDtypeStruct((m, n), jnp.float32),
                   jax.ShapeDtypeStruct((m, n), jnp.bfloat16)],
        scratch_shapes=[pltpu.VMEM((k, n), jnp.bfloat16)],
        compiler_params=_params("arbitrary"),
        name="proj_residual_ln",
    )(a, w, resid, g.reshape(1, n), b.reshape(1, n))


def _bf16_pieces(x, n):
    rest = np.asarray(x, np.float64)
    pieces = []
    for _ in range(n):
        p = rest.astype(np.float32).astype(jnp.bfloat16).astype(np.float64)
        pieces.append(p)
        rest = rest - p
    return pieces


def _slope_columns(mult, n_parts):
    out = np.zeros((N_KV, GQA_GROUP * BLOCK, LANES), np.float32)
    for h in range(N_HEADS):
        pieces = [float(p) for p in _bf16_pieces(np.float64(_SLOPES[h]) * mult * LOG2E, N_SLOPE_PIECES)]
        kv, g = divmod(h, GQA_GROUP)
        out[kv, g * BLOCK:(g + 1) * BLOCK, HEAD_DIM:HEAD_DIM + n_parts * N_SLOPE_PIECES] = np.asarray(
            pieces * n_parts, np.float32)
    return jnp.asarray(out, jnp.bfloat16)


def _band_q_columns(dilation):
    out = np.zeros((N_KV, GQA_GROUP * BLOCK, LANES), np.float32)
    qpos = BLOCK + np.arange(BLOCK, dtype=np.float64)
    for h in range(N_HEADS):
        s2 = np.float64(_SLOPES[h]) * dilation * LOG2E
        kv, g = divmod(h, GQA_GROUP)
        rows = slice(g * BLOCK, (g + 1) * BLOCK)
        for c, p in enumerate(_bf16_pieces(s2, N_SLOPE_PIECES)):
            out[kv, rows, HEAD_DIM + c] = p
        for c, p in enumerate(_bf16_pieces(-s2 * qpos, N_SLOPE_PIECES)):
            out[kv, rows, HEAD_DIM + N_SLOPE_PIECES + c] = p
    return jnp.asarray(out, jnp.bfloat16)


def _band_k_columns():
    out = np.zeros((2 * BLOCK, LANES), np.float32)
    out[:, HEAD_DIM:HEAD_DIM + N_SLOPE_PIECES] = np.arange(2 * BLOCK, dtype=np.float32)[:, None]
    out[:, HEAD_DIM + N_SLOPE_PIECES:HEAD_DIM + 2 * N_SLOPE_PIECES] = 1.0
    return jnp.asarray(out, jnp.bfloat16)


def _band_kernel(*refs, max_j, has_sink, ahead):
    if has_sink:
        sink_ref, q_ref, kp_ref, kc_ref, vp_ref, vc_ref, qc_ref, kc_cols_ref, o_ref = refs
    else:
        q_ref, kp_ref, kc_ref, vp_ref, vc_ref, qc_ref, kc_cols_ref, o_ref, lse_ref = refs
    i = pl.program_id(1)
    n_qb = q_ref.shape[0] // BLOCK
    k = jnp.concatenate([kp_ref[...], kc_ref[...]], axis=0)
    v = jnp.concatenate([vp_ref[...], vc_ref[...]], axis=0)
    qpos = lax.broadcasted_iota(jnp.int32, (BLOCK, 2 * BLOCK), 0) + BLOCK
    kpos = lax.broadcasted_iota(jnp.int32, (BLOCK, 2 * BLOCK), 1)
    j = qpos - kpos
    in_band = (j >= 0) & (j <= max_j)
    bias_rest = jnp.where(in_band, 0.0, MASKED)
    bias_first = jnp.where(in_band & ((kpos >= BLOCK) | (i > 0)), 0.0, MASKED)
    low = lax.broadcasted_iota(jnp.int32, (BLOCK, LANES), 1) < HEAD_DIM
    k_cols = kc_cols_ref[:, HEAD_DIM:]
    ones64 = jnp.ones((2 * BLOCK, HEAD_DIM), jnp.bfloat16)
    ones128 = jnp.ones((2 * BLOCK, LANES), jnp.bfloat16)
    units = [(b, kv) for b in range(n_qb) for kv in range(N_KV)]

    def scores(b, kv):
        ka = jnp.concatenate([k[b * BLOCK:(b + 2) * BLOCK, kv * HEAD_DIM:(kv + 1) * HEAD_DIM], k_cols],
                             axis=1)
        qa = jnp.concatenate(
            [jnp.concatenate([q_ref[b * BLOCK:(b + 1) * BLOCK,
                                    (kv * GQA_GROUP + g) * HEAD_DIM:(kv * GQA_GROUP + g + 1) * HEAD_DIM]
                              for g in range(GQA_GROUP)], axis=0),
             qc_ref[kv][:, HEAD_DIM:]], axis=1)
        return lax.dot_general(qa, ka, (((1,), (1,)), ((), ())),
                               preferred_element_type=jnp.float32)

    pending = [scores(*units[u]) for u in range(min(ahead, len(units)))]
    for u, (b, kv) in enumerate(units):
        s_all = pending.pop(0)
        if u + ahead < len(units):
            pending.append(scores(*units[u + ahead]))
        h0 = kv * GQA_GROUP
        bias = bias_first if b == 0 else bias_rest
        out_rows = slice(b * BLOCK, (b + 1) * BLOCK)
        v_kv = v[b * BLOCK:(b + 2) * BLOCK, kv * HEAD_DIM:(kv + 1) * HEAD_DIM]
        v_even = jnp.concatenate([v_kv, ones64, ones128], axis=1)
        v_odd = jnp.concatenate([ones64, v_kv, ones128], axis=1)
        ps, ms = [], []
        for g in range(GQA_GROUP):
            s = s_all[g * BLOCK:(g + 1) * BLOCK] + bias
            m = jnp.max(s, axis=-1, keepdims=True)
            if has_sink:
                m = jnp.maximum(m, sink_ref[h0 + g] * LOG2E)
            ps.append(jnp.exp2(s - m).astype(jnp.bfloat16))
            ms.append(m)
        acc_e = jnp.dot(jnp.concatenate(ps[0::2], axis=0), v_even,
                        preferred_element_type=jnp.float32)
        acc_o = jnp.dot(jnp.concatenate(ps[1::2], axis=0), v_odd,
                        preferred_element_type=jnp.float32)
        for t in range(GQA_GROUP // 2):
            h = h0 + 2 * t
            rows = slice(t * BLOCK, (t + 1) * BLOCK)
            num = jnp.where(low, acc_e[rows, :LANES], acc_o[rows, :LANES])
            den = jnp.where(low, acc_e[rows, LANES:], acc_o[rows, LANES:])
            m2 = jnp.where(low, jnp.broadcast_to(ms[2 * t], (BLOCK, LANES)),
                           jnp.broadcast_to(ms[2 * t + 1], (BLOCK, LANES)))
            if has_sink:
                sink2 = jnp.where(low, sink_ref[h] * LOG2E, sink_ref[h + 1] * LOG2E)
                den = den + jnp.exp2(sink2 - m2)
            cols = slice(h * HEAD_DIM, (h + 2) * HEAD_DIM)
            o_ref[out_rows, cols] = (num / den).astype(o_ref.dtype)
            if not has_sink:
                lse_ref[out_rows, cols] = m2 * (1.0 / LOG2E) + jnp.log(den)


def _band_attention(h, max_j, dilation, sinks=None):
    r, l, _ = h.shape
    n_qb = min(BAND_Q_BLOCKS, l // BLOCK)
    tq = n_qb * BLOCK
    assert l % tq == 0
    kcol, vcol = Q_DIM // KV_DIM, Q_DIM // KV_DIM + 1
    has_sink = sinks is not None
    rows = GQA_GROUP * BLOCK
    prev = lambda s, i, col: (s, jnp.maximum(i * n_qb - 1, 0), col)
    in_specs = [pl.BlockSpec((None, tq, Q_DIM), lambda s, i: (s, i, 0)),
                pl.BlockSpec((None, BLOCK, KV_DIM), lambda s, i: prev(s, i, kcol)),
                pl.BlockSpec((None, tq, KV_DIM), lambda s, i: (s, i, kcol)),
                pl.BlockSpec((None, BLOCK, KV_DIM), lambda s, i: prev(s, i, vcol)),
                pl.BlockSpec((None, tq, KV_DIM), lambda s, i: (s, i, vcol)),
                pl.BlockSpec((N_KV, rows, LANES), lambda s, i: (0, 0, 0)),
                pl.BlockSpec((2 * BLOCK, LANES), lambda s, i: (0, 0))]
    o_spec = pl.BlockSpec((None, tq, Q_DIM), lambda s, i: (s, i, 0))
    args = [h, h, h, h, h, _band_q_columns(dilation), _band_k_columns()]
    if has_sink:
        in_specs = [pl.BlockSpec(memory_space=pltpu.SMEM)] + in_specs
        args = [sinks.astype(jnp.float32)] + args
        out_specs = o_spec
        out_shape = jax.ShapeDtypeStruct((r, l, Q_DIM), jnp.bfloat16)
    else:
        out_specs = [o_spec, o_spec]
        out_shape = [jax.ShapeDtypeStruct((r, l, Q_DIM), jnp.bfloat16),
                     jax.ShapeDtypeStruct((r, l, Q_DIM), jnp.float32)]
    return pl.pallas_call(
        functools.partial(_band_kernel, max_j=max_j, has_sink=has_sink, ahead=BAND_AHEAD),
        grid=(r, l // tq),
        in_specs=in_specs,
        out_specs=out_specs,
        out_shape=out_shape,
        compiler_params=_params("parallel", "parallel"),
        name="band_attention",
    )(*args)


def _merge_kernel(*refs, dils):
    n_g = len(dils)
    o_refs, l_refs, out_ref, bufs = refs[:n_g], refs[n_g:2 * n_g], refs[2 * n_g], list(refs[2 * n_g + 1:])

    def natural(ref, d):
        if d == 1:
            return ref[0].astype(jnp.float32)
        buf = bufs.pop()
        per = ref.shape[1]
        for c in range(buf.shape[0]):
            for r in range(d):
                buf[c, pl.ds(r, per, stride=d), :] = ref[r, :, c * LANES:(c + 1) * LANES].astype(jnp.float32)
        return jnp.concatenate([buf[c] for c in range(buf.shape[0])], axis=1)

    os = [natural(ref, d) for ref, d in zip(o_refs, dils)]
    ls = [natural(ref, d) for ref, d in zip(l_refs, dils)]
    m = functools.reduce(jnp.maximum, ls)
    es = [jnp.exp(l - m) for l in ls]
    num = sum(e * o for e, o in zip(es, os))
    out_ref[...] = (num / sum(es)).astype(out_ref.dtype)


def _merge_groups(outs, lses, dils, tm):
    b, _, s, n = outs[0].shape
    tm = min(tm, s)
    assert s % tm == 0 and all(tm % (8 * d) == 0 for d in dils)
    per_batch = s // tm
    specs = [pl.BlockSpec((None, d, tm // d, n), lambda i: (i // per_batch, 0, i % per_batch, 0))
             for d in dils]
    n_buf = 2 * sum(d > 1 for d in dils)
    return pl.pallas_call(
        functools.partial(_merge_kernel, dils=tuple(dils)),
        grid=(b * per_batch,),
        in_specs=specs + specs,
        out_specs=pl.BlockSpec((tm, n), lambda i: (i, 0)),
        out_shape=jax.ShapeDtypeStruct((b * s, n), jnp.bfloat16),
        scratch_shapes=[pltpu.VMEM((n // LANES, tm, LANES), jnp.float32)] * n_buf,
        compiler_params=_params("parallel"),
        name="merge_groups",
    )(*outs, *lses)


def _count_rows(n_tiles, tile_hits, tq, tk):
    rows = 32
    def body(t, acc):
        hit = tile_hits(pl.multiple_of(t * tk, tk)).astype(jnp.float32)
        return acc + hit.reshape(tk // rows, rows, tq).sum(axis=0)
    acc = lax.fori_loop(0, n_tiles, body, jnp.zeros((rows, tq), jnp.float32))
    return jnp.sum(acc, axis=0, keepdims=True)


def _dsa_select_kernel(qia_ref, qib_ref, ki_ref, wi_ref, bias_ref, key_ref, *, top_k, tq, tk):
    i = pl.program_id(1)
    s_len = key_ref.shape[0]
    n_t = ((i + 1) * tq + tk - 1) // tk
    qpos = lax.broadcasted_iota(jnp.int32, (1, tq), 1) + i * tq
    kloc = lax.broadcasted_iota(jnp.int32, (tk, tq), 0)

    half = IDX_HEADS // 2
    q_t = [ref[...].astype(jnp.float32).T.astype(jnp.bfloat16) for ref in (qia_ref, qib_ref)]
    q_cat = [jnp.concatenate([q[h * IDX_DIM:(h + 1) * IDX_DIM, :] for h in range(half)], axis=1)
             for q in q_t]
    w_t = wi_ref[...].T

    def score_body(t, carry):
        start = pl.multiple_of(t * tk, tk)
        ki = ki_ref[pl.ds(start, tk), :][:, :IDX_DIM].astype(jnp.bfloat16)
        rels = [jnp.dot(ki, q, preferred_element_type=jnp.float32) for q in q_cat]
        sc = jnp.zeros((tk, tq), jnp.float32)
        for d, rel in enumerate(rels):
            for g in range(half):
                row = IDX_DIM + d * half + g
                sc = sc + jnp.maximum(rel[:, g * tq:(g + 1) * tq], 0.0) * w_t[row:row + 1, :]
        sc = sc + 0.0
        bits = pltpu.bitcast(sc, jnp.int32)
        keys = jnp.where(bits < 0, bits ^ jnp.int32(0x7FFFFFFF), bits)
        key_ref[pl.ds(start, tk), :] = jnp.where(kloc + start <= qpos, keys, jnp.int32(INT_MIN))
        return carry

    lax.fori_loop(0, n_t, score_body, 0)

    kf = jnp.float32(top_k)

    def count_ge(cand):
        return _count_rows(n_t, lambda st: key_ref[pl.ds(st, tk), :] >= cand, tq, tk)

    short = (qpos + 1 < top_k).astype(jnp.float32)

    def bit_cond(c):
        b, _, done, _ = c
        return (b < 32) & (jnp.min(done) < 0.5)

    def bit_body(c):
        b, prefix, done, thr = c
        cand = prefix ^ lax.shift_left(jnp.int32(1), jnp.int32(31) - b)
        cnt = count_ge(cand)
        prefix = jnp.where(cnt >= kf, cand, prefix)
        hit = (cnt == kf) & (done < 0.5)
        thr = jnp.where(hit, cand - 1, thr)
        done = jnp.where(hit, 1.0, done)
        return b + 1, prefix, done, thr

    init = (jnp.int32(0), jnp.full((1, tq), INT_MIN, jnp.int32), short,
            jnp.full((1, tq), INT_MIN, jnp.int32))
    _, prefix, done, thr = lax.while_loop(bit_cond, bit_body, init)
    is_done = done > 0.5
    thr = jnp.where(is_done, thr, prefix)

    def tie_search(_):
        n_gt = _count_rows(n_t, lambda st: key_ref[pl.ds(st, tk), :] > thr, tq, tk)
        need = kf - n_gt
        n_idx_bits = max(1, int(math.ceil(math.log2(s_len))))

        def idx_body(b, p):
            cand = p | lax.shift_left(jnp.int32(1), jnp.int32(n_idx_bits - 1) - b)
            below = _count_rows(
                n_t, lambda st: (key_ref[pl.ds(st, tk), :] == thr) & (kloc + st < cand), tq, tk)
            return jnp.where(below <= need - 1.0, cand, p)

        return lax.fori_loop(0, n_idx_bits, idx_body, jnp.zeros((1, tq), jnp.int32))

    last_tie = lax.cond(jnp.min(done) < 0.5, tie_search,
                        lambda _: jnp.full((1, tq), -1, jnp.int32), 0)
    last_tie = jnp.where(is_done, -1, last_tie)

    def write_body(t, carry):
        start = pl.multiple_of(t * tk, tk)
        kt = key_ref[pl.ds(start, tk), :]
        pos = kloc + start
        sel = ((kt > thr) | ((kt == thr) & (pos <= last_tie))) & (pos <= qpos)
        bias_ref[:, pl.ds(start, tk)] = jnp.where(sel, 0.0, MASKED).T.astype(bias_ref.dtype)
        return carry

    lax.fori_loop(0, n_t, write_body, 0)

    def fill_body(t, carry):
        bias_ref[:, pl.ds(pl.multiple_of(t * tk, tk), tk)] = jnp.full((tq, tk), MASKED, bias_ref.dtype)
        return carry

    lax.fori_loop(n_t, s_len // tk, fill_body, 0)


def _dsa_select(hm, kiwi, tq, tk):
    b, s, _ = hm.shape
    tq, tk = min(tq, s), min(tk, s)
    assert s % tk == 0 and s % tq == 0
    top_k = min(TOPK_MAX, s // 4)
    qi_half = IDX_HEADS * IDX_DIM // 2
    qicol = GRP_IN // qi_half
    return pl.pallas_call(
        functools.partial(_dsa_select_kernel, top_k=top_k, tq=tq, tk=tk),
        grid=(b, s // tq),
        in_specs=[pl.BlockSpec((None, tq, qi_half), lambda bb, i: (bb, i, qicol)),
                  pl.BlockSpec((None, tq, qi_half), lambda bb, i: (bb, i, qicol + 1)),
                  pl.BlockSpec((None, s, LANES), lambda bb, i: (bb, 0, 0)),
                  pl.BlockSpec((None, tq, LANES), lambda bb, i: (bb, i, 0))],
        out_specs=pl.BlockSpec((None, tq, s), lambda bb, i: (bb, i, 0)),
        out_shape=jax.ShapeDtypeStruct((b, s, s), jnp.bfloat16),
        scratch_shapes=[pltpu.VMEM((s, tq), jnp.int32)],
        compiler_params=_params("parallel", "parallel"),
        name="dsa_select",
    )(hm, hm, kiwi, kiwi)


def _dsa_attn_kernel(q_ref, k_ref, v_ref, bias_ref, sl_ref, o_ref,
                     kaug_ref, vaug_ref, m_ref, acc_ref, *, tk, chunk):
    i = pl.program_id(1)
    s_len = k_ref.shape[0]

    @pl.when(i == 0)
    def _():
        lane = lax.broadcasted_iota(jnp.int32, (chunk, HEAD_DIM), 1)
        ones_col = (lane == 0).astype(jnp.bfloat16)
        for c in range(s_len // chunk):
            pos = lax.broadcasted_iota(jnp.int32, (chunk, HEAD_DIM), 0) + c * chunk
            hi = pos - (pos & (POS_SPLIT - 1))
            lo = pos & (POS_SPLIT - 1)
            pcols = jnp.where(lane < N_SLOPE_PIECES, hi, jnp.where(lane < 2 * N_SLOPE_PIECES, lo, 0))
            pcols = pcols.astype(jnp.float32).astype(jnp.bfloat16)
            rows = pl.ds(c * chunk, chunk)
            for kv in range(N_KV):
                kaug_ref[kv, rows, :] = jnp.concatenate(
                    [k_ref[rows, kv * HEAD_DIM:(kv + 1) * HEAD_DIM], pcols], axis=1)
                vaug_ref[kv, rows, :] = jnp.concatenate(
                    [v_ref[rows, kv * HEAD_DIM:(kv + 1) * HEAD_DIM], ones_col], axis=1)

    n_tiles = (i * BLOCK + BLOCK + tk - 1) // tk
    m_ref[...] = jnp.full(m_ref.shape, MASKED, jnp.float32)
    acc_ref[...] = jnp.zeros(acc_ref.shape, jnp.float32)
    qa = [jnp.concatenate(
        [jnp.concatenate(
            [q_ref[:, (kv * GQA_GROUP + g) * HEAD_DIM:(kv * GQA_GROUP + g + 1) * HEAD_DIM]
             for g in range(GQA_GROUP)], axis=0), sl_ref[kv][:, HEAD_DIM:]], axis=1)
        for kv in range(N_KV)]
    n_rep = tk // LANES

    def attn_body(t, carry):
        start = pl.multiple_of(t * tk, tk)
        bias = bias_ref[:, pl.ds(start, tk)].astype(jnp.float32)
        scores = [lax.dot_general(qa[kv], kaug_ref[kv, pl.ds(start, tk), :], (((1,), (1,)), ((), ())),
                                  preferred_element_type=jnp.float32) for kv in range(N_KV)]
        for kv in range(N_KV):
            s_all = scores[kv]
            m_old = m_ref[kv]
            ps, m_news = [], []
            for g in range(GQA_GROUP):
                s = s_all[g * BLOCK:(g + 1) * BLOCK] + bias
                m_new = jnp.maximum(m_old[g * BLOCK:(g + 1) * BLOCK],
                                    jnp.max(s, axis=-1, keepdims=True))
                ps.append(jnp.exp2(s - jnp.concatenate([m_new] * n_rep, axis=1)).astype(jnp.bfloat16))
                m_news.append(m_new)
            m_new = jnp.concatenate(m_news, axis=0)
            pv = jnp.dot(jnp.concatenate(ps, axis=0), vaug_ref[kv, pl.ds(start, tk), :],
                         preferred_element_type=jnp.float32)
            acc_ref[kv] = jnp.exp2(m_old - m_new) * acc_ref[kv] + pv
            m_ref[kv] = m_new
        return carry

    lax.fori_loop(0, n_tiles, attn_body, 0)

    for kv in range(N_KV):
        acc = acc_ref[kv]
        o_all = (acc[:, :HEAD_DIM] / acc[:, HEAD_DIM:HEAD_DIM + 1]).astype(o_ref.dtype)
        for g in range(0, GQA_GROUP, 2):
            h = kv * GQA_GROUP + g
            o_ref[:, h * HEAD_DIM:(h + 2) * HEAD_DIM] = jnp.concatenate(
                [o_all[g * BLOCK:(g + 1) * BLOCK], o_all[(g + 1) * BLOCK:(g + 2) * BLOCK]], axis=1)


def _dsa_attention(hm, bias, tk):
    b, s, _ = hm.shape
    tk = min(tk, s)
    chunk = min(512, s)
    kcol = Q_DIM // KV_DIM
    rows = GQA_GROUP * BLOCK
    return pl.pallas_call(
        functools.partial(_dsa_attn_kernel, tk=tk, chunk=chunk),
        grid=(b, s // BLOCK),
        in_specs=[pl.BlockSpec((None, BLOCK, Q_DIM), lambda bb, i: (bb, i, 0)),
                  pl.BlockSpec((None, s, KV_DIM), lambda bb, i: (bb, 0, kcol)),
                  pl.BlockSpec((None, s, KV_DIM), lambda bb, i: (bb, 0, kcol + 1)),
                  pl.BlockSpec((None, BLOCK, s), lambda bb, i: (bb, i, 0)),
                  pl.BlockSpec((N_KV, rows, LANES), lambda bb, i: (0, 0, 0))],
        out_specs=pl.BlockSpec((None, BLOCK, Q_DIM), lambda bb, i: (bb, i, 0)),
        out_shape=jax.ShapeDtypeStruct((b, s, Q_DIM), jnp.bfloat16),
        scratch_shapes=[pltpu.VMEM((N_KV, s, LANES), jnp.bfloat16),
                        pltpu.VMEM((N_KV, s, LANES), jnp.bfloat16),
                        pltpu.VMEM((N_KV, rows, LANES), jnp.float32),
                        pltpu.VMEM((N_KV, rows, LANES), jnp.float32)],
        compiler_params=_params("parallel", "arbitrary"),
        name="dsa_attention",
    )(hm, hm, hm, bias, _slope_columns(1, 2))


_QKV_SCALE = np.concatenate([np.full(Q_DIM, ATTN_SCALE * LOG2E), np.ones(2 * KV_DIM)])
_DSA_SCALE = np.concatenate([_QKV_SCALE, np.full(IDX_HEADS * IDX_DIM, IDX_DIM ** -0.5)])


def _swa_layer(xb, w_in, layer, sinks, b, s):
    h = _matmul(xb, w_in, layer, 0, _QKV_SCALE, jnp.bfloat16, 1024, 1280)
    o = _band_attention(h.reshape(b, s, GRP_IN), SWA_WINDOW - 1, 1, sinks)
    return o.reshape(b * s, Q_DIM)


def _dilated_layer(xb, w_in, layer, b, s):
    outs, lses, dils = [], [], []
    for g, (window, dil) in enumerate(DIL_PATTERNS):
        h = _matmul(xb, w_in, layer, g * GRP_IN, _QKV_SCALE, jnp.bfloat16, 1024, 1280, batch=b, dil=dil)
        o, lse = _band_attention(h.reshape(b * dil, s // dil, GRP_IN), window // dil, dil)
        outs.append(o.reshape(b, dil, s // dil, Q_DIM))
        lses.append(lse.reshape(b, dil, s // dil, Q_DIM))
        dils.append(dil)
    return _merge_groups(outs, lses, dils, 256)


def _dsa_layer(xb, w_in, layer, b, s):
    hm = _matmul(xb, w_in, layer, 0, _DSA_SCALE, jnp.bfloat16, 2048, 512).reshape(b, s, C_MAIN)
    w_small = jnp.concatenate([w_in[layer, :, C_MAIN:C_MAIN + IDX_DIM],
                               w_in[layer, :, C_MAIN + IDX_DIM:] * IDX_HEADS ** -0.5,
                               jnp.zeros((D_MODEL, LANES - IDX_DIM - IDX_HEADS), w_in.dtype)], axis=1)
    kiwi = _matmul(xb, w_small[None], 0, 0, np.ones(LANES), jnp.float32, 1024, LANES).reshape(b, s, LANES)
    bias = _dsa_select(hm, kiwi, 256, 512)
    o = _dsa_attention(hm, bias, 512)
    return o.reshape(b * s, Q_DIM)


def kernel(x, a_w_in, a_sinks, a_w_out, b_w_in, b_w_out, c_w_in, c_w_out, ln_g, ln_b,
           ffn_w_gate_up, ffn_w_down):
    b, s, d = x.shape
    xf = x.reshape(b * s, d)
    xb = xf
    for i in range(DEPTH):
        kind, j = i % N_MIXERS, i // N_MIXERS
        if kind == 0:
            o = _swa_layer(xb, a_w_in, j, a_sinks[j], b, s)
            w_out = a_w_out
        elif kind == 1:
            o = _dilated_layer(xb, b_w_in, j, b, s)
            w_out = b_w_out
        else:
            o = _dsa_layer(xb, c_w_in, j, b, s)
            w_out = c_w_out
        xf, xb = _proj_residual_ln(o, w_out, j, xf, ln_g[i, 0], ln_b[i, 0], 512, 512)
        hmid = _ffn_gate_up(xb, ffn_w_gate_up, i, 1024, 512)
        xf, xb = _proj_residual_ln(hmid, ffn_w_down, i, xf, ln_g[i, 1], ln_b[i, 1], 256, 512)
    return xf.reshape(b, s, d)
```

```python
import functools
import math

import numpy as np
import jax
import jax.numpy as jnp
from jax import lax
from jax.experimental import pallas as pl
from jax.experimental.pallas import tpu as pltpu

D_MODEL = 2048
DEPTH = 4
N_MIXERS = 3
HEAD_DIM = 64
N_HEADS = D_MODEL // HEAD_DIM
N_KV = N_HEADS // 8
GQA_GROUP = N_HEADS // N_KV
Q_DIM = N_HEADS * HEAD_DIM
KV_DIM = N_KV * HEAD_DIM
ATTN_SCALE = HEAD_DIM ** -0.5
BLOCK = 128
SWA_WINDOW = 128
DIL_PATTERNS = ((128, 1), (512, 4), (2048, 16))
N_DIL = len(DIL_PATTERNS)
IDX_HEADS = 16
IDX_DIM = 64
TOPK_MAX = 256
D_FF = 256 * math.ceil(8 * D_MODEL / (3 * 256))
DEEPNORM_ALPHA = (2 * DEPTH) ** 0.25
LN_EPS = 1e-5
GRP_IN = Q_DIM + 2 * KV_DIM
C_MAIN = Q_DIM + 2 * KV_DIM + IDX_HEADS * IDX_DIM
C_IN = C_MAIN + IDX_DIM + IDX_HEADS
LANES = 128
MASKED = -1e30
INT_MIN = -2 ** 31
LOG2E = 1.4426950408889634
VMEM_LIMIT = 56 * 1024 * 1024

_SLOPES = [float(np.float32(2.0 ** (-8.0 * (i + 1) / N_HEADS))) for i in range(N_HEADS)]
N_SLOPE_PIECES = 3
POS_SPLIT = 64
BAND_Q_BLOCKS = 1
BAND_AHEAD = 4
SELECT_DOT_COLS = 2048


def _params(*sem):
    return pltpu.CompilerParams(dimension_semantics=sem, vmem_limit_bytes=VMEM_LIMIT)


def _mm_kernel(x_ref, w_ref, sc_ref, o_ref, wb_ref, *acc, dil):
    @pl.when(pl.program_id(1) == 0)
    def _():
        wb_ref[...] = (w_ref[...] * sc_ref[...]).astype(jnp.bfloat16)

    y = jnp.dot(x_ref[...].astype(jnp.bfloat16), wb_ref[...], preferred_element_type=jnp.float32)
    if dil == 1:
        o_ref[...] = y.astype(o_ref.dtype)
    else:
        acc_ref, = acc
        per = acc_ref.shape[1] // dil
        for c in range(acc_ref.shape[0]):
            cols = slice(c * LANES, (c + 1) * LANES)
            acc_ref[c] = y[:, cols]
            for r in range(dil):
                o_ref[r, :, cols] = acc_ref[c, pl.ds(r, per, stride=dil), :].astype(o_ref.dtype)


def _matmul(x, w, layer, col0, scale, out_dtype, tm, tn, batch=1, dil=1):
    m, k = x.shape
    n = scale.shape[0]
    tm, tn = min(tm, m // batch), min(tn, n)
    assert m % (batch * tm) == 0 and n % tn == 0 and col0 % tn == 0 and tm % (16 * dil) == 0
    j0 = col0 // tn
    if dil == 1:
        out_specs = pl.BlockSpec((tm, tn), lambda j, i: (i, j))
        out_shape = jax.ShapeDtypeStruct((m, n), out_dtype)
        scratch = []
    else:
        per_batch = m // batch // tm
        out_specs = pl.BlockSpec((None, dil, tm // dil, tn),
                                 lambda j, i: (i // per_batch, 0, i % per_batch, j))
        out_shape = jax.ShapeDtypeStruct((batch, dil, m // batch // dil, n), out_dtype)
        scratch = [pltpu.VMEM((tn // LANES, tm, LANES), jnp.float32)]
    return pl.pallas_call(
        functools.partial(_mm_kernel, dil=dil),
        grid=(n // tn, m // tm),
        in_specs=[pl.BlockSpec((tm, k), lambda j, i: (i, 0)),
                  pl.BlockSpec((None, k, tn), lambda j, i: (layer, 0, j0 + j)),
                  pl.BlockSpec((1, tn), lambda j, i: (0, j))],
        out_specs=out_specs,
        out_shape=out_shape,
        scratch_shapes=[pltpu.VMEM((k, tn), jnp.bfloat16)] + scratch,
        compiler_params=_params("parallel", "arbitrary"),
        name="proj_matmul",
    )(x, w, jnp.asarray(scale, jnp.float32).reshape(1, n))


def _gate_up_kernel(x_ref, wg_ref, wu_ref, o_ref, wgb_ref, wub_ref):
    @pl.when(pl.program_id(1) == 0)
    def _():
        wgb_ref[...] = wg_ref[...].astype(jnp.bfloat16)
        wub_ref[...] = wu_ref[...].astype(jnp.bfloat16)

    x = x_ref[...]
    g = jnp.dot(x, wgb_ref[...], preferred_element_type=jnp.float32)
    u = jnp.dot(x, wub_ref[...], preferred_element_type=jnp.float32)
    o_ref[...] = (g * (1.0 / (1.0 + jnp.exp(-g))) * u).astype(o_ref.dtype)


def _ffn_gate_up(x, w, layer, tm, tn):
    m, k = x.shape
    d_ff = w.shape[2] // 2
    tm = min(tm, m)
    assert m % tm == 0 and d_ff % tn == 0
    nj = d_ff // tn
    return pl.pallas_call(
        _gate_up_kernel,
        grid=(nj, m // tm),
        in_specs=[pl.BlockSpec((tm, k), lambda j, i: (i, 0)),
                  pl.BlockSpec((None, k, tn), lambda j, i: (layer, 0, j)),
                  pl.BlockSpec((None, k, tn), lambda j, i: (layer, 0, j + nj))],
        out_specs=pl.BlockSpec((tm, tn), lambda j, i: (i, j)),
        out_shape=jax.ShapeDtypeStruct((m, d_ff), jnp.bfloat16),
        scratch_shapes=[pltpu.VMEM((k, tn), jnp.bfloat16), pltpu.VMEM((k, tn), jnp.bfloat16)],
        compiler_params=_params("parallel", "arbitrary"),
        name="ffn_gate_up",
    )(x, w, w)


def _proj_ln_kernel(a_ref, w_ref, r_ref, g_ref, b_ref, o_ref, ob_ref, wb_ref, *, n_w, tkc):
    s = pl.program_id(0)

    @pl.when(s < n_w)
    def _():
        wb_ref[pl.ds(pl.multiple_of(s * tkc, tkc), tkc), :] = w_ref[...].astype(jnp.bfloat16)

    @pl.when(s >= n_w)
    def _():
        z = DEEPNORM_ALPHA * r_ref[...] + jnp.dot(a_ref[...], wb_ref[...],
                                                  preferred_element_type=jnp.float32)
        mu = jnp.mean(z, axis=-1, keepdims=True)
        zc = z - mu
        var = jnp.mean(zc * zc, axis=-1, keepdims=True)
        y = zc * lax.rsqrt(var + LN_EPS) * g_ref[...] + b_ref[...]
        o_ref[...] = y
        ob_ref[...] = y.astype(jnp.bfloat16)


def _proj_residual_ln(a, w, layer, resid, g, b, tm, tkc):
    m, k = a.shape
    n = w.shape[2]
    tm, tkc = min(tm, m), min(tkc, k)
    assert m % tm == 0 and k % tkc == 0
    n_w = k // tkc
    row = lambda s: (jnp.maximum(s - n_w, 0), 0)
    return pl.pallas_call(
        functools.partial(_proj_ln_kernel, n_w=n_w, tkc=tkc),
        grid=(n_w + m // tm,),
        in_specs=[pl.BlockSpec((tm, k), row),
                  pl.BlockSpec((None, tkc, n), lambda s: (layer, jnp.minimum(s, n_w - 1), 0)),
                  pl.BlockSpec((tm, n), row),
                  pl.BlockSpec((1, n), lambda s: (0, 0)),
                  pl.BlockSpec((1, n), lambda s: (0, 0))],
        out_specs=[pl.BlockSpec((tm, n), row), pl.BlockSpec((tm, n), row)],
        out_shape=[jax.ShapeDtypeStruct((m, n), jnp.float32),
                   jax.ShapeDtypeStruct((m, n), jnp.bfloat16)],
        scratch_shapes=[pltpu.VMEM((k, n), jnp.bfloat16)],
        compiler_params=_params("arbitrary"),
        name="proj_residual_ln",
    )(a, w, resid, g.reshape(1, n), b.reshape(1, n))


def _bf16_pieces(x, n):
    rest = np.asarray(x, np.float64)
    pieces = []
    for _ in range(n):
        p = rest.astype(np.float32).astype(jnp.bfloat16).astype(np.float64)
        pieces.append(p)
        rest = rest - p
    return pieces


def _slope_columns(mult, n_parts):
    out = np.zeros((N_KV, GQA_GROUP * BLOCK, LANES), np.float32)
    for h in range(N_HEADS):
        pieces = [float(p) for p in _bf16_pieces(np.float64(_SLOPES[h]) * mult * LOG2E, N_SLOPE_PIECES)]
        kv, g = divmod(h, GQA_GROUP)
        out[kv, g * BLOCK:(g + 1) * BLOCK, HEAD_DIM:HEAD_DIM + n_parts * N_SLOPE_PIECES] = np.asarray(
            pieces * n_parts, np.float32)
    return jnp.asarray(out, jnp.bfloat16)


def _band_q_columns(dilation):
    out = np.zeros((N_KV, GQA_GROUP * BLOCK, LANES), np.float32)
    qpos = BLOCK + np.arange(BLOCK, dtype=np.float64)
    for h in range(N_HEADS):
        s2 = np.float64(_SLOPES[h]) * dilation * LOG2E
        kv, g = divmod(h, GQA_GROUP)
        rows = slice(g * BLOCK, (g + 1) * BLOCK)
        for c, p in enumerate(_bf16_pieces(s2, N_SLOPE_PIECES)):
            out[kv, rows, HEAD_DIM + c] = p
        for c, p in enumerate(_bf16_pieces(-s2 * qpos, N_SLOPE_PIECES)):
            out[kv, rows, HEAD_DIM + N_SLOPE_PIECES + c] = p
    return jnp.asarray(out, jnp.bfloat16)


def _band_k_columns():
    out = np.zeros((2 * BLOCK, LANES), np.float32)
    out[:, HEAD_DIM:HEAD_DIM + N_SLOPE_PIECES] = np.arange(2 * BLOCK, dtype=np.float32)[:, None]
    out[:, HEAD_DIM + N_SLOPE_PIECES:HEAD_DIM + 2 * N_SLOPE_PIECES] = 1.0
    return jnp.asarray(out, jnp.bfloat16)


def _band_kernel(*refs, max_j, has_sink, ahead):
    if has_sink:
        sink_ref, q_ref, kp_ref, kc_ref, vp_ref, vc_ref, qc_ref, kc_cols_ref, o_ref = refs
    else:
        q_ref, kp_ref, kc_ref, vp_ref, vc_ref, qc_ref, kc_cols_ref, o_ref, lse_ref = refs
    i = pl.program_id(1)
    n_qb = q_ref.shape[0] // BLOCK
    k = jnp.concatenate([kp_ref[...], kc_ref[...]], axis=0)
    v = jnp.concatenate([vp_ref[...], vc_ref[...]], axis=0)
    qpos = lax.broadcasted_iota(jnp.int32, (BLOCK, 2 * BLOCK), 0) + BLOCK
    kpos = lax.broadcasted_iota(jnp.int32, (BLOCK, 2 * BLOCK), 1)
    j = qpos - kpos
    in_band = (j >= 0) & (j <= max_j)
    bias_rest = jnp.where(in_band, 0.0, MASKED)
    bias_first = jnp.where(in_band & ((kpos >= BLOCK) | (i > 0)), 0.0, MASKED)
    low = lax.broadcasted_iota(jnp.int32, (BLOCK, LANES), 1) < HEAD_DIM
    k_cols = kc_cols_ref[:, HEAD_DIM:]
    ones64 = jnp.ones((2 * BLOCK, HEAD_DIM), jnp.bfloat16)
    ones128 = jnp.ones((2 * BLOCK, LANES), jnp.bfloat16)
    units = [(b, kv) for b in range(n_qb) for kv in range(N_KV)]

    def scores(b, kv):
        ka = jnp.concatenate([k[b * BLOCK:(b + 2) * BLOCK, kv * HEAD_DIM:(kv + 1) * HEAD_DIM], k_cols],
                             axis=1)
        qa = jnp.concatenate(
            [jnp.concatenate([q_ref[b * BLOCK:(b + 1) * BLOCK,
                                    (kv * GQA_GROUP + g) * HEAD_DIM:(kv * GQA_GROUP + g + 1) * HEAD_DIM]
                              for g in range(GQA_GROUP)], axis=0),
             qc_ref[kv][:, HEAD_DIM:]], axis=1)
        return lax.dot_general(qa, ka, (((1,), (1,)), ((), ())),
                               preferred_element_type=jnp.float32)

    pending = [scores(*units[u]) for u in range(min(ahead, len(units)))]
    for u, (b, kv) in enumerate(units):
        s_all = pending.pop(0)
        if u + ahead < len(units):
            pending.append(scores(*units[u + ahead]))
        h0 = kv * GQA_GROUP
        bias = bias_first if b == 0 else bias_rest
        out_rows = slice(b * BLOCK, (b + 1) * BLOCK)
        v_kv = v[b * BLOCK:(b + 2) * BLOCK, kv * HEAD_DIM:(kv + 1) * HEAD_DIM]
        v_even = jnp.concatenate([v_kv, ones64, ones128], axis=1)
        v_odd = jnp.concatenate([ones64, v_kv, ones128], axis=1)
        ps, ms = [], []
        for g in range(GQA_GROUP):
            s = s_all[g * BLOCK:(g + 1) * BLOCK] + bias
            m = jnp.max(s, axis=-1, keepdims=True)
            if has_sink:
                m = jnp.maximum(m, sink_ref[h0 + g] * LOG2E)
            ps.append(jnp.exp2(s - m).astype(jnp.bfloat16))
            ms.append(m)
        acc_e = jnp.dot(jnp.concatenate(ps[0::2], axis=0), v_even,
                        preferred_element_type=jnp.float32)
        acc_o = jnp.dot(jnp.concatenate(ps[1::2], axis=0), v_odd,
                        preferred_element_type=jnp.float32)
        for t in range(GQA_GROUP // 2):
            h = h0 + 2 * t
            rows = slice(t * BLOCK, (t + 1) * BLOCK)
            num = jnp.where(low, acc_e[rows, :LANES], acc_o[rows, :LANES])
            den = jnp.where(low, acc_e[rows, LANES:], acc_o[rows, LANES:])
            m2 = jnp.where(low, jnp.broadcast_to(ms[2 * t], (BLOCK, LANES)),
                           jnp.broadcast_to(ms[2 * t + 1], (BLOCK, LANES)))
            if has_sink:
                sink2 = jnp.where(low, sink_ref[h] * LOG2E, sink_ref[h + 1] * LOG2E)
                den = den + jnp.exp2(sink2 - m2)
            cols = slice(h * HEAD_DIM, (h + 2) * HEAD_DIM)
            o_ref[out_rows, cols] = (num / den).astype(o_ref.dtype)
            if not has_sink:
                lse_ref[out_rows, cols] = m2 * (1.0 / LOG2E) + jnp.log(den)


def _band_attention(h, max_j, dilation, sinks=None):
    r, l, _ = h.shape
    n_qb = min(BAND_Q_BLOCKS, l // BLOCK)
    tq = n_qb * BLOCK
    assert l % tq == 0
    kcol, vcol = Q_DIM // KV_DIM, Q_DIM // KV_DIM + 1
    has_sink = sinks is not None
    rows = GQA_GROUP * BLOCK
    prev = lambda s, i, col: (s, jnp.maximum(i * n_qb - 1, 0), col)
    in_specs = [pl.BlockSpec((None, tq, Q_DIM), lambda s, i: (s, i, 0)),
                pl.BlockSpec((None, BLOCK, KV_DIM), lambda s, i: prev(s, i, kcol)),
                pl.BlockSpec((None, tq, KV_DIM), lambda s, i: (s, i, kcol)),
                pl.BlockSpec((None, BLOCK, KV_DIM), lambda s, i: prev(s, i, vcol)),
                pl.BlockSpec((None, tq, KV_DIM), lambda s, i: (s, i, vcol)),
                pl.BlockSpec((N_KV, rows, LANES), lambda s, i: (0, 0, 0)),
                pl.BlockSpec((2 * BLOCK, LANES), lambda s, i: (0, 0))]
    o_spec = pl.BlockSpec((None, tq, Q_DIM), lambda s, i: (s, i, 0))
    args = [h, h, h, h, h, _band_q_columns(dilation), _band_k_columns()]
    if has_sink:
        in_specs = [pl.BlockSpec(memory_space=pltpu.SMEM)] + in_specs
        args = [sinks.astype(jnp.float32)] + args
        out_specs = o_spec
        out_shape = jax.ShapeDtypeStruct((r, l, Q_DIM), jnp.bfloat16)
    else:
        out_specs = [o_spec, o_spec]
        out_shape = [jax.ShapeDtypeStruct((r, l, Q_DIM), jnp.bfloat16),
                     jax.ShapeDtypeStruct((r, l, Q_DIM), jnp.float32)]
    return pl.pallas_call(
        functools.partial(_band_kernel, max_j=max_j, has_sink=has_sink, ahead=BAND_AHEAD),
        grid=(r, l // tq),
        in_specs=in_specs,
        out_specs=out_specs,
        out_shape=out_shape,
        compiler_params=_params("parallel", "parallel"),
        name="band_attention",
    )(*args)


def _merge_kernel(*refs, dils):
    n_g = len(dils)
    o_refs, l_refs, out_ref, bufs = refs[:n_g], refs[n_g:2 * n_g], refs[2 * n_g], list(refs[2 * n_g + 1:])

    def natural(ref, d):
        if d == 1:
            return ref[0].astype(jnp.float32)
        buf = bufs.pop()
        per = ref.shape[1]
        for c in range(buf.shape[0]):
            for r in range(d):
                buf[c, pl.ds(r, per, stride=d), :] = ref[r, :, c * LANES:(c + 1) * LANES].astype(jnp.float32)
        return jnp.concatenate([buf[c] for c in range(buf.shape[0])], axis=1)

    os = [natural(ref, d) for ref, d in zip(o_refs, dils)]
    ls = [natural(ref, d) for ref, d in zip(l_refs, dils)]
    m = functools.reduce(jnp.maximum, ls)
    es = [jnp.exp(l - m) for l in ls]
    num = sum(e * o for e, o in zip(es, os))
    out_ref[...] = (num / sum(es)).astype(out_ref.dtype)


def _merge_groups(outs, lses, dils, tm):
    b, _, s, n = outs[0].shape
    tm = min(tm, s)
    assert s % tm == 0 and all(tm % (8 * d) == 0 for d in dils)
    per_batch = s // tm
    specs = [pl.BlockSpec((None, d, tm // d, n), lambda i: (i // per_batch, 0, i % per_batch, 0))
             for d in dils]
    n_buf = 2 * sum(d > 1 for d in dils)
    return pl.pallas_call(
        functools.partial(_merge_kernel, dils=tuple(dils)),
        grid=(b * per_batch,),
        in_specs=specs + specs,
        out_specs=pl.BlockSpec((tm, n), lambda i: (i, 0)),
        out_shape=jax.ShapeDtypeStruct((b * s, n), jnp.bfloat16),
        scratch_shapes=[pltpu.VMEM((n // LANES, tm, LANES), jnp.float32)] * n_buf,
        compiler_params=_params("parallel"),
        name="merge_groups",
    )(*outs, *lses)


def _count_rows(n_tiles, tile_hits, tq, tk):
    rows = 32
    def body(t, acc):
        hit = tile_hits(pl.multiple_of(t * tk, tk)).astype(jnp.float32)
        return acc + hit.reshape(tk // rows, rows, tq).sum(axis=0)
    acc = lax.fori_loop(0, n_tiles, body, jnp.zeros((rows, tq), jnp.float32))
    return jnp.sum(acc, axis=0, keepdims=True)


def _dsa_select_kernel(qia_ref, qib_ref, ki_ref, wi_ref, bias_ref, key_ref, *, top_k, tq, tk):
    i = pl.program_id(1)
    s_len = key_ref.shape[0]
    n_t = ((i + 1) * tq + tk - 1) // tk
    qpos = lax.broadcasted_iota(jnp.int32, (1, tq), 1) + i * tq
    kloc = lax.broadcasted_iota(jnp.int32, (tk, tq), 0)

    half = IDX_HEADS // 2
    q_t = [ref[...].astype(jnp.float32).T.astype(jnp.bfloat16) for ref in (qia_ref, qib_ref)]
    hpd = SELECT_DOT_COLS // tq
    q_cat = [jnp.concatenate([q[h * IDX_DIM:(h + 1) * IDX_DIM, :] for h in range(h0, h0 + hpd)], axis=1)
             for q in q_t for h0 in range(0, half, hpd)]
    w_t = wi_ref[...].T

    def score_body(t, carry):
        start = pl.multiple_of(t * tk, tk)
        ki = ki_ref[pl.ds(start, tk), :][:, :IDX_DIM].astype(jnp.bfloat16)
        rels = [jnp.dot(ki, q, preferred_element_type=jnp.float32) for q in q_cat]
        sc = jnp.zeros((tk, tq), jnp.float32)
        for d, rel in enumerate(rels):
            for g in range(hpd):
                row = IDX_DIM + d * hpd + g
                sc = sc + jnp.maximum(rel[:, g * tq:(g + 1) * tq], 0.0) * w_t[row:row + 1, :]
        sc = sc + 0.0
        bits = pltpu.bitcast(sc, jnp.int32)
        keys = jnp.where(bits < 0, bits ^ jnp.int32(0x7FFFFFFF), bits)
        key_ref[pl.ds(start, tk), :] = jnp.where(kloc + start <= qpos, keys, jnp.int32(INT_MIN))
        return carry

    lax.fori_loop(0, n_t, score_body, 0)

    kf = jnp.float32(top_k)

    def count_ge(cand):
        return _count_rows(n_t, lambda st: key_ref[pl.ds(st, tk), :] >= cand, tq, tk)

    short = (qpos + 1 < top_k).astype(jnp.float32)

    def bit_cond(c):
        b, _, done, _ = c
        return (b < 32) & (jnp.min(done) < 0.5)

    def bit_body(c):
        b, prefix, done, thr = c
        cand = prefix ^ lax.shift_left(jnp.int32(1), jnp.int32(31) - b)
        cnt = count_ge(cand)
        prefix = jnp.where(cnt >= kf, cand, prefix)
        hit = (cnt == kf) & (done < 0.5)
        thr = jnp.where(hit, cand - 1, thr)
        done = jnp.where(hit, 1.0, done)
        return b + 1, prefix, done, thr

    init = (jnp.int32(0), jnp.full((1, tq), INT_MIN, jnp.int32), short,
            jnp.full((1, tq), INT_MIN, jnp.int32))
    _, prefix, done, thr = lax.while_loop(bit_cond, bit_body, init)
    is_done = done > 0.5
    thr = jnp.where(is_done, thr, prefix)

    def tie_search(_):
        n_gt = _count_rows(n_t, lambda st: key_ref[pl.ds(st, tk), :] > thr, tq, tk)
        need = kf - n_gt
        n_idx_bits = max(1, int(math.ceil(math.log2(s_len))))

        def idx_body(b, p):
            cand = p | lax.shift_left(jnp.int32(1), jnp.int32(n_idx_bits - 1) - b)
            below = _count_rows(
                n_t, lambda st: (key_ref[pl.ds(st, tk), :] == thr) & (kloc + st < cand), tq, tk)
            return jnp.where(below <= need - 1.0, cand, p)

        return lax.fori_loop(0, n_idx_bits, idx_body, jnp.zeros((1, tq), jnp.int32))

    last_tie = lax.cond(jnp.min(done) < 0.5, tie_search,
                        lambda _: jnp.full((1, tq), -1, jnp.int32), 0)
    last_tie = jnp.where(is_done, -1, last_tie)

    def write_body(t, carry):
        start = pl.multiple_of(t * tk, tk)
        kt = key_ref[pl.ds(start, tk), :]
        pos = kloc + start
        sel = ((kt > thr) | ((kt == thr) & (pos <= last_tie))) & (pos <= qpos)
        bias_ref[:, pl.ds(start, tk)] = jnp.where(sel, 0.0, MASKED).T.astype(bias_ref.dtype)
        return carry

    lax.fori_loop(0, n_t, write_body, 0)

    def fill_body(t, carry):
        bias_ref[:, pl.ds(pl.multiple_of(t * tk, tk), tk)] = jnp.full((tq, tk), MASKED, bias_ref.dtype)
        return carry

    lax.fori_loop(n_t, s_len // tk, fill_body, 0)


def _dsa_select(hm, kiwi, tq, tk):
    b, s, _ = hm.shape
    tq, tk = min(tq, s), min(tk, s)
    assert s % tk == 0 and s % tq == 0
    top_k = min(TOPK_MAX, s // 4)
    qi_half = IDX_HEADS * IDX_DIM // 2
    qicol = GRP_IN // qi_half
    return pl.pallas_call(
        functools.partial(_dsa_select_kernel, top_k=top_k, tq=tq, tk=tk),
        grid=(b, s // tq),
        in_specs=[pl.BlockSpec((None, tq, qi_half), lambda bb, i: (bb, i, qicol)),
                  pl.BlockSpec((None, tq, qi_half), lambda bb, i: (bb, i, qicol + 1)),
                  pl.BlockSpec((None, s, LANES), lambda bb, i: (bb, 0, 0)),
                  pl.BlockSpec((None, tq, LANES), lambda bb, i: (bb, i, 0))],
        out_specs=pl.BlockSpec((None, tq, s), lambda bb, i: (bb, i, 0)),
        out_shape=jax.ShapeDtypeStruct((b, s, s), jnp.bfloat16),
        scratch_shapes=[pltpu.VMEM((s, tq), jnp.int32)],
        compiler_params=_params("parallel", "parallel"),
        name="dsa_select",
    )(hm, hm, kiwi, kiwi)


def _dsa_attn_kernel(q_ref, k_ref, v_ref, bias_ref, sl_ref, o_ref,
                     kaug_ref, vaug_ref, m_ref, acc_ref, *, tk, chunk):
    i = pl.program_id(1)
    s_len = k_ref.shape[0]

    @pl.when(i == 0)
    def _():
        lane = lax.broadcasted_iota(jnp.int32, (chunk, HEAD_DIM), 1)
        ones_col = (lane == 0).astype(jnp.bfloat16)
        for c in range(s_len // chunk):
            pos = lax.broadcasted_iota(jnp.int32, (chunk, HEAD_DIM), 0) + c * chunk
            hi = pos - (pos & (POS_SPLIT - 1))
            lo = pos & (POS_SPLIT - 1)
            pcols = jnp.where(lane < N_SLOPE_PIECES, hi, jnp.where(lane < 2 * N_SLOPE_PIECES, lo, 0))
            pcols = pcols.astype(jnp.float32).astype(jnp.bfloat16)
            rows = pl.ds(c * chunk, chunk)
            for kv in range(N_KV):
                kaug_ref[kv, rows, :] = jnp.concatenate(
                    [k_ref[rows, kv * HEAD_DIM:(kv + 1) * HEAD_DIM], pcols], axis=1)
                vaug_ref[kv, rows, :] = jnp.concatenate(
                    [v_ref[rows, kv * HEAD_DIM:(kv + 1) * HEAD_DIM], ones_col], axis=1)

    n_tiles = (i * BLOCK + BLOCK + tk - 1) // tk
    m_ref[...] = jnp.full(m_ref.shape, MASKED, jnp.float32)
    acc_ref[...] = jnp.zeros(acc_ref.shape, jnp.float32)
    qa = [jnp.concatenate(
        [jnp.concatenate(
            [q_ref[:, (kv * GQA_GROUP + g) * HEAD_DIM:(kv * GQA_GROUP + g + 1) * HEAD_DIM]
             for g in range(GQA_GROUP)], axis=0), sl_ref[kv][:, HEAD_DIM:]], axis=1)
        for kv in range(N_KV)]
    n_rep = tk // LANES

    def attn_body(t, carry):
        start = pl.multiple_of(t * tk, tk)
        bias = bias_ref[:, pl.ds(start, tk)].astype(jnp.float32)
        scores = [lax.dot_general(qa[kv], kaug_ref[kv, pl.ds(start, tk), :], (((1,), (1,)), ((), ())),
                                  preferred_element_type=jnp.float32) for kv in range(N_KV)]
        for kv in range(N_KV):
            s_all = scores[kv]
            m_old = m_ref[kv]
            ps, m_news = [], []
            for g in range(GQA_GROUP):
                s = s_all[g * BLOCK:(g + 1) * BLOCK] + bias
                m_new = jnp.maximum(m_old[g * BLOCK:(g + 1) * BLOCK],
                                    jnp.max(s, axis=-1, keepdims=True))
                ps.append(jnp.exp2(s - jnp.concatenate([m_new] * n_rep, axis=1)).astype(jnp.bfloat16))
                m_news.append(m_new)
            m_new = jnp.concatenate(m_news, axis=0)
            pv = jnp.dot(jnp.concatenate(ps, axis=0), vaug_ref[kv, pl.ds(start, tk), :],
                         preferred_element_type=jnp.float32)
            acc_ref[kv] = jnp.exp2(m_old - m_new) * acc_ref[kv] + pv
            m_ref[kv] = m_new
        return carry

    lax.fori_loop(0, n_tiles, attn_body, 0)

    for kv in range(N_KV):
        acc = acc_ref[kv]
        o_all = (acc[:, :HEAD_DIM] / acc[:, HEAD_DIM:HEAD_DIM + 1]).astype(o_ref.dtype)
        for g in range(0, GQA_GROUP, 2):
            h = kv * GQA_GROUP + g
            o_ref[:, h * HEAD_DIM:(h + 2) * HEAD_DIM] = jnp.concatenate(
                [o_all[g * BLOCK:(g + 1) * BLOCK], o_all[(g + 1) * BLOCK:(g + 2) * BLOCK]], axis=1)


def _dsa_attention(hm, bias, tk):
    b, s, _ = hm.shape
    tk = min(tk, s)
    chunk = min(512, s)
    kcol = Q_DIM // KV_DIM
    rows = GQA_GROUP * BLOCK
    return pl.pallas_call(
        functools.partial(_dsa_attn_kernel, tk=tk, chunk=chunk),
        grid=(b, s // BLOCK),
        in_specs=[pl.BlockSpec((None, BLOCK, Q_DIM), lambda bb, i: (bb, i, 0)),
                  pl.BlockSpec((None, s, KV_DIM), lambda bb, i: (bb, 0, kcol)),
                  pl.BlockSpec((None, s, KV_DIM), lambda bb, i: (bb, 0, kcol + 1)),
                  pl.BlockSpec((None, BLOCK, s), lambda bb, i: (bb, i, 0)),
                  pl.BlockSpec((N_KV, rows, LANES), lambda bb, i: (0, 0, 0))],
        out_specs=pl.BlockSpec((None, BLOCK, Q_DIM), lambda bb, i: (bb, i, 0)),
        out_shape=jax.ShapeDtypeStruct((b, s, Q_DIM), jnp.bfloat16),
        scratch_shapes=[pltpu.VMEM((N_KV, s, LANES), jnp.bfloat16),
                        pltpu.VMEM((N_KV, s, LANES), jnp.bfloat16),
                        pltpu.VMEM((N_KV, rows, LANES), jnp.float32),
                        pltpu.VMEM((N_KV, rows, LANES), jnp.float32)],
        compiler_params=_params("parallel", "arbitrary"),
        name="dsa_attention",
    )(hm, hm, hm, bias, _slope_columns(1, 2))


_QKV_SCALE = np.concatenate([np.full(Q_DIM, ATTN_SCALE * LOG2E), np.ones(2 * KV_DIM)])
_DSA_SCALE = np.concatenate([_QKV_SCALE, np.full(IDX_HEADS * IDX_DIM, IDX_DIM ** -0.5)])


def _swa_layer(xb, w_in, layer, sinks, b, s):
    h = _matmul(xb, w_in, layer, 0, _QKV_SCALE, jnp.bfloat16, 1024, 1280)
    o = _band_attention(h.reshape(b, s, GRP_IN), SWA_WINDOW - 1, 1, sinks)
    return o.reshape(b * s, Q_DIM)


def _dilated_layer(xb, w_in, layer, b, s):
    outs, lses, dils = [], [], []
    for g, (window, dil) in enumerate(DIL_PATTERNS):
        h = _matmul(xb, w_in, layer, g * GRP_IN, _QKV_SCALE, jnp.bfloat16, 1024, 1280, batch=b, dil=dil)
        o, lse = _band_attention(h.reshape(b * dil, s // dil, GRP_IN), window // dil, dil)
        outs.append(o.reshape(b, dil, s // dil, Q_DIM))
        lses.append(lse.reshape(b, dil, s // dil, Q_DIM))
        dils.append(dil)
    return _merge_groups(outs, lses, dils, 256)


def _dsa_layer(xb, w_in, layer, b, s):
    hm = _matmul(xb, w_in, layer, 0, _DSA_SCALE, jnp.bfloat16, 2048, 512).reshape(b, s, C_MAIN)
    w_small = jnp.concatenate([w_in[layer, :, C_MAIN:C_MAIN + IDX_DIM],
                               w_in[layer, :, C_MAIN + IDX_DIM:] * IDX_HEADS ** -0.5,
                               jnp.zeros((D_MODEL, LANES - IDX_DIM - IDX_HEADS), w_in.dtype)], axis=1)
    kiwi = _matmul(xb, w_small[None], 0, 0, np.ones(LANES), jnp.float32, 1024, LANES).reshape(b, s, LANES)
    bias = _dsa_select(hm, kiwi, 512, 512)
    o = _dsa_attention(hm, bias, 512)
    return o.reshape(b * s, Q_DIM)


def kernel(x, a_w_in, a_sinks, a_w_out, b_w_in, b_w_out, c_w_in, c_w_out, ln_g, ln_b,
           ffn_w_gate_up, ffn_w_down):
    b, s, d = x.shape
    xf = x.reshape(b * s, d)
    xb = xf
    for i in range(DEPTH):
        kind, j = i % N_MIXERS, i // N_MIXERS
        if kind == 0:
            o = _swa_layer(xb, a_w_in, j, a_sinks[j], b, s)
            w_out = a_w_out
        elif kind == 1:
            o = _dilated_layer(xb, b_w_in, j, b, s)
            w_out = b_w_out
        else:
            o = _dsa_layer(xb, c_w_in, j, b, s)
            w_out = c_w_out
        xf, xb = _proj_residual_ln(o, w_out, j, xf, ln_g[i, 0], ln_b[i, 0], 512, 512)
        hmid = _ffn_gate_up(xb, ffn_w_gate_up, i, 1024, 512)
        xf, xb = _proj_residual_ln(hmid, ffn_w_down, i, xf, ln_g[i, 1], ln_b[i, 1], 256, 512)
    return xf.reshape(b, s, d)
```

```python
import functools
import math

import numpy as np
import jax
import jax.numpy as jnp
from jax import lax
from jax.experimental import pallas as pl
from jax.experimental.pallas import tpu as pltpu

D_MODEL = 2048
DEPTH = 4
N_MIXERS = 3
HEAD_DIM = 64
N_HEADS = D_MODEL // HEAD_DIM
N_KV = N_HEADS // 8
GQA_GROUP = N_HEADS // N_KV
Q_DIM = N_HEADS * HEAD_DIM
KV_DIM = N_KV * HEAD_DIM
ATTN_SCALE = HEAD_DIM ** -0.5
BLOCK = 128
SWA_WINDOW = 128
DIL_PATTERNS = ((128, 1), (512, 4), (2048, 16))
IDX_HEADS = 16
IDX_DIM = 64
TOPK_MAX = 256
DEEPNORM_ALPHA = (2 * DEPTH) ** 0.25
LN_EPS = 1e-5
GRP_IN = Q_DIM + 2 * KV_DIM
C_MAIN = Q_DIM + 2 * KV_DIM + IDX_HEADS * IDX_DIM
LANES = 128
MASKED = -1e30
INT_MIN = -2 ** 31
LOG2E = 1.4426950408889634
VMEM_LIMIT = 56 * 1024 * 1024

_SLOPES = [float(np.float32(2.0 ** (-8.0 * (i + 1) / N_HEADS))) for i in range(N_HEADS)]
N_SLOPE_PIECES = 3
POS_SPLIT = 64
BAND_Q_BLOCKS = 4
SELECT_DOT_COLS = 2048


def _params(*sem):
    return pltpu.CompilerParams(dimension_semantics=sem, vmem_limit_bytes=VMEM_LIMIT)


def _mm_kernel(x_ref, w_ref, sc_ref, o_ref, wb_ref, *acc, dil):
    @pl.when(pl.program_id(1) == 0)
    def _():
        wb_ref[...] = (w_ref[...] * sc_ref[...]).astype(jnp.bfloat16)

    y = jnp.dot(x_ref[...].astype(jnp.bfloat16), wb_ref[...], preferred_element_type=jnp.float32)
    if dil == 1:
        o_ref[...] = y.astype(o_ref.dtype)
    else:
        acc_ref, = acc
        per = acc_ref.shape[1] // dil
        for c in range(acc_ref.shape[0]):
            cols = slice(c * LANES, (c + 1) * LANES)
            acc_ref[c] = y[:, cols]
            for r in range(dil):
                o_ref[r, :, cols] = acc_ref[c, pl.ds(r, per, stride=dil), :].astype(o_ref.dtype)


def _matmul(x, w, layer, col0, scale, out_dtype, tm, tn, batch=1, dil=1):
    m, k = x.shape
    n = scale.shape[0]
    tm, tn = min(tm, m // batch), min(tn, n)
    assert m % (batch * tm) == 0 and n % tn == 0 and col0 % tn == 0 and tm % (16 * dil) == 0
    j0 = col0 // tn
    if dil == 1:
        out_specs = pl.BlockSpec((tm, tn), lambda j, i: (i, j))
        out_shape = jax.ShapeDtypeStruct((m, n), out_dtype)
        scratch = []
    else:
        per_batch = m // batch // tm
        out_specs = pl.BlockSpec((None, dil, tm // dil, tn),
                                 lambda j, i: (i // per_batch, 0, i % per_batch, j))
        out_shape = jax.ShapeDtypeStruct((batch, dil, m // batch // dil, n), out_dtype)
        scratch = [pltpu.VMEM((tn // LANES, tm, LANES), jnp.float32)]
    return pl.pallas_call(
        functools.partial(_mm_kernel, dil=dil),
        grid=(n // tn, m // tm),
        in_specs=[pl.BlockSpec((tm, k), lambda j, i: (i, 0)),
                  pl.BlockSpec((None, k, tn), lambda j, i: (layer, 0, j0 + j)),
                  pl.BlockSpec((1, tn), lambda j, i: (0, j))],
        out_specs=out_specs,
        out_shape=out_shape,
        scratch_shapes=[pltpu.VMEM((k, tn), jnp.bfloat16)] + scratch,
        compiler_params=_params("parallel", "arbitrary"),
        name="proj_matmul",
    )(x, w, jnp.asarray(scale, jnp.float32).reshape(1, n))


def _gate_up_kernel(x_ref, wg_ref, wu_ref, o_ref, wgb_ref, wub_ref):
    @pl.when(pl.program_id(1) == 0)
    def _():
        wgb_ref[...] = wg_ref[...].astype(jnp.bfloat16)
        wub_ref[...] = wu_ref[...].astype(jnp.bfloat16)

    x = x_ref[...]
    g = jnp.dot(x, wgb_ref[...], preferred_element_type=jnp.float32)
    u = jnp.dot(x, wub_ref[...], preferred_element_type=jnp.float32)
    o_ref[...] = (g * (1.0 / (1.0 + jnp.exp(-g))) * u).astype(o_ref.dtype)


def _ffn_gate_up(x, w, layer, tm, tn):
    m, k = x.shape
    d_ff = w.shape[2] // 2
    tm = min(tm, m)
    assert m % tm == 0 and d_ff % tn == 0
    nj = d_ff // tn
    return pl.pallas_call(
        _gate_up_kernel,
        grid=(nj, m // tm),
        in_specs=[pl.BlockSpec((tm, k), lambda j, i: (i, 0)),
                  pl.BlockSpec((None, k, tn), lambda j, i: (layer, 0, j)),
                  pl.BlockSpec((None, k, tn), lambda j, i: (layer, 0, j + nj))],
        out_specs=pl.BlockSpec((tm, tn), lambda j, i: (i, j)),
        out_shape=jax.ShapeDtypeStruct((m, d_ff), jnp.bfloat16),
        scratch_shapes=[pltpu.VMEM((k, tn), jnp.bfloat16), pltpu.VMEM((k, tn), jnp.bfloat16)],
        compiler_params=_params("parallel", "arbitrary"),
        name="ffn_gate_up",
    )(x, w, w)


def _proj_ln_kernel(a_ref, w_ref, r_ref, g_ref, b_ref, o_ref, ob_ref, wb_ref, *, n_w, tkc):
    s = pl.program_id(0)

    @pl.when(s < n_w)
    def _():
        wb_ref[pl.ds(pl.multiple_of(s * tkc, tkc), tkc), :] = w_ref[...].astype(jnp.bfloat16)

    @pl.when(s >= n_w)
    def _():
        z = DEEPNORM_ALPHA * r_ref[...] + jnp.dot(a_ref[...], wb_ref[...],
                                                  preferred_element_type=jnp.float32)
        mu = jnp.mean(z, axis=-1, keepdims=True)
        zc = z - mu
        var = jnp.mean(zc * zc, axis=-1, keepdims=True)
        y = zc * lax.rsqrt(var + LN_EPS) * g_ref[...] + b_ref[...]
        o_ref[...] = y
        ob_ref[...] = y.astype(jnp.bfloat16)


def _proj_residual_ln(a, w, layer, resid, g, b, tm, tkc):
    m, k = a.shape
    n = w.shape[2]
    tm, tkc = min(tm, m), min(tkc, k)
    assert m % tm == 0 and k % tkc == 0
    n_w = k // tkc
    row = lambda s: (jnp.maximum(s - n_w, 0), 0)
    return pl.pallas_call(
        functools.partial(_proj_ln_kernel, n_w=n_w, tkc=tkc),
        grid=(n_w + m // tm,),
        in_specs=[pl.BlockSpec((tm, k), row),
                  pl.BlockSpec((None, tkc, n), lambda s: (layer, jnp.minimum(s, n_w - 1), 0)),
                  pl.BlockSpec((tm, n), row),
                  pl.BlockSpec((1, n), lambda s: (0, 0)),
                  pl.BlockSpec((1, n), lambda s: (0, 0))],
        out_specs=[pl.BlockSpec((tm, n), row), pl.BlockSpec((tm, n), row)],
        out_shape=[jax.ShapeDtypeStruct((m, n), jnp.float32),
                   jax.ShapeDtypeStruct((m, n), jnp.bfloat16)],
        scratch_shapes=[pltpu.VMEM((k, n), jnp.bfloat16)],
        compiler_params=_params("arbitrary"),
        name="proj_residual_ln",
    )(a, w, resid, g.reshape(1, n), b.reshape(1, n))


def _bf16_pieces(x, n):
    rest = np.asarray(x, np.float64)
    pieces = []
    for _ in range(n):
        p = rest.astype(np.float32).astype(jnp.bfloat16).astype(np.float64)
        pieces.append(p)
        rest = rest - p
    return pieces


def _slope_columns(mult, n_parts):
    out = np.zeros((N_KV, GQA_GROUP * BLOCK, LANES), np.float32)
    for h in range(N_HEADS):
        pieces = [float(p) for p in _bf16_pieces(np.float64(_SLOPES[h]) * mult * LOG2E, N_SLOPE_PIECES)]
        kv, g = divmod(h, GQA_GROUP)
        out[kv, g * BLOCK:(g + 1) * BLOCK, HEAD_DIM:HEAD_DIM + n_parts * N_SLOPE_PIECES] = np.asarray(
            pieces * n_parts, np.float32)
    return jnp.asarray(out, jnp.bfloat16)


def _band_q_columns(dilation):
    out = np.zeros((N_KV, GQA_GROUP * BLOCK, LANES), np.float32)
    qpos = BLOCK + np.arange(BLOCK, dtype=np.float64)
    for h in range(N_HEADS):
        s2 = np.float64(_SLOPES[h]) * dilation * LOG2E
        kv, g = divmod(h, GQA_GROUP)
        rows = slice(g * BLOCK, (g + 1) * BLOCK)
        for c, p in enumerate(_bf16_pieces(s2, N_SLOPE_PIECES)):
            out[kv, rows, HEAD_DIM + c] = p
        for c, p in enumerate(_bf16_pieces(-s2 * qpos, N_SLOPE_PIECES)):
            out[kv, rows, HEAD_DIM + N_SLOPE_PIECES + c] = p
    return jnp.asarray(out, jnp.bfloat16)


def _band_k_columns():
    out = np.zeros((2 * BLOCK, LANES), np.float32)
    out[:, HEAD_DIM:HEAD_DIM + N_SLOPE_PIECES] = np.arange(2 * BLOCK, dtype=np.float32)[:, None]
    out[:, HEAD_DIM + N_SLOPE_PIECES:HEAD_DIM + 2 * N_SLOPE_PIECES] = 1.0
    return jnp.asarray(out, jnp.bfloat16)


def _band_kernel(*refs, max_j, has_sink):
    refs, (k_scr, v_scr) = refs[:-2], refs[-2:]
    if has_sink:
        sink_ref, q_ref, kp_ref, kc_ref, vp_ref, vc_ref, qc_ref, kc_cols_ref, o_ref = refs
    else:
        q_ref, kp_ref, kc_ref, vp_ref, vc_ref, qc_ref, kc_cols_ref, o_ref, lse_ref = refs
    i = pl.program_id(1)
    n_qb = q_ref.shape[0] // BLOCK
    k_scr[:BLOCK, :] = kp_ref[...]
    k_scr[BLOCK:, :] = kc_ref[...]
    v_scr[:BLOCK, :] = vp_ref[...]
    v_scr[BLOCK:, :] = vc_ref[...]
    qpos = lax.broadcasted_iota(jnp.int32, (BLOCK, 2 * BLOCK), 0) + BLOCK
    kpos = lax.broadcasted_iota(jnp.int32, (BLOCK, 2 * BLOCK), 1)
    j = qpos - kpos
    in_band = (j >= 0) & (j <= max_j)
    low = lax.broadcasted_iota(jnp.int32, (BLOCK, LANES), 1) < HEAD_DIM
    k_cols = kc_cols_ref[:, HEAD_DIM:]
    ones64 = jnp.ones((2 * BLOCK, HEAD_DIM), jnp.bfloat16)
    ones128 = jnp.ones((2 * BLOCK, LANES), jnp.bfloat16)

    def block(b, carry):
        out_rows = pl.ds(pl.multiple_of(b * BLOCK, BLOCK), BLOCK)
        k = k_scr[pl.ds(pl.multiple_of(b * BLOCK, BLOCK), 2 * BLOCK), :]
        v = v_scr[pl.ds(pl.multiple_of(b * BLOCK, BLOCK), 2 * BLOCK), :]
        bias = jnp.where(in_band & ((kpos >= BLOCK) | (i > 0) | (b > 0)), 0.0, MASKED)
        band_block(b, out_rows, k, v, bias)
        return carry

    def band_block(b, out_rows, k, v, bias):
        def scores(kv):
            ka = jnp.concatenate([k[:, kv * HEAD_DIM:(kv + 1) * HEAD_DIM], k_cols], axis=1)
            qa = jnp.concatenate(
                [jnp.concatenate([q_ref[out_rows, (kv * GQA_GROUP + g) * HEAD_DIM:
                                        (kv * GQA_GROUP + g + 1) * HEAD_DIM]
                                  for g in range(GQA_GROUP)], axis=0),
                 qc_ref[kv][:, HEAD_DIM:]], axis=1)
            return lax.dot_general(qa, ka, (((1,), (1,)), ((), ())),
                                   preferred_element_type=jnp.float32)

        all_scores = [scores(kv) for kv in range(N_KV)]
        for kv in range(N_KV):
            band_group(kv, all_scores[kv], out_rows, v, bias)

    def band_group(kv, s_all, out_rows, v, bias):
        h0 = kv * GQA_GROUP
        v_kv = v[:, kv * HEAD_DIM:(kv + 1) * HEAD_DIM]
        v_even = jnp.concatenate([v_kv, ones64, ones128], axis=1)
        v_odd = jnp.concatenate([ones64, v_kv, ones128], axis=1)
        ps, ms = [], []
        for g in range(GQA_GROUP):
            s = s_all[g * BLOCK:(g + 1) * BLOCK] + bias
            m = jnp.max(s, axis=-1, keepdims=True)
            if has_sink:
                m = jnp.maximum(m, sink_ref[h0 + g] * LOG2E)
            ps.append(jnp.exp2(s - m).astype(jnp.bfloat16))
            ms.append(m)
        acc_e = jnp.dot(jnp.concatenate(ps[0::2], axis=0), v_even,
                        preferred_element_type=jnp.float32)
        acc_o = jnp.dot(jnp.concatenate(ps[1::2], axis=0), v_odd,
                        preferred_element_type=jnp.float32)
        for t in range(GQA_GROUP // 2):
            h = h0 + 2 * t
            rows = slice(t * BLOCK, (t + 1) * BLOCK)
            num = jnp.where(low, acc_e[rows, :LANES], acc_o[rows, :LANES])
            den = jnp.where(low, acc_e[rows, LANES:], acc_o[rows, LANES:])
            m2 = jnp.where(low, jnp.broadcast_to(ms[2 * t], (BLOCK, LANES)),
                           jnp.broadcast_to(ms[2 * t + 1], (BLOCK, LANES)))
            if has_sink:
                sink2 = jnp.where(low, sink_ref[h] * LOG2E, sink_ref[h + 1] * LOG2E)
                den = den + jnp.exp2(sink2 - m2)
            cols = slice(h * HEAD_DIM, (h + 2) * HEAD_DIM)
            o_ref[out_rows, cols] = (num / den).astype(o_ref.dtype)
            if not has_sink:
                lse_ref[out_rows, cols] = m2 * (1.0 / LOG2E) + jnp.log(den)

    lax.fori_loop(0, n_qb, block, 0)


def _band_attention(h, max_j, dilation, sinks=None):
    r, l, _ = h.shape
    n_qb = min(BAND_Q_BLOCKS, l // BLOCK)
    tq = n_qb * BLOCK
    assert l % tq == 0
    kcol, vcol = Q_DIM // KV_DIM, Q_DIM // KV_DIM + 1
    has_sink = sinks is not None
    rows = GQA_GROUP * BLOCK
    prev = lambda s, i, col: (s, jnp.maximum(i * n_qb - 1, 0), col)
    in_specs = [pl.BlockSpec((None, tq, Q_DIM), lambda s, i: (s, i, 0)),
                pl.BlockSpec((None, BLOCK, KV_DIM), lambda s, i: prev(s, i, kcol)),
                pl.BlockSpec((None, tq, KV_DIM), lambda s, i: (s, i, kcol)),
                pl.BlockSpec((None, BLOCK, KV_DIM), lambda s, i: prev(s, i, vcol)),
                pl.BlockSpec((None, tq, KV_DIM), lambda s, i: (s, i, vcol)),
                pl.BlockSpec((N_KV, rows, LANES), lambda s, i: (0, 0, 0)),
                pl.BlockSpec((2 * BLOCK, LANES), lambda s, i: (0, 0))]
    o_spec = pl.BlockSpec((None, tq, Q_DIM), lambda s, i: (s, i, 0))
    args = [h, h, h, h, h, _band_q_columns(dilation), _band_k_columns()]
    if has_sink:
        in_specs = [pl.BlockSpec(memory_space=pltpu.SMEM)] + in_specs
        args = [sinks.astype(jnp.float32)] + args
        out_specs = o_spec
        out_shape = jax.ShapeDtypeStruct((r, l, Q_DIM), jnp.bfloat16)
    else:
        out_specs = [o_spec, o_spec]
        out_shape = [jax.ShapeDtypeStruct((r, l, Q_DIM), jnp.bfloat16),
                     jax.ShapeDtypeStruct((r, l, Q_DIM), jnp.float32)]
    return pl.pallas_call(
        functools.partial(_band_kernel, max_j=max_j, has_sink=has_sink),
        grid=(r, l // tq),
        in_specs=in_specs,
        out_specs=out_specs,
        out_shape=out_shape,
        scratch_shapes=[pltpu.VMEM((tq + BLOCK, KV_DIM), jnp.bfloat16)] * 2,
        compiler_params=_params("parallel", "parallel"),
        name="band_attention",
    )(*args)


def _merge_kernel(*refs, dils):
    n_g = len(dils)
    o_refs, l_refs, out_ref, bufs = refs[:n_g], refs[n_g:2 * n_g], refs[2 * n_g], list(refs[2 * n_g + 1:])

    def natural(ref, d):
        if d == 1:
            return ref[0].astype(jnp.float32)
        buf = bufs.pop()
        per = ref.shape[1]
        for c in range(buf.shape[0]):
            for r in range(d):
                buf[c, pl.ds(r, per, stride=d), :] = ref[r, :, c * LANES:(c + 1) * LANES].astype(jnp.float32)
        return jnp.concatenate([buf[c] for c in range(buf.shape[0])], axis=1)

    os = [natural(ref, d) for ref, d in zip(o_refs, dils)]
    ls = [natural(ref, d) for ref, d in zip(l_refs, dils)]
    m = functools.reduce(jnp.maximum, ls)
    es = [jnp.exp(l - m) for l in ls]
    num = sum(e * o for e, o in zip(es, os))
    out_ref[...] = (num / sum(es)).astype(out_ref.dtype)


def _merge_groups(outs, lses, dils, tm):
    b, _, s, n = outs[0].shape
    tm = min(tm, s)
    assert s % tm == 0 and all(tm % (8 * d) == 0 for d in dils)
    per_batch = s // tm
    specs = [pl.BlockSpec((None, d, tm // d, n), lambda i: (i // per_batch, 0, i % per_batch, 0))
             for d in dils]
    n_buf = 2 * sum(d > 1 for d in dils)
    return pl.pallas_call(
        functools.partial(_merge_kernel, dils=tuple(dils)),
        grid=(b * per_batch,),
        in_specs=specs + specs,
        out_specs=pl.BlockSpec((tm, n), lambda i: (i, 0)),
        out_shape=jax.ShapeDtypeStruct((b * s, n), jnp.bfloat16),
        scratch_shapes=[pltpu.VMEM((n // LANES, tm, LANES), jnp.float32)] * n_buf,
        compiler_params=_params("parallel"),
        name="merge_groups",
    )(*outs, *lses)


def _count_rows(n_tiles, tile_hits, tq, tk):
    rows = 32
    def body(t, acc):
        hit = tile_hits(pl.multiple_of(t * tk, tk)).astype(jnp.float32)
        return acc + hit.reshape(tk // rows, rows, tq).sum(axis=0)
    acc = lax.fori_loop(0, n_tiles, body, jnp.zeros((rows, tq), jnp.float32))
    return jnp.sum(acc, axis=0, keepdims=True)


def _dsa_select_kernel(qia_ref, qib_ref, ki_ref, wi_ref, bias_ref, key_ref, *, top_k, tq, tk):
    i = pl.program_id(1)
    s_len = key_ref.shape[0]
    n_t = ((i + 1) * tq + tk - 1) // tk
    qpos = lax.broadcasted_iota(jnp.int32, (1, tq), 1) + i * tq
    kloc = lax.broadcasted_iota(jnp.int32, (tk, tq), 0)

    half = IDX_HEADS // 2
    q_t = [ref[...].astype(jnp.float32).T.astype(jnp.bfloat16) for ref in (qia_ref, qib_ref)]
    hpd = SELECT_DOT_COLS // tq
    q_cat = [jnp.concatenate([q[h * IDX_DIM:(h + 1) * IDX_DIM, :] for h in range(h0, h0 + hpd)], axis=1)
             for q in q_t for h0 in range(0, half, hpd)]
    w_t = wi_ref[...].T

    def score_body(t, carry):
        start = pl.multiple_of(t * tk, tk)
        ki = ki_ref[pl.ds(start, tk), :][:, :IDX_DIM].astype(jnp.bfloat16)
        rels = [jnp.dot(ki, q, preferred_element_type=jnp.float32) for q in q_cat]
        sc = jnp.zeros((tk, tq), jnp.float32)
        for d, rel in enumerate(rels):
            for g in range(hpd):
                row = IDX_DIM + d * hpd + g
                sc = sc + jnp.maximum(rel[:, g * tq:(g + 1) * tq], 0.0) * w_t[row:row + 1, :]
        sc = sc + 0.0
        bits = pltpu.bitcast(sc, jnp.int32)
        keys = jnp.where(bits < 0, bits ^ jnp.int32(0x7FFFFFFF), bits)
        key_ref[pl.ds(start, tk), :] = jnp.where(kloc + start <= qpos, keys, jnp.int32(INT_MIN))
        return carry

    lax.fori_loop(0, n_t, score_body, 0)

    kf = jnp.float32(top_k)

    def count_ge(cand):
        return _count_rows(n_t, lambda st: key_ref[pl.ds(st, tk), :] >= cand, tq, tk)

    short = (qpos + 1 < top_k).astype(jnp.float32)

    def bit_cond(c):
        b, _, done, _ = c
        return (b < 32) & (jnp.min(done) < 0.5)

    def bit_body(c):
        b, prefix, done, thr = c
        cand = prefix ^ lax.shift_left(jnp.int32(1), jnp.int32(31) - b)
        cnt = count_ge(cand)
        prefix = jnp.where(cnt >= kf, cand, prefix)
        hit = (cnt == kf) & (done < 0.5)
        thr = jnp.where(hit, cand - 1, thr)
        done = jnp.where(hit, 1.0, done)
        return b + 1, prefix, done, thr

    init = (jnp.int32(0), jnp.full((1, tq), INT_MIN, jnp.int32), short,
            jnp.full((1, tq), INT_MIN, jnp.int32))
    _, prefix, done, thr = lax.while_loop(bit_cond, bit_body, init)
    is_done = done > 0.5
    thr = jnp.where(is_done, thr, prefix)

    def tie_search(_):
        n_gt = _count_rows(n_t, lambda st: key_ref[pl.ds(st, tk), :] > thr, tq, tk)
        need = kf - n_gt
        n_idx_bits = max(1, int(math.ceil(math.log2(s_len))))

        def idx_body(b, p):
            cand = p | lax.shift_left(jnp.int32(1), jnp.int32(n_idx_bits - 1) - b)
            below = _count_rows(
                n_t, lambda st: (key_ref[pl.ds(st, tk), :] == thr) & (kloc + st < cand), tq, tk)
            return jnp.where(below <= need - 1.0, cand, p)

        return lax.fori_loop(0, n_idx_bits, idx_body, jnp.zeros((1, tq), jnp.int32))

    last_tie = lax.cond(jnp.min(done) < 0.5, tie_search,
                        lambda _: jnp.full((1, tq), -1, jnp.int32), 0)
    last_tie = jnp.where(is_done, -1, last_tie)

    def write_body(t, carry):
        start = pl.multiple_of(t * tk, tk)
        kt = key_ref[pl.ds(start, tk), :]
        pos = kloc + start
        sel = ((kt > thr) | ((kt == thr) & (pos <= last_tie))) & (pos <= qpos)
        bias_ref[:, pl.ds(start, tk)] = jnp.where(sel, 0.0, MASKED).T.astype(bias_ref.dtype)
        return carry

    lax.fori_loop(0, n_t, write_body, 0)

    def fill_body(t, carry):
        bias_ref[:, pl.ds(pl.multiple_of(t * tk, tk), tk)] = jnp.full((tq, tk), MASKED, bias_ref.dtype)
        return carry

    lax.fori_loop(n_t, s_len // tk, fill_body, 0)


def _dsa_select(hm, kiwi, tq, tk):
    b, s, _ = hm.shape
    tq, tk = min(tq, s), min(tk, s)
    assert s % tk == 0 and s % tq == 0
    top_k = min(TOPK_MAX, s // 4)
    qi_half = IDX_HEADS * IDX_DIM // 2
    qicol = GRP_IN // qi_half
    return pl.pallas_call(
        functools.partial(_dsa_select_kernel, top_k=top_k, tq=tq, tk=tk),
        grid=(b, s // tq),
        in_specs=[pl.BlockSpec((None, tq, qi_half), lambda bb, i: (bb, i, qicol)),
                  pl.BlockSpec((None, tq, qi_half), lambda bb, i: (bb, i, qicol + 1)),
                  pl.BlockSpec((None, s, LANES), lambda bb, i: (bb, 0, 0)),
                  pl.BlockSpec((None, tq, LANES), lambda bb, i: (bb, i, 0))],
        out_specs=pl.BlockSpec((None, tq, s), lambda bb, i: (bb, i, 0)),
        out_shape=jax.ShapeDtypeStruct((b, s, s), jnp.bfloat16),
        scratch_shapes=[pltpu.VMEM((s, tq), jnp.int32)],
        compiler_params=_params("parallel", "parallel"),
        name="dsa_select",
    )(hm, hm, kiwi, kiwi)


def _dsa_attn_kernel(q_ref, k_ref, v_ref, bias_ref, sl_ref, o_ref,
                     kaug_ref, vaug_ref, m_ref, acc_ref, *, tk, chunk):
    i = pl.program_id(1)
    s_len = k_ref.shape[0]

    @pl.when(i == 0)
    def _():
        lane = lax.broadcasted_iota(jnp.int32, (chunk, HEAD_DIM), 1)
        ones_col = (lane == 0).astype(jnp.bfloat16)
        for c in range(s_len // chunk):
            pos = lax.broadcasted_iota(jnp.int32, (chunk, HEAD_DIM), 0) + c * chunk
            hi = pos - (pos & (POS_SPLIT - 1))
            lo = pos & (POS_SPLIT - 1)
            pcols = jnp.where(lane < N_SLOPE_PIECES, hi, jnp.where(lane < 2 * N_SLOPE_PIECES, lo, 0))
            pcols = pcols.astype(jnp.float32).astype(jnp.bfloat16)
            rows = pl.ds(c * chunk, chunk)
            for kv in range(N_KV):
                kaug_ref[kv, rows, :] = jnp.concatenate(
                    [k_ref[rows, kv * HEAD_DIM:(kv + 1) * HEAD_DIM], pcols], axis=1)
                vaug_ref[kv, rows, :] = jnp.concatenate(
                    [v_ref[rows, kv * HEAD_DIM:(kv + 1) * HEAD_DIM], ones_col], axis=1)

    n_tiles = (i * BLOCK + BLOCK + tk - 1) // tk
    m_ref[...] = jnp.full(m_ref.shape, MASKED, jnp.float32)
    acc_ref[...] = jnp.zeros(acc_ref.shape, jnp.float32)
    qa = [jnp.concatenate(
        [jnp.concatenate(
            [q_ref[:, (kv * GQA_GROUP + g) * HEAD_DIM:(kv * GQA_GROUP + g + 1) * HEAD_DIM]
             for g in range(GQA_GROUP)], axis=0), sl_ref[kv][:, HEAD_DIM:]], axis=1)
        for kv in range(N_KV)]
    n_rep = tk // LANES

    def attn_body(t, carry):
        start = pl.multiple_of(t * tk, tk)
        bias = bias_ref[:, pl.ds(start, tk)].astype(jnp.float32)
        scores = [lax.dot_general(qa[kv], kaug_ref[kv, pl.ds(start, tk), :], (((1,), (1,)), ((), ())),
                                  preferred_element_type=jnp.float32) for kv in range(N_KV)]
        for kv in range(N_KV):
            s_all = scores[kv]
            m_old = m_ref[kv]
            ps, m_news = [], []
            for g in range(GQA_GROUP):
                s = s_all[g * BLOCK:(g + 1) * BLOCK] + bias
                m_new = jnp.maximum(m_old[g * BLOCK:(g + 1) * BLOCK],
                                    jnp.max(s, axis=-1, keepdims=True))
                ps.append(jnp.exp2(s - jnp.concatenate([m_new] * n_rep, axis=1)).astype(jnp.bfloat16))
                m_news.append(m_new)
            m_new = jnp.concatenate(m_news, axis=0)
            pv = jnp.dot(jnp.concatenate(ps, axis=0), vaug_ref[kv, pl.ds(start, tk), :],
                         preferred_element_type=jnp.float32)
            acc_ref[kv] = jnp.exp2(m_old - m_new) * acc_ref[kv] + pv
            m_ref[kv] = m_new
        return carry

    lax.fori_loop(0, n_tiles, attn_body, 0)

    for kv in range(N_KV):
        acc = acc_ref[kv]
        o_all = (acc[:, :HEAD_DIM] / acc[:, HEAD_DIM:HEAD_DIM + 1]).astype(o_ref.dtype)
        for g in range(0, GQA_GROUP, 2):
            h = kv * GQA_GROUP + g
            o_ref[:, h * HEAD_DIM:(h + 2) * HEAD_DIM] = jnp.concatenate(
                [o_all[g * BLOCK:(g + 1) * BLOCK], o_all[(g + 1) * BLOCK:(g + 2) * BLOCK]], axis=1)


def _dsa_attention(hm, bias, tk):
    b, s, _ = hm.shape
    tk = min(tk, s)
    chunk = min(512, s)
    kcol = Q_DIM // KV_DIM
    rows = GQA_GROUP * BLOCK
    return pl.pallas_call(
        functools.partial(_dsa_attn_kernel, tk=tk, chunk=chunk),
        grid=(b, s // BLOCK),
        in_specs=[pl.BlockSpec((None, BLOCK, Q_DIM), lambda bb, i: (bb, i, 0)),
                  pl.BlockSpec((None, s, KV_DIM), lambda bb, i: (bb, 0, kcol)),
                  pl.BlockSpec((None, s, KV_DIM), lambda bb, i: (bb, 0, kcol + 1)),
                  pl.BlockSpec((None, BLOCK, s), lambda bb, i: (bb, i, 0)),
                  pl.BlockSpec((N_KV, rows, LANES), lambda bb, i: (0, 0, 0))],
        out_specs=pl.BlockSpec((None, BLOCK, Q_DIM), lambda bb, i: (bb, i, 0)),
        out_shape=jax.ShapeDtypeStruct((b, s, Q_DIM), jnp.bfloat16),
        scratch_shapes=[pltpu.VMEM((N_KV, s, LANES), jnp.bfloat16),
                        pltpu.VMEM((N_KV, s, LANES), jnp.bfloat16),
                        pltpu.VMEM((N_KV, rows, LANES), jnp.float32),
                        pltpu.VMEM((N_KV, rows, LANES), jnp.float32)],
        compiler_params=_params("parallel", "arbitrary"),
        name="dsa_attention",
    )(hm, hm, hm, bias, _slope_columns(1, 2))


_QKV_SCALE = np.concatenate([np.full(Q_DIM, ATTN_SCALE * LOG2E), np.ones(2 * KV_DIM)])
_DSA_SCALE = np.concatenate([_QKV_SCALE, np.full(IDX_HEADS * IDX_DIM, IDX_DIM ** -0.5)])


def _swa_layer(xb, w_in, layer, sinks, b, s):
    h = _matmul(xb, w_in, layer, 0, _QKV_SCALE, jnp.bfloat16, 1024, 1280)
    o = _band_attention(h.reshape(b, s, GRP_IN), SWA_WINDOW - 1, 1, sinks)
    return o.reshape(b * s, Q_DIM)


def _dilated_layer(xb, w_in, layer, b, s):
    outs, lses, dils = [], [], []
    for g, (window, dil) in enumerate(DIL_PATTERNS):
        h = _matmul(xb, w_in, layer, g * GRP_IN, _QKV_SCALE, jnp.bfloat16, 1024, 1280, batch=b, dil=dil)
        o, lse = _band_attention(h.reshape(b * dil, s // dil, GRP_IN), window // dil, dil)
        outs.append(o.reshape(b, dil, s // dil, Q_DIM))
        lses.append(lse.reshape(b, dil, s // dil, Q_DIM))
        dils.append(dil)
    return _merge_groups(outs, lses, dils, 256)


def _dsa_layer(xb, w_in, layer, b, s):
    hm = _matmul(xb, w_in, layer, 0, _DSA_SCALE, jnp.bfloat16, 2048, 512).reshape(b, s, C_MAIN)
    w_small = jnp.concatenate([w_in[layer, :, C_MAIN:C_MAIN + IDX_DIM],
                               w_in[layer, :, C_MAIN + IDX_DIM:] * IDX_HEADS ** -0.5,
                               jnp.zeros((D_MODEL, LANES - IDX_DIM - IDX_HEADS), w_in.dtype)], axis=1)
    kiwi = _matmul(xb, w_small[None], 0, 0, np.ones(LANES), jnp.float32, 1024, LANES).reshape(b, s, LANES)
    bias = _dsa_select(hm, kiwi, 512, 512)
    o = _dsa_attention(hm, bias, 512)
    return o.reshape(b * s, Q_DIM)


def kernel(x, a_w_in, a_sinks, a_w_out, b_w_in, b_w_out, c_w_in, c_w_out, ln_g, ln_b,
           ffn_w_gate_up, ffn_w_down):
    b, s, d = x.shape
    xf = x.reshape(b * s, d)
    xb = xf
    for i in range(DEPTH):
        kind, j = i % N_MIXERS, i // N_MIXERS
        if kind == 0:
            o = _swa_layer(xb, a_w_in, j, a_sinks[j], b, s)
            w_out = a_w_out
        elif kind == 1:
            o = _dilated_layer(xb, b_w_in, j, b, s)
            w_out = b_w_out
        else:
            o = _dsa_layer(xb, c_w_in, j, b, s)
            w_out = c_w_out
        xf, xb = _proj_residual_ln(o, w_out, j, xf, ln_g[i, 0], ln_b[i, 0], 512, 512)
        hmid = _ffn_gate_up(xb, ffn_w_gate_up, i, 1024, 512)
        xf, xb = _proj_residual_ln(hmid, ffn_w_down, i, xf, ln_g[i, 1], ln_b[i, 1], 256, 512)
    return xf.reshape(b, s, d)
```

```python
import functools
import math

import numpy as np
import jax
import jax.numpy as jnp
from jax import lax
from jax.experimental import pallas as pl
from jax.experimental.pallas import tpu as pltpu

D_MODEL = 2048
DEPTH = 4
N_MIXERS = 3
HEAD_DIM = 64
N_HEADS = D_MODEL // HEAD_DIM
N_KV = N_HEADS // 8
GQA_GROUP = N_HEADS // N_KV
Q_DIM = N_HEADS * HEAD_DIM
KV_DIM = N_KV * HEAD_DIM
ATTN_SCALE = HEAD_DIM ** -0.5
BLOCK = 128
SWA_WINDOW = 128
DIL_PATTERNS = ((128, 1), (512, 4), (2048, 16))
N_DIL = len(DIL_PATTERNS)
IDX_HEADS = 16
IDX_DIM = 64
TOPK_MAX = 256
D_FF = 256 * math.ceil(8 * D_MODEL / (3 * 256))
DEEPNORM_ALPHA = (2 * DEPTH) ** 0.25
LN_EPS = 1e-5
GRP_IN = Q_DIM + 2 * KV_DIM
C_MAIN = Q_DIM + 2 * KV_DIM + IDX_HEADS * IDX_DIM
C_IN = C_MAIN + IDX_DIM + IDX_HEADS
LANES = 128
MASKED = -1e30
INT_MIN = -2 ** 31
LOG2E = 1.4426950408889634
VMEM_LIMIT = 56 * 1024 * 1024

_SLOPES = [float(np.float32(2.0 ** (-8.0 * (i + 1) / N_HEADS))) for i in range(N_HEADS)]
N_SLOPE_PIECES = 3
POS_SPLIT = 64
BAND_Q_BLOCKS = 1
BAND_AHEAD = 4
SELECT_DOT_COLS = 2048


def _params(*sem):
    return pltpu.CompilerParams(dimension_semantics=sem, vmem_limit_bytes=VMEM_LIMIT)


def _mm_kernel(x_ref, w_ref, sc_ref, o_ref, wb_ref, *acc, dil):
    @pl.when(pl.program_id(1) == 0)
    def _():
        wb_ref[...] = (w_ref[...] * sc_ref[...]).astype(jnp.bfloat16)

    y = jnp.dot(x_ref[...].astype(jnp.bfloat16), wb_ref[...], preferred_element_type=jnp.float32)
    if dil == 1:
        o_ref[...] = y.astype(o_ref.dtype)
    else:
        acc_ref, = acc
        per = acc_ref.shape[1] // dil
        for c in range(acc_ref.shape[0]):
            cols = slice(c * LANES, (c + 1) * LANES)
            acc_ref[c] = y[:, cols]
            for r in range(dil):
                o_ref[r, :, cols] = acc_ref[c, pl.ds(r, per, stride=dil), :].astype(o_ref.dtype)


def _matmul(x, w, layer, col0, scale, out_dtype, tm, tn, batch=1, dil=1):
    m, k = x.shape
    n = scale.shape[0]
    tm, tn = min(tm, m // batch), min(tn, n)
    assert m % (batch * tm) == 0 and n % tn == 0 and col0 % tn == 0 and tm % (16 * dil) == 0
    j0 = col0 // tn
    if dil == 1:
        out_specs = pl.BlockSpec((tm, tn), lambda j, i: (i, j))
        out_shape = jax.ShapeDtypeStruct((m, n), out_dtype)
        scratch = []
    else:
        per_batch = m // batch // tm
        out_specs = pl.BlockSpec((None, dil, tm // dil, tn),
                                 lambda j, i: (i // per_batch, 0, i % per_batch, j))
        out_shape = jax.ShapeDtypeStruct((batch, dil, m // batch // dil, n), out_dtype)
        scratch = [pltpu.VMEM((tn // LANES, tm, LANES), jnp.float32)]
    return pl.pallas_call(
        functools.partial(_mm_kernel, dil=dil),
        grid=(n // tn, m // tm),
        in_specs=[pl.BlockSpec((tm, k), lambda j, i: (i, 0)),
                  pl.BlockSpec((None, k, tn), lambda j, i: (layer, 0, j0 + j)),
                  pl.BlockSpec((1, tn), lambda j, i: (0, j))],
        out_specs=out_specs,
        out_shape=out_shape,
        scratch_shapes=[pltpu.VMEM((k, tn), jnp.bfloat16)] + scratch,
        compiler_params=_params("parallel", "arbitrary"),
        name="proj_matmul",
    )(x, w, jnp.asarray(scale, jnp.float32).reshape(1, n))


def _gate_up_kernel(x_ref, wg_ref, wu_ref, o_ref, wgb_ref, wub_ref):
    @pl.when(pl.program_id(1) == 0)
    def _():
        wgb_ref[...] = wg_ref[...].astype(jnp.bfloat16)
        wub_ref[...] = wu_ref[...].astype(jnp.bfloat16)

    x = x_ref[...]
    g = jnp.dot(x, wgb_ref[...], preferred_element_type=jnp.float32)
    u = jnp.dot(x, wub_ref[...], preferred_element_type=jnp.float32)
    o_ref[...] = (g * (1.0 / (1.0 + jnp.exp(-g))) * u).astype(o_ref.dtype)


def _ffn_gate_up(x, w, layer, tm, tn):
    m, k = x.shape
    d_ff = w.shape[2] // 2
    tm = min(tm, m)
    assert m % tm == 0 and d_ff % tn == 0
    nj = d_ff // tn
    return pl.pallas_call(
        _gate_up_kernel,
        grid=(nj, m // tm),
        in_specs=[pl.BlockSpec((tm, k), lambda j, i: (i, 0)),
                  pl.BlockSpec((None, k, tn), lambda j, i: (layer, 0, j)),
                  pl.BlockSpec((None, k, tn), lambda j, i: (layer, 0, j + nj))],
        out_specs=pl.BlockSpec((tm, tn), lambda j, i: (i, j)),
        out_shape=jax.ShapeDtypeStruct((m, d_ff), jnp.bfloat16),
        scratch_shapes=[pltpu.VMEM((k, tn), jnp.bfloat16), pltpu.VMEM((k, tn), jnp.bfloat16)],
        compiler_params=_params("parallel", "arbitrary"),
        name="ffn_gate_up",
    )(x, w, w)


def _proj_ln_kernel(a_ref, w_ref, r_ref, g_ref, b_ref, o_ref, ob_ref, wb_ref, *, n_w, tkc, chunk):
    s = pl.program_id(0)

    @pl.when(s < n_w)
    def _():
        wb_ref[pl.ds(pl.multiple_of(s * tkc, tkc), tkc), :] = w_ref[...].astype(jnp.bfloat16)

    @pl.when(s >= n_w)
    def _():
        pieces = [slice(c, c + chunk) for c in range(0, a_ref.shape[0], chunk)]
        ys = [jnp.dot(a_ref[rows, :], wb_ref[...], preferred_element_type=jnp.float32) for rows in pieces]
        for rows, y in zip(pieces, ys):
            z = DEEPNORM_ALPHA * r_ref[rows, :] + y
            mu = jnp.mean(z, axis=-1, keepdims=True)
            zc = z - mu
            var = jnp.mean(zc * zc, axis=-1, keepdims=True)
            out = zc * lax.rsqrt(var + LN_EPS) * g_ref[...] + b_ref[...]
            o_ref[rows, :] = out
            ob_ref[rows, :] = out.astype(jnp.bfloat16)


def _proj_residual_ln(a, w, layer, resid, g, b, tm, tkc, chunk):
    m, k = a.shape
    n = w.shape[2]
    tm, tkc = min(tm, m), min(tkc, k)
    chunk = min(chunk, tm)
    assert m % tm == 0 and k % tkc == 0 and tm % chunk == 0
    n_w = k // tkc
    row = lambda s: (jnp.maximum(s - n_w, 0), 0)
    return pl.pallas_call(
        functools.partial(_proj_ln_kernel, n_w=n_w, tkc=tkc, chunk=chunk),
        grid=(n_w + m // tm,),
        in_specs=[pl.BlockSpec((tm, k), row),
                  pl.BlockSpec((None, tkc, n), lambda s: (layer, jnp.minimum(s, n_w - 1), 0)),
                  pl.BlockSpec((tm, n), row),
                  pl.BlockSpec((1, n), lambda s: (0, 0)),
                  pl.BlockSpec((1, n), lambda s: (0, 0))],
        out_specs=[pl.BlockSpec((tm, n), row), pl.BlockSpec((tm, n), row)],
        out_shape=[jax.ShapeDtypeStruct((m, n), jnp.float32),
                   jax.ShapeDtypeStruct((m, n), jnp.bfloat16)],
        scratch_shapes=[pltpu.VMEM((k, n), jnp.bfloat16)],
        compiler_params=_params("arbitrary"),
        name="proj_residual_ln",
    )(a, w, resid, g.reshape(1, n), b.reshape(1, n))


def _bf16_pieces(x, n):
    rest = np.asarray(x, np.float64)
    pieces = []
    for _ in range(n):
        p = rest.astype(np.float32).astype(jnp.bfloat16).astype(np.float64)
        pieces.append(p)
        rest = rest - p
    return pieces


def _slope_columns(mult, n_parts):
    out = np.zeros((N_KV, GQA_GROUP * BLOCK, LANES), np.float32)
    for h in range(N_HEADS):
        pieces = [float(p) for p in _bf16_pieces(np.float64(_SLOPES[h]) * mult * LOG2E, N_SLOPE_PIECES)]
        kv, g = divmod(h, GQA_GROUP)
        out[kv, g * BLOCK:(g + 1) * BLOCK, HEAD_DIM:HEAD_DIM + n_parts * N_SLOPE_PIECES] = np.asarray(
            pieces * n_parts, np.float32)
    return jnp.asarray(out, jnp.bfloat16)


def _band_q_columns(dilation):
    out = np.zeros((N_KV, GQA_GROUP * BLOCK, LANES), np.float32)
    qpos = BLOCK + np.arange(BLOCK, dtype=np.float64)
    for h in range(N_HEADS):
        s2 = np.float64(_SLOPES[h]) * dilation * LOG2E
        kv, g = divmod(h, GQA_GROUP)
        rows = slice(g * BLOCK, (g + 1) * BLOCK)
        for c, p in enumerate(_bf16_pieces(s2, N_SLOPE_PIECES)):
            out[kv, rows, HEAD_DIM + c] = p
        for c, p in enumerate(_bf16_pieces(-s2 * qpos, N_SLOPE_PIECES)):
            out[kv, rows, HEAD_DIM + N_SLOPE_PIECES + c] = p
    return jnp.asarray(out, jnp.bfloat16)


def _band_k_columns():
    out = np.zeros((2 * BLOCK, LANES), np.float32)
    out[:, HEAD_DIM:HEAD_DIM + N_SLOPE_PIECES] = np.arange(2 * BLOCK, dtype=np.float32)[:, None]
    out[:, HEAD_DIM + N_SLOPE_PIECES:HEAD_DIM + 2 * N_SLOPE_PIECES] = 1.0
    return jnp.asarray(out, jnp.bfloat16)


def _band_kernel(*refs, max_j, has_sink, ahead):
    if has_sink:
        sink_ref, q_ref, kp_ref, kc_ref, vp_ref, vc_ref, qc_ref, kc_cols_ref, o_ref = refs
    else:
        q_ref, kp_ref, kc_ref, vp_ref, vc_ref, qc_ref, kc_cols_ref, o_ref, lse_ref = refs
    i = pl.program_id(1)
    n_qb = q_ref.shape[0] // BLOCK
    k = jnp.concatenate([kp_ref[...], kc_ref[...]], axis=0)
    v = jnp.concatenate([vp_ref[...], vc_ref[...]], axis=0)
    qpos = lax.broadcasted_iota(jnp.int32, (BLOCK, 2 * BLOCK), 0) + BLOCK
    kpos = lax.broadcasted_iota(jnp.int32, (BLOCK, 2 * BLOCK), 1)
    j = qpos - kpos
    in_band = (j >= 0) & (j <= max_j)
    bias_rest = jnp.where(in_band, 0.0, MASKED)
    bias_first = jnp.where(in_band & ((kpos >= BLOCK) | (i > 0)), 0.0, MASKED)
    low = lax.broadcasted_iota(jnp.int32, (BLOCK, LANES), 1) < HEAD_DIM
    k_cols = kc_cols_ref[:, HEAD_DIM:]
    ones64 = jnp.ones((2 * BLOCK, HEAD_DIM), jnp.bfloat16)
    ones128 = jnp.ones((2 * BLOCK, LANES), jnp.bfloat16)
    units = [(b, kv) for b in range(n_qb) for kv in range(N_KV)]

    def scores(b, kv):
        ka = jnp.concatenate([k[b * BLOCK:(b + 2) * BLOCK, kv * HEAD_DIM:(kv + 1) * HEAD_DIM], k_cols],
                             axis=1)
        qa = jnp.concatenate(
            [jnp.concatenate([q_ref[b * BLOCK:(b + 1) * BLOCK,
                                    (kv * GQA_GROUP + g) * HEAD_DIM:(kv * GQA_GROUP + g + 1) * HEAD_DIM]
                              for g in range(GQA_GROUP)], axis=0),
             qc_ref[kv][:, HEAD_DIM:]], axis=1)
        return lax.dot_general(qa, ka, (((1,), (1,)), ((), ())),
                               preferred_element_type=jnp.float32)

    pending = [scores(*units[u]) for u in range(min(ahead, len(units)))]
    for u, (b, kv) in enumerate(units):
        s_all = pending.pop(0)
        if u + ahead < len(units):
            pending.append(scores(*units[u + ahead]))
        h0 = kv * GQA_GROUP
        bias = bias_first if b == 0 else bias_rest
        out_rows = slice(b * BLOCK, (b + 1) * BLOCK)
        v_kv = v[b * BLOCK:(b + 2) * BLOCK, kv * HEAD_DIM:(kv + 1) * HEAD_DIM]
        v_even = jnp.concatenate([v_kv, ones64, ones128], axis=1)
        v_odd = jnp.concatenate([ones64, v_kv, ones128], axis=1)
        ps, ms = [], []
        for g in range(GQA_GROUP):
            s = s_all[g * BLOCK:(g + 1) * BLOCK] + bias
            m = jnp.max(s, axis=-1, keepdims=True)
            if has_sink:
                m = jnp.maximum(m, sink_ref[h0 + g] * LOG2E)
            ps.append(jnp.exp2(s - m).astype(jnp.bfloat16))
            ms.append(m)
        acc_e = jnp.dot(jnp.concatenate(ps[0::2], axis=0), v_even,
                        preferred_element_type=jnp.float32)
        acc_o = jnp.dot(jnp.concatenate(ps[1::2], axis=0), v_odd,
                        preferred_element_type=jnp.float32)
        for t in range(GQA_GROUP // 2):
            h = h0 + 2 * t
            rows = slice(t * BLOCK, (t + 1) * BLOCK)
            num = jnp.where(low, acc_e[rows, :LANES], acc_o[rows, :LANES])
            den = jnp.where(low, acc_e[rows, LANES:], acc_o[rows, LANES:])
            m2 = jnp.where(low, jnp.broadcast_to(ms[2 * t], (BLOCK, LANES)),
                           jnp.broadcast_to(ms[2 * t + 1], (BLOCK, LANES)))
            if has_sink:
                sink2 = jnp.where(low, sink_ref[h] * LOG2E, sink_ref[h + 1] * LOG2E)
                den = den + jnp.exp2(sink2 - m2)
            cols = slice(h * HEAD_DIM, (h + 2) * HEAD_DIM)
            o_ref[out_rows, cols] = (num / den).astype(o_ref.dtype)
            if not has_sink:
                lse_ref[out_rows, cols] = m2 * (1.0 / LOG2E) + jnp.log(den)


def _band_attention(h, max_j, dilation, sinks=None):
    r, l, _ = h.shape
    n_qb = min(BAND_Q_BLOCKS, l // BLOCK)
    tq = n_qb * BLOCK
    assert l % tq == 0
    kcol, vcol = Q_DIM // KV_DIM, Q_DIM // KV_DIM + 1
    has_sink = sinks is not None
    rows = GQA_GROUP * BLOCK
    prev = lambda s, i, col: (s, jnp.maximum(i * n_qb - 1, 0), col)
    in_specs = [pl.BlockSpec((None, tq, Q_DIM), lambda s, i: (s, i, 0)),
                pl.BlockSpec((None, BLOCK, KV_DIM), lambda s, i: prev(s, i, kcol)),
                pl.BlockSpec((None, tq, KV_DIM), lambda s, i: (s, i, kcol)),
                pl.BlockSpec((None, BLOCK, KV_DIM), lambda s, i: prev(s, i, vcol)),
                pl.BlockSpec((None, tq, KV_DIM), lambda s, i: (s, i, vcol)),
                pl.BlockSpec((N_KV, rows, LANES), lambda s, i: (0, 0, 0)),
                pl.BlockSpec((2 * BLOCK, LANES), lambda s, i: (0, 0))]
    o_spec = pl.BlockSpec((None, tq, Q_DIM), lambda s, i: (s, i, 0))
    args = [h, h, h, h, h, _band_q_columns(dilation), _band_k_columns()]
    if has_sink:
        in_specs = [pl.BlockSpec(memory_space=pltpu.SMEM)] + in_specs
        args = [sinks.astype(jnp.float32)] + args
        out_specs = o_spec
        out_shape = jax.ShapeDtypeStruct((r, l, Q_DIM), jnp.bfloat16)
    else:
        out_specs = [o_spec, o_spec]
        out_shape = [jax.ShapeDtypeStruct((r, l, Q_DIM), jnp.bfloat16),
                     jax.ShapeDtypeStruct((r, l, Q_DIM), jnp.float32)]
    return pl.pallas_call(
        functools.partial(_band_kernel, max_j=max_j, has_sink=has_sink, ahead=BAND_AHEAD),
        grid=(r, l // tq),
        in_specs=in_specs,
        out_specs=out_specs,
        out_shape=out_shape,
        compiler_params=_params("parallel", "parallel"),
        name="band_attention",
    )(*args)


def _merge_kernel(*refs, dils):
    n_g = len(dils)
    o_refs, l_refs, out_ref, bufs = refs[:n_g], refs[n_g:2 * n_g], refs[2 * n_g], list(refs[2 * n_g + 1:])

    def natural(ref, d):
        if d == 1:
            return ref[0].astype(jnp.float32)
        buf = bufs.pop()
        per = ref.shape[1]
        for c in range(buf.shape[0]):
            for r in range(d):
                buf[c, pl.ds(r, per, stride=d), :] = ref[r, :, c * LANES:(c + 1) * LANES].astype(jnp.float32)
        return jnp.concatenate([buf[c] for c in range(buf.shape[0])], axis=1)

    os = [natural(ref, d) for ref, d in zip(o_refs, dils)]
    ls = [natural(ref, d) for ref, d in zip(l_refs, dils)]
    m = functools.reduce(jnp.maximum, ls)
    es = [jnp.exp(l - m) for l in ls]
    num = sum(e * o for e, o in zip(es, os))
    out_ref[...] = (num / sum(es)).astype(out_ref.dtype)


def _merge_groups(outs, lses, dils, tm):
    b, _, s, n = outs[0].shape
    tm = min(tm, s)
    assert s % tm == 0 and all(tm % (8 * d) == 0 for d in dils)
    per_batch = s // tm
    specs = [pl.BlockSpec((None, d, tm // d, n), lambda i: (i // per_batch, 0, i % per_batch, 0))
             for d in dils]
    n_buf = 2 * sum(d > 1 for d in dils)
    return pl.pallas_call(
        functools.partial(_merge_kernel, dils=tuple(dils)),
        grid=(b * per_batch,),
        in_specs=specs + specs,
        out_specs=pl.BlockSpec((tm, n), lambda i: (i, 0)),
        out_shape=jax.ShapeDtypeStruct((b * s, n), jnp.bfloat16),
        scratch_shapes=[pltpu.VMEM((n // LANES, tm, LANES), jnp.float32)] * n_buf,
        compiler_params=_params("parallel"),
        name="merge_groups",
    )(*outs, *lses)


def _count_rows(n_tiles, tile_hits, tq, tk):
    rows = 32
    def body(t, acc):
        hit = tile_hits(pl.multiple_of(t * tk, tk)).astype(jnp.float32)
        return acc + hit.reshape(tk // rows, rows, tq).sum(axis=0)
    acc = lax.fori_loop(0, n_tiles, body, jnp.zeros((rows, tq), jnp.float32))
    return jnp.sum(acc, axis=0, keepdims=True)


def _dsa_select_kernel(qia_ref, qib_ref, ki_ref, wi_ref, bias_ref, key_ref, *, top_k, tq, tk):
    i = pl.program_id(1)
    s_len = key_ref.shape[0]
    n_t = ((i + 1) * tq + tk - 1) // tk
    qpos = lax.broadcasted_iota(jnp.int32, (1, tq), 1) + i * tq
    kloc = lax.broadcasted_iota(jnp.int32, (tk, tq), 0)

    half = IDX_HEADS // 2
    q_t = [ref[...].astype(jnp.float32).T.astype(jnp.bfloat16) for ref in (qia_ref, qib_ref)]
    hpd = SELECT_DOT_COLS // tq
    q_cat = [jnp.concatenate([q[h * IDX_DIM:(h + 1) * IDX_DIM, :] for h in range(h0, h0 + hpd)], axis=1)
             for q in q_t for h0 in range(0, half, hpd)]
    w_t = wi_ref[...].T

    def score_body(t, carry):
        start = pl.multiple_of(t * tk, tk)
        ki = ki_ref[pl.ds(start, tk), :][:, :IDX_DIM].astype(jnp.bfloat16)
        rels = [jnp.dot(ki, q, preferred_element_type=jnp.float32) for q in q_cat]
        sc = jnp.zeros((tk, tq), jnp.float32)
        for d, rel in enumerate(rels):
            for g in range(hpd):
                row = IDX_DIM + d * hpd + g
                sc = sc + jnp.maximum(rel[:, g * tq:(g + 1) * tq], 0.0) * w_t[row:row + 1, :]
        sc = sc + 0.0
        bits = pltpu.bitcast(sc, jnp.int32)
        keys = jnp.where(bits < 0, bits ^ jnp.int32(0x7FFFFFFF), bits)
        key_ref[pl.ds(start, tk), :] = jnp.where(kloc + start <= qpos, keys, jnp.int32(INT_MIN))
        return carry

    lax.fori_loop(0, n_t, score_body, 0)

    kf = jnp.float32(top_k)

    def count_ge(cand):
        return _count_rows(n_t, lambda st: key_ref[pl.ds(st, tk), :] >= cand, tq, tk)

    short = (qpos + 1 < top_k).astype(jnp.float32)

    def bit_cond(c):
        b, _, done, _ = c
        return (b < 32) & (jnp.min(done) < 0.5)

    def bit_body(c):
        b, prefix, done, thr = c
        cand = prefix ^ lax.shift_left(jnp.int32(1), jnp.int32(31) - b)
        cnt = count_ge(cand)
        prefix = jnp.where(cnt >= kf, cand, prefix)
        hit = (cnt == kf) & (done < 0.5)
        thr = jnp.where(hit, cand - 1, thr)
        done = jnp.where(hit, 1.0, done)
        return b + 1, prefix, done, thr

    init = (jnp.int32(0), jnp.full((1, tq), INT_MIN, jnp.int32), short,
            jnp.full((1, tq), INT_MIN, jnp.int32))
    _, prefix, done, thr = lax.while_loop(bit_cond, bit_body, init)
    is_done = done > 0.5
    thr = jnp.where(is_done, thr, prefix)

    def tie_search(_):
        n_gt = _count_rows(n_t, lambda st: key_ref[pl.ds(st, tk), :] > thr, tq, tk)
        need = kf - n_gt
        n_idx_bits = max(1, int(math.ceil(math.log2(s_len))))

        def idx_body(b, p):
            cand = p | lax.shift_left(jnp.int32(1), jnp.int32(n_idx_bits - 1) - b)
            below = _count_rows(
                n_t, lambda st: (key_ref[pl.ds(st, tk), :] == thr) & (kloc + st < cand), tq, tk)
            return jnp.where(below <= need - 1.0, cand, p)

        return lax.fori_loop(0, n_idx_bits, idx_body, jnp.zeros((1, tq), jnp.int32))

    last_tie = lax.cond(jnp.min(done) < 0.5, tie_search,
                        lambda _: jnp.full((1, tq), -1, jnp.int32), 0)
    last_tie = jnp.where(is_done, -1, last_tie)

    def write_body(t, carry):
        start = pl.multiple_of(t * tk, tk)
        kt = key_ref[pl.ds(start, tk), :]
        pos = kloc + start
        sel = ((kt > thr) | ((kt == thr) & (pos <= last_tie))) & (pos <= qpos)
        bias_ref[:, pl.ds(start, tk)] = jnp.where(sel, 0.0, MASKED).T.astype(bias_ref.dtype)
        return carry

    lax.fori_loop(0, n_t, write_body, 0)

    def fill_body(t, carry):
        bias_ref[:, pl.ds(pl.multiple_of(t * tk, tk), tk)] = jnp.full((tq, tk), MASKED, bias_ref.dtype)
        return carry

    lax.fori_loop(n_t, s_len // tk, fill_body, 0)


def _dsa_select(hm, kiwi, tq, tk):
    b, s, _ = hm.shape
    tq, tk = min(tq, s), min(tk, s)
    assert s % tk == 0 and s % tq == 0
    top_k = min(TOPK_MAX, s // 4)
    qi_half = IDX_HEADS * IDX_DIM // 2
    qicol = GRP_IN // qi_half
    return pl.pallas_call(
        functools.partial(_dsa_select_kernel, top_k=top_k, tq=tq, tk=tk),
        grid=(b, s // tq),
        in_specs=[pl.BlockSpec((None, tq, qi_half), lambda bb, i: (bb, i, qicol)),
                  pl.BlockSpec((None, tq, qi_half), lambda bb, i: (bb, i, qicol + 1)),
                  pl.BlockSpec((None, s, LANES), lambda bb, i: (bb, 0, 0)),
                  pl.BlockSpec((None, tq, LANES), lambda bb, i: (bb, i, 0))],
        out_specs=pl.BlockSpec((None, tq, s), lambda bb, i: (bb, i, 0)),
        out_shape=jax.ShapeDtypeStruct((b, s, s), jnp.bfloat16),
        scratch_shapes=[pltpu.VMEM((s, tq), jnp.int32)],
        compiler_params=_params("parallel", "parallel"),
        name="dsa_select",
    )(hm, hm, kiwi, kiwi)


def _dsa_attn_kernel(q_ref, k_ref, v_ref, bias_ref, sl_ref, o_ref,
                     kaug_ref, vaug_ref, m_ref, acc_ref, *, tk, chunk):
    i = pl.program_id(1)
    s_len = k_ref.shape[0]

    @pl.when(i == 0)
    def _():
        lane = lax.broadcasted_iota(jnp.int32, (chunk, HEAD_DIM), 1)
        ones_col = (lane == 0).astype(jnp.bfloat16)
        for c in range(s_len // chunk):
            pos = lax.broadcasted_iota(jnp.int32, (chunk, HEAD_DIM), 0) + c * chunk
            hi = pos - (pos & (POS_SPLIT - 1))
            lo = pos & (POS_SPLIT - 1)
            pcols = jnp.where(lane < N_SLOPE_PIECES, hi, jnp.where(lane < 2 * N_SLOPE_PIECES, lo, 0))
            pcols = pcols.astype(jnp.float32).astype(jnp.bfloat16)
            rows = pl.ds(c * chunk, chunk)
            for kv in range(N_KV):
                kaug_ref[kv, rows, :] = jnp.concatenate(
                    [k_ref[rows, kv * HEAD_DIM:(kv + 1) * HEAD_DIM], pcols], axis=1)
                vaug_ref[kv, rows, :] = jnp.concatenate(
                    [v_ref[rows, kv * HEAD_DIM:(kv + 1) * HEAD_DIM], ones_col], axis=1)

    n_keys = (i + 1) * BLOCK
    m_ref[...] = jnp.full(m_ref.shape, MASKED, jnp.float32)
    acc_ref[...] = jnp.zeros(acc_ref.shape, jnp.float32)
    qa = [jnp.concatenate(
        [jnp.concatenate(
            [q_ref[:, (kv * GQA_GROUP + g) * HEAD_DIM:(kv * GQA_GROUP + g + 1) * HEAD_DIM]
             for g in range(GQA_GROUP)], axis=0), sl_ref[kv][:, HEAD_DIM:]], axis=1)
        for kv in range(N_KV)]

    def key_tile(t, width):
        start = pl.multiple_of(t * width, width)
        bias = bias_ref[:, pl.ds(start, width)].astype(jnp.float32)
        scores = [lax.dot_general(qa[kv], kaug_ref[kv, pl.ds(start, width), :], (((1,), (1,)), ((), ())),
                                  preferred_element_type=jnp.float32) for kv in range(N_KV)]
        for kv in range(N_KV):
            s_all = scores[kv]
            m_old = m_ref[kv]
            ps, m_news = [], []
            for g in range(GQA_GROUP):
                s = s_all[g * BLOCK:(g + 1) * BLOCK] + bias
                m_new = jnp.maximum(m_old[g * BLOCK:(g + 1) * BLOCK],
                                    jnp.max(s, axis=-1, keepdims=True))
                ps.append(jnp.exp2(s - jnp.concatenate([m_new] * (width // LANES), axis=1))
                          .astype(jnp.bfloat16))
                m_news.append(m_new)
            m_new = jnp.concatenate(m_news, axis=0)
            pv = jnp.dot(jnp.concatenate(ps, axis=0), vaug_ref[kv, pl.ds(start, width), :],
                         preferred_element_type=jnp.float32)
            acc_ref[kv] = jnp.exp2(m_old - m_new) * acc_ref[kv] + pv
            m_ref[kv] = m_new

    def full_tile(t, carry):
        key_tile(t, tk)
        return carry

    n_full = n_keys // tk
    rest = n_keys - n_full * tk
    lax.fori_loop(0, n_full, full_tile, 0)
    half = tk // 2

    @pl.when(rest > half)
    def _():
        key_tile(n_full, tk)

    @pl.when((rest > 0) & (rest <= half))
    def _():
        key_tile(2 * n_full, half)

    for kv in range(N_KV):
        acc = acc_ref[kv]
        o_all = (acc[:, :HEAD_DIM] / acc[:, HEAD_DIM:HEAD_DIM + 1]).astype(o_ref.dtype)
        for g in range(0, GQA_GROUP, 2):
            h = kv * GQA_GROUP + g
            o_ref[:, h * HEAD_DIM:(h + 2) * HEAD_DIM] = jnp.concatenate(
                [o_all[g * BLOCK:(g + 1) * BLOCK], o_all[(g + 1) * BLOCK:(g + 2) * BLOCK]], axis=1)


def _dsa_attention(hm, bias, tk):
    b, s, _ = hm.shape
    tk = min(tk, s)
    chunk = min(512, s)
    kcol = Q_DIM // KV_DIM
    rows = GQA_GROUP * BLOCK
    return pl.pallas_call(
        functools.partial(_dsa_attn_kernel, tk=tk, chunk=chunk),
        grid=(b, s // BLOCK),
        in_specs=[pl.BlockSpec((None, BLOCK, Q_DIM), lambda bb, i: (bb, i, 0)),
                  pl.BlockSpec((None, s, KV_DIM), lambda bb, i: (bb, 0, kcol)),
                  pl.BlockSpec((None, s, KV_DIM), lambda bb, i: (bb, 0, kcol + 1)),
                  pl.BlockSpec((None, BLOCK, s), lambda bb, i: (bb, i, 0)),
                  pl.BlockSpec((N_KV, rows, LANES), lambda bb, i: (0, 0, 0))],
        out_specs=pl.BlockSpec((None, BLOCK, Q_DIM), lambda bb, i: (bb, i, 0)),
        out_shape=jax.ShapeDtypeStruct((b, s, Q_DIM), jnp.bfloat16),
        scratch_shapes=[pltpu.VMEM((N_KV, s, LANES), jnp.bfloat16),
                        pltpu.VMEM((N_KV, s, LANES), jnp.bfloat16),
                        pltpu.VMEM((N_KV, rows, LANES), jnp.float32),
                        pltpu.VMEM((N_KV, rows, LANES), jnp.float32)],
        compiler_params=_params("parallel", "arbitrary"),
        name="dsa_attention",
    )(hm, hm, hm, bias, _slope_columns(1, 2))


_QKV_SCALE = np.concatenate([np.full(Q_DIM, ATTN_SCALE * LOG2E), np.ones(2 * KV_DIM)])
_DSA_SCALE = np.concatenate([_QKV_SCALE, np.full(IDX_HEADS * IDX_DIM, IDX_DIM ** -0.5)])


def _swa_layer(xb, w_in, layer, sinks, b, s):
    h = _matmul(xb, w_in, layer, 0, _QKV_SCALE, jnp.bfloat16, 1024, 1280)
    o = _band_attention(h.reshape(b, s, GRP_IN), SWA_WINDOW - 1, 1, sinks)
    return o.reshape(b * s, Q_DIM)


def _dilated_layer(xb, w_in, layer, b, s):
    outs, lses, dils = [], [], []
    for g, (window, dil) in enumerate(DIL_PATTERNS):
        h = _matmul(xb, w_in, layer, g * GRP_IN, _QKV_SCALE, jnp.bfloat16, 1024, 1280, batch=b, dil=dil)
        o, lse = _band_attention(h.reshape(b * dil, s // dil, GRP_IN), window // dil, dil)
        outs.append(o.reshape(b, dil, s // dil, Q_DIM))
        lses.append(lse.reshape(b, dil, s // dil, Q_DIM))
        dils.append(dil)
    return _merge_groups(outs, lses, dils, 256)


def _dsa_layer(xb, w_in, layer, b, s):
    hm = _matmul(xb, w_in, layer, 0, _DSA_SCALE, jnp.bfloat16, 2048, 512).reshape(b, s, C_MAIN)
    w_small = jnp.concatenate([w_in[layer, :, C_MAIN:C_MAIN + IDX_DIM],
                               w_in[layer, :, C_MAIN + IDX_DIM:] * IDX_HEADS ** -0.5,
                               jnp.zeros((D_MODEL, LANES - IDX_DIM - IDX_HEADS), w_in.dtype)], axis=1)
    kiwi = _matmul(xb, w_small[None], 0, 0, np.ones(LANES), jnp.float32, 1024, LANES).reshape(b, s, LANES)
    bias = _dsa_select(hm, kiwi, 512, 512)
    o = _dsa_attention(hm, bias, 512)
    return o.reshape(b * s, Q_DIM)


def kernel(x, a_w_in, a_sinks, a_w_out, b_w_in, b_w_out, c_w_in, c_w_out, ln_g, ln_b,
           ffn_w_gate_up, ffn_w_down):
    b, s, d = x.shape
    xf = x.reshape(b * s, d)
    xb = xf
    for i in range(DEPTH):
        kind, j = i % N_MIXERS, i // N_MIXERS
        if kind == 0:
            o = _swa_layer(xb, a_w_in, j, a_sinks[j], b, s)
            w_out = a_w_out
        elif kind == 1:
            o = _dilated_layer(xb, b_w_in, j, b, s)
            w_out = b_w_out
        else:
            o = _dsa_layer(xb, c_w_in, j, b, s)
            w_out = c_w_out
        xf, xb = _proj_residual_ln(o, w_out, j, xf, ln_g[i, 0], ln_b[i, 0], 512, 512, 256)
        hmid = _ffn_gate_up(xb, ffn_w_gate_up, i, 1024, 512)
        xf, xb = _proj_residual_ln(hmid, ffn_w_down, i, xf, ln_g[i, 1], ln_b[i, 1], 256, 512, 256)
    return xf.reshape(b, s, d)
```

```python
import functools
import math

import numpy as np
import jax
import jax.numpy as jnp
from jax import lax
from jax.experimental import pallas as pl
from jax.experimental.pallas import tpu as pltpu

D_MODEL = 2048
DEPTH = 4
N_MIXERS = 3
HEAD_DIM = 64
N_HEADS = D_MODEL // HEAD_DIM
N_KV = N_HEADS // 8
GQA_GROUP = N_HEADS // N_KV
Q_DIM = N_HEADS * HEAD_DIM
KV_DIM = N_KV * HEAD_DIM
ATTN_SCALE = HEAD_DIM ** -0.5
BLOCK = 128
SWA_WINDOW = 128
DIL_PATTERNS = ((128, 1), (512, 4), (2048, 16))
N_DIL = len(DIL_PATTERNS)
IDX_HEADS = 16
IDX_DIM = 64
TOPK_MAX = 256
D_FF = 256 * math.ceil(8 * D_MODEL / (3 * 256))
DEEPNORM_ALPHA = (2 * DEPTH) ** 0.25
LN_EPS = 1e-5
GRP_IN = Q_DIM + 2 * KV_DIM
C_MAIN = Q_DIM + 2 * KV_DIM + IDX_HEADS * IDX_DIM
C_IN = C_MAIN + IDX_DIM + IDX_HEADS
LANES = 128
MASKED = -1e30
INT_MIN = -2 ** 31
LOG2E = 1.4426950408889634
VMEM_LIMIT = 56 * 1024 * 1024

_SLOPES = [float(np.float32(2.0 ** (-8.0 * (i + 1) / N_HEADS))) for i in range(N_HEADS)]
N_SLOPE_PIECES = 3
POS_SPLIT = 64
BAND_Q_BLOCKS = 1
BAND_AHEAD = 4
SELECT_DOT_COLS = 2048


def _params(*sem):
    return pltpu.CompilerParams(dimension_semantics=sem, vmem_limit_bytes=VMEM_LIMIT)


def _mm_kernel(x_ref, w_ref, sc_ref, o_ref, wb_ref, *acc, dil):
    @pl.when(pl.program_id(1) == 0)
    def _():
        wb_ref[...] = (w_ref[...] * sc_ref[...]).astype(jnp.bfloat16)

    y = jnp.dot(x_ref[...].astype(jnp.bfloat16), wb_ref[...], preferred_element_type=jnp.float32)
    if dil == 1:
        o_ref[...] = y.astype(o_ref.dtype)
    else:
        acc_ref, = acc
        per = acc_ref.shape[1] // dil
        for c in range(acc_ref.shape[0]):
            cols = slice(c * LANES, (c + 1) * LANES)
            acc_ref[c] = y[:, cols]
            for r in range(dil):
                o_ref[r, :, cols] = acc_ref[c, pl.ds(r, per, stride=dil), :].astype(o_ref.dtype)


def _matmul(x, w, layer, col0, scale, out_dtype, tm, tn, batch=1, dil=1):
    m, k = x.shape
    n = scale.shape[0]
    tm, tn = min(tm, m // batch), min(tn, n)
    assert m % (batch * tm) == 0 and n % tn == 0 and col0 % tn == 0 and tm % (16 * dil) == 0
    j0 = col0 // tn
    if dil == 1:
        out_specs = pl.BlockSpec((tm, tn), lambda j, i: (i, j))
        out_shape = jax.ShapeDtypeStruct((m, n), out_dtype)
        scratch = []
    else:
        per_batch = m // batch // tm
        out_specs = pl.BlockSpec((None, dil, tm // dil, tn),
                                 lambda j, i: (i // per_batch, 0, i % per_batch, j))
        out_shape = jax.ShapeDtypeStruct((batch, dil, m // batch // dil, n), out_dtype)
        scratch = [pltpu.VMEM((tn // LANES, tm, LANES), jnp.float32)]
    return pl.pallas_call(
        functools.partial(_mm_kernel, dil=dil),
        grid=(n // tn, m // tm),
        in_specs=[pl.BlockSpec((tm, k), lambda j, i: (i, 0)),
                  pl.BlockSpec((None, k, tn), lambda j, i: (layer, 0, j0 + j)),
                  pl.BlockSpec((1, tn), lambda j, i: (0, j))],
        out_specs=out_specs,
        out_shape=out_shape,
        scratch_shapes=[pltpu.VMEM((k, tn), jnp.bfloat16)] + scratch,
        compiler_params=_params("parallel", "arbitrary"),
        name="proj_matmul",
    )(x, w, jnp.asarray(scale, jnp.float32).reshape(1, n))


def _gate_up_kernel(x_ref, wg_ref, wu_ref, o_ref, wgb_ref, wub_ref):
    @pl.when(pl.program_id(1) == 0)
    def _():
        wgb_ref[...] = wg_ref[...].astype(jnp.bfloat16)
        wub_ref[...] = wu_ref[...].astype(jnp.bfloat16)

    x = x_ref[...]
    g = jnp.dot(x, wgb_ref[...], preferred_element_type=jnp.float32)
    u = jnp.dot(x, wub_ref[...], preferred_element_type=jnp.float32)
    o_ref[...] = (g * (1.0 / (1.0 + jnp.exp(-g))) * u).astype(o_ref.dtype)


def _ffn_gate_up(x, w, layer, tm, tn):
    m, k = x.shape
    d_ff = w.shape[2] // 2
    tm = min(tm, m)
    assert m % tm == 0 and d_ff % tn == 0
    nj = d_ff // tn
    return pl.pallas_call(
        _gate_up_kernel,
        grid=(nj, m // tm),
        in_specs=[pl.BlockSpec((tm, k), lambda j, i: (i, 0)),
                  pl.BlockSpec((None, k, tn), lambda j, i: (layer, 0, j)),
                  pl.BlockSpec((None, k, tn), lambda j, i: (layer, 0, j + nj))],
        out_specs=pl.BlockSpec((tm, tn), lambda j, i: (i, j)),
        out_shape=jax.ShapeDtypeStruct((m, d_ff), jnp.bfloat16),
        scratch_shapes=[pltpu.VMEM((k, tn), jnp.bfloat16), pltpu.VMEM((k, tn), jnp.bfloat16)],
        compiler_params=_params("parallel", "arbitrary"),
        name="ffn_gate_up",
    )(x, w, w)


def _proj_ln_kernel(a_ref, a0_ref, w_ref, r_ref, g_ref, b_ref, o_ref, ob_ref, wb_ref, *, n_w, tkc, chunk):
    s = pl.program_id(0)
    pieces = [slice(c, c + chunk) for c in range(0, a_ref.shape[0], chunk)]

    def residual_layer_norm(rows, y):
        z = DEEPNORM_ALPHA * r_ref[rows, :] + y
        mu = jnp.mean(z, axis=-1, keepdims=True)
        zc = z - mu
        var = jnp.mean(zc * zc, axis=-1, keepdims=True)
        out = zc * lax.rsqrt(var + LN_EPS) * g_ref[...] + b_ref[...]
        o_ref[rows, :] = out
        ob_ref[rows, :] = out.astype(jnp.bfloat16)

    @pl.when(s < n_w)
    def _():
        wb = w_ref[...].astype(jnp.bfloat16)
        wb_ref[pl.ds(pl.multiple_of(s * tkc, tkc), tkc), :] = wb
        part = jnp.dot(a0_ref[...], wb, preferred_element_type=jnp.float32)

        @pl.when(s == 0)
        def _():
            o_ref[...] = part

        @pl.when(s > 0)
        def _():
            o_ref[...] += part

    @pl.when(s == n_w)
    def _():
        for rows in pieces:
            residual_layer_norm(rows, o_ref[rows, :])

    @pl.when(s > n_w)
    def _():
        ys = [jnp.dot(a_ref[rows, :], wb_ref[...], preferred_element_type=jnp.float32) for rows in pieces]
        for rows, y in zip(pieces, ys):
            residual_layer_norm(rows, y)


def _proj_residual_ln(a, w, layer, resid, g, b, tm, tkc, chunk):
    m, k = a.shape
    n = w.shape[2]
    tm, tkc = min(tm, m), min(tkc, k)
    chunk = min(chunk, tm)
    assert m % tm == 0 and m // tm >= 2 and k % tkc == 0 and tm % chunk == 0
    n_w = k // tkc
    row = lambda s: (jnp.maximum(s - n_w, 0), 0)
    w_chunk = lambda s: jnp.minimum(s, n_w - 1)
    return pl.pallas_call(
        functools.partial(_proj_ln_kernel, n_w=n_w, tkc=tkc, chunk=chunk),
        grid=(n_w + m // tm,),
        in_specs=[pl.BlockSpec((tm, k), lambda s: (jnp.maximum(s - n_w, 1), 0)),
                  pl.BlockSpec((tm, tkc), lambda s: (0, w_chunk(s))),
                  pl.BlockSpec((None, tkc, n), lambda s: (layer, w_chunk(s), 0)),
                  pl.BlockSpec((tm, n), row),
                  pl.BlockSpec((1, n), lambda s: (0, 0)),
                  pl.BlockSpec((1, n), lambda s: (0, 0))],
        out_specs=[pl.BlockSpec((tm, n), row), pl.BlockSpec((tm, n), row)],
        out_shape=[jax.ShapeDtypeStruct((m, n), jnp.float32),
                   jax.ShapeDtypeStruct((m, n), jnp.bfloat16)],
        scratch_shapes=[pltpu.VMEM((k, n), jnp.bfloat16)],
        compiler_params=_params("arbitrary"),
        name="proj_residual_ln",
    )(a, a, w, resid, g.reshape(1, n), b.reshape(1, n))


def _bf16_pieces(x, n):
    rest = np.asarray(x, np.float64)
    pieces = []
    for _ in range(n):
        p = rest.astype(np.float32).astype(jnp.bfloat16).astype(np.float64)
        pieces.append(p)
        rest = rest - p
    return pieces


def _slope_columns(mult, n_parts):
    out = np.zeros((N_KV, GQA_GROUP * BLOCK, LANES), np.float32)
    for h in range(N_HEADS):
        pieces = [float(p) for p in _bf16_pieces(np.float64(_SLOPES[h]) * mult * LOG2E, N_SLOPE_PIECES)]
        kv, g = divmod(h, GQA_GROUP)
        out[kv, g * BLOCK:(g + 1) * BLOCK, HEAD_DIM:HEAD_DIM + n_parts * N_SLOPE_PIECES] = np.asarray(
            pieces * n_parts, np.float32)
    return jnp.asarray(out, jnp.bfloat16)


def _band_q_columns(dilation):
    out = np.zeros((N_KV, GQA_GROUP * BLOCK, LANES), np.float32)
    qpos = BLOCK + np.arange(BLOCK, dtype=np.float64)
    for h in range(N_HEADS):
        s2 = np.float64(_SLOPES[h]) * dilation * LOG2E
        kv, g = divmod(h, GQA_GROUP)
        rows = slice(g * BLOCK, (g + 1) * BLOCK)
        for c, p in enumerate(_bf16_pieces(s2, N_SLOPE_PIECES)):
            out[kv, rows, HEAD_DIM + c] = p
        for c, p in enumerate(_bf16_pieces(-s2 * qpos, N_SLOPE_PIECES)):
            out[kv, rows, HEAD_DIM + N_SLOPE_PIECES + c] = p
    return jnp.asarray(out, jnp.bfloat16)


def _band_k_columns():
    out = np.zeros((2 * BLOCK, LANES), np.float32)
    out[:, HEAD_DIM:HEAD_DIM + N_SLOPE_PIECES] = np.arange(2 * BLOCK, dtype=np.float32)[:, None]
    out[:, HEAD_DIM + N_SLOPE_PIECES:HEAD_DIM + 2 * N_SLOPE_PIECES] = 1.0
    return jnp.asarray(out, jnp.bfloat16)


def _band_kernel(*refs, max_j, has_sink, ahead):
    if has_sink:
        sink_ref, q_ref, kp_ref, kc_ref, vp_ref, vc_ref, qc_ref, kc_cols_ref, o_ref = refs
    else:
        q_ref, kp_ref, kc_ref, vp_ref, vc_ref, qc_ref, kc_cols_ref, o_ref, lse_ref = refs
    i = pl.program_id(1)
    n_qb = q_ref.shape[0] // BLOCK
    k = jnp.concatenate([kp_ref[...], kc_ref[...]], axis=0)
    v = jnp.concatenate([vp_ref[...], vc_ref[...]], axis=0)
    qpos = lax.broadcasted_iota(jnp.int32, (BLOCK, 2 * BLOCK), 0) + BLOCK
    kpos = lax.broadcasted_iota(jnp.int32, (BLOCK, 2 * BLOCK), 1)
    j = qpos - kpos
    in_band = (j >= 0) & (j <= max_j)
    bias_rest = jnp.where(in_band, 0.0, MASKED)
    bias_first = jnp.where(in_band & ((kpos >= BLOCK) | (i > 0)), 0.0, MASKED)
    low = lax.broadcasted_iota(jnp.int32, (BLOCK, LANES), 1) < HEAD_DIM
    k_cols = kc_cols_ref[:, HEAD_DIM:]
    ones64 = jnp.ones((2 * BLOCK, HEAD_DIM), jnp.bfloat16)
    ones128 = jnp.ones((2 * BLOCK, LANES), jnp.bfloat16)
    units = [(b, kv) for b in range(n_qb) for kv in range(N_KV)]

    def scores(b, kv):
        ka = jnp.concatenate([k[b * BLOCK:(b + 2) * BLOCK, kv * HEAD_DIM:(kv + 1) * HEAD_DIM], k_cols],
                             axis=1)
        qa = jnp.concatenate(
            [jnp.concatenate([q_ref[b * BLOCK:(b + 1) * BLOCK,
                                    (kv * GQA_GROUP + g) * HEAD_DIM:(kv * GQA_GROUP + g + 1) * HEAD_DIM]
                              for g in range(GQA_GROUP)], axis=0),
             qc_ref[kv][:, HEAD_DIM:]], axis=1)
        return lax.dot_general(qa, ka, (((1,), (1,)), ((), ())),
                               preferred_element_type=jnp.float32)

    pending = [scores(*units[u]) for u in range(min(ahead, len(units)))]
    for u, (b, kv) in enumerate(units):
        s_all = pending.pop(0)
        if u + ahead < len(units):
            pending.append(scores(*units[u + ahead]))
        h0 = kv * GQA_GROUP
        bias = bias_first if b == 0 else bias_rest
        out_rows = slice(b * BLOCK, (b + 1) * BLOCK)
        v_kv = v[b * BLOCK:(b + 2) * BLOCK, kv * HEAD_DIM:(kv + 1) * HEAD_DIM]
        v_even = jnp.concatenate([v_kv, ones64, ones128], axis=1)
        v_odd = jnp.concatenate([ones64, v_kv, ones128], axis=1)
        ps, ms = [], []
        for g in range(GQA_GROUP):
            s = s_all[g * BLOCK:(g + 1) * BLOCK] + bias
            m = jnp.max(s, axis=-1, keepdims=True)
            if has_sink:
                m = jnp.maximum(m, sink_ref[h0 + g] * LOG2E)
            ps.append(jnp.exp2(s - m).astype(jnp.bfloat16))
            ms.append(m)
        acc_e = jnp.dot(jnp.concatenate(ps[0::2], axis=0), v_even,
                        preferred_element_type=jnp.float32)
        acc_o = jnp.dot(jnp.concatenate(ps[1::2], axis=0), v_odd,
                        preferred_element_type=jnp.float32)
        for t in range(GQA_GROUP // 2):
            h = h0 + 2 * t
            rows = slice(t * BLOCK, (t + 1) * BLOCK)
            num = jnp.where(low, acc_e[rows, :LANES], acc_o[rows, :LANES])
            den = jnp.where(low, acc_e[rows, LANES:], acc_o[rows, LANES:])
            m2 = jnp.where(low, jnp.broadcast_to(ms[2 * t], (BLOCK, LANES)),
                           jnp.broadcast_to(ms[2 * t + 1], (BLOCK, LANES)))
            if has_sink:
                sink2 = jnp.where(low, sink_ref[h] * LOG2E, sink_ref[h + 1] * LOG2E)
                den = den + jnp.exp2(sink2 - m2)
            cols = slice(h * HEAD_DIM, (h + 2) * HEAD_DIM)
            o_ref[out_rows, cols] = (num / den).astype(o_ref.dtype)
            if not has_sink:
                lse_ref[out_rows, cols] = m2 * (1.0 / LOG2E) + jnp.log(den)


def _band_attention(h, max_j, dilation, sinks=None):
    r, l, _ = h.shape
    n_qb = min(BAND_Q_BLOCKS, l // BLOCK)
    tq = n_qb * BLOCK
    assert l % tq == 0
    kcol, vcol = Q_DIM // KV_DIM, Q_DIM // KV_DIM + 1
    has_sink = sinks is not None
    rows = GQA_GROUP * BLOCK
    prev = lambda s, i, col: (s, jnp.maximum(i * n_qb - 1, 0), col)
    in_specs = [pl.BlockSpec((None, tq, Q_DIM), lambda s, i: (s, i, 0)),
                pl.BlockSpec((None, BLOCK, KV_DIM), lambda s, i: prev(s, i, kcol)),
                pl.BlockSpec((None, tq, KV_DIM), lambda s, i: (s, i, kcol)),
                pl.BlockSpec((None, BLOCK, KV_DIM), lambda s, i: prev(s, i, vcol)),
                pl.BlockSpec((None, tq, KV_DIM), lambda s, i: (s, i, vcol)),
                pl.BlockSpec((N_KV, rows, LANES), lambda s, i: (0, 0, 0)),
                pl.BlockSpec((2 * BLOCK, LANES), lambda s, i: (0, 0))]
    o_spec = pl.BlockSpec((None, tq, Q_DIM), lambda s, i: (s, i, 0))
    args = [h, h, h, h, h, _band_q_columns(dilation), _band_k_columns()]
    if has_sink:
        in_specs = [pl.BlockSpec(memory_space=pltpu.SMEM)] + in_specs
        args = [sinks.astype(jnp.float32)] + args
        out_specs = o_spec
        out_shape = jax.ShapeDtypeStruct((r, l, Q_DIM), jnp.bfloat16)
    else:
        out_specs = [o_spec, o_spec]
        out_shape = [jax.ShapeDtypeStruct((r, l, Q_DIM), jnp.bfloat16),
                     jax.ShapeDtypeStruct((r, l, Q_DIM), jnp.float32)]
    return pl.pallas_call(
        functools.partial(_band_kernel, max_j=max_j, has_sink=has_sink, ahead=BAND_AHEAD),
        grid=(r, l // tq),
        in_specs=in_specs,
        out_specs=out_specs,
        out_shape=out_shape,
        compiler_params=_params("parallel", "parallel"),
        name="band_attention",
    )(*args)


def _merge_kernel(*refs, dils):
    n_g = len(dils)
    o_refs, l_refs, out_ref, bufs = refs[:n_g], refs[n_g:2 * n_g], refs[2 * n_g], list(refs[2 * n_g + 1:])

    def natural(ref, d):
        if d == 1:
            return ref[0].astype(jnp.float32)
        buf = bufs.pop()
        per = ref.shape[1]
        for c in range(buf.shape[0]):
            for r in range(d):
                buf[c, pl.ds(r, per, stride=d), :] = ref[r, :, c * LANES:(c + 1) * LANES].astype(jnp.float32)
        return jnp.concatenate([buf[c] for c in range(buf.shape[0])], axis=1)

    os = [natural(ref, d) for ref, d in zip(o_refs, dils)]
    ls = [natural(ref, d) for ref, d in zip(l_refs, dils)]
    m = functools.reduce(jnp.maximum, ls)
    es = [jnp.exp(l - m) for l in ls]
    num = sum(e * o for e, o in zip(es, os))
    out_ref[...] = (num / sum(es)).astype(out_ref.dtype)


def _merge_groups(outs, lses, dils, tm):
    b, _, s, n = outs[0].shape
    tm = min(tm, s)
    assert s % tm == 0 and all(tm % (8 * d) == 0 for d in dils)
    per_batch = s // tm
    specs = [pl.BlockSpec((None, d, tm // d, n), lambda i: (i // per_batch, 0, i % per_batch, 0))
             for d in dils]
    n_buf = 2 * sum(d > 1 for d in dils)
    return pl.pallas_call(
        functools.partial(_merge_kernel, dils=tuple(dils)),
        grid=(b * per_batch,),
        in_specs=specs + specs,
        out_specs=pl.BlockSpec((tm, n), lambda i: (i, 0)),
        out_shape=jax.ShapeDtypeStruct((b * s, n), jnp.bfloat16),
        scratch_shapes=[pltpu.VMEM((n // LANES, tm, LANES), jnp.float32)] * n_buf,
        compiler_params=_params("parallel"),
        name="merge_groups",
    )(*outs, *lses)


def _count_rows(n_tiles, tile_hits, tq, tk):
    rows = 32
    def body(t, acc):
        hit = tile_hits(pl.multiple_of(t * tk, tk)).astype(jnp.float32)
        return acc + hit.reshape(tk // rows, rows, tq).sum(axis=0)
    acc = lax.fori_loop(0, n_tiles, body, jnp.zeros((rows, tq), jnp.float32))
    return jnp.sum(acc, axis=0, keepdims=True)


def _dsa_select_kernel(qia_ref, qib_ref, ki_ref, wi_ref, bias_ref, key_ref, *, top_k, tq, tk):
    i = pl.program_id(1)
    s_len = key_ref.shape[0]
    n_t = ((i + 1) * tq + tk - 1) // tk
    qpos = lax.broadcasted_iota(jnp.int32, (1, tq), 1) + i * tq
    kloc = lax.broadcasted_iota(jnp.int32, (tk, tq), 0)

    half = IDX_HEADS // 2
    q_t = [ref[...].astype(jnp.float32).T.astype(jnp.bfloat16) for ref in (qia_ref, qib_ref)]
    hpd = SELECT_DOT_COLS // tq
    q_cat = [jnp.concatenate([q[h * IDX_DIM:(h + 1) * IDX_DIM, :] for h in range(h0, h0 + hpd)], axis=1)
             for q in q_t for h0 in range(0, half, hpd)]
    w_t = wi_ref[...].T

    def score_body(t, carry):
        start = pl.multiple_of(t * tk, tk)
        ki = ki_ref[pl.ds(start, tk), :][:, :IDX_DIM].astype(jnp.bfloat16)
        rels = [jnp.dot(ki, q, preferred_element_type=jnp.float32) for q in q_cat]
        sc = jnp.zeros((tk, tq), jnp.float32)
        for d, rel in enumerate(rels):
            for g in range(hpd):
                row = IDX_DIM + d * hpd + g
                sc = sc + jnp.maximum(rel[:, g * tq:(g + 1) * tq], 0.0) * w_t[row:row + 1, :]
        sc = sc + 0.0
        bits = pltpu.bitcast(sc, jnp.int32)
        keys = jnp.where(bits < 0, bits ^ jnp.int32(0x7FFFFFFF), bits)
        key_ref[pl.ds(start, tk), :] = jnp.where(kloc + start <= qpos, keys, jnp.int32(INT_MIN))
        return carry

    lax.fori_loop(0, n_t, score_body, 0)

    kf = jnp.float32(top_k)

    def count_ge(cand):
        return _count_rows(n_t, lambda st: key_ref[pl.ds(st, tk), :] >= cand, tq, tk)

    short = (qpos + 1 < top_k).astype(jnp.float32)

    def bit_cond(c):
        b, _, done, _ = c
        return (b < 32) & (jnp.min(done) < 0.5)

    def bit_body(c):
        b, prefix, done, thr = c
        cand = prefix ^ lax.shift_left(jnp.int32(1), jnp.int32(31) - b)
        cnt = count_ge(cand)
        prefix = jnp.where(cnt >= kf, cand, prefix)
        hit = (cnt == kf) & (done < 0.5)
        thr = jnp.where(hit, cand - 1, thr)
        done = jnp.where(hit, 1.0, done)
        return b + 1, prefix, done, thr

    init = (jnp.int32(0), jnp.full((1, tq), INT_MIN, jnp.int32), short,
            jnp.full((1, tq), INT_MIN, jnp.int32))
    _, prefix, done, thr = lax.while_loop(bit_cond, bit_body, init)
    is_done = done > 0.5
    thr = jnp.where(is_done, thr, prefix)

    def tie_search(_):
        n_gt = _count_rows(n_t, lambda st: key_ref[pl.ds(st, tk), :] > thr, tq, tk)
        need = kf - n_gt
        n_idx_bits = max(1, int(math.ceil(math.log2(s_len))))

        def idx_body(b, p):
            cand = p | lax.shift_left(jnp.int32(1), jnp.int32(n_idx_bits - 1) - b)
            below = _count_rows(
                n_t, lambda st: (key_ref[pl.ds(st, tk), :] == thr) & (kloc + st < cand), tq, tk)
            return jnp.where(below <= need - 1.0, cand, p)

        return lax.fori_loop(0, n_idx_bits, idx_body, jnp.zeros((1, tq), jnp.int32))

    last_tie = lax.cond(jnp.min(done) < 0.5, tie_search,
                        lambda _: jnp.full((1, tq), -1, jnp.int32), 0)
    last_tie = jnp.where(is_done, -1, last_tie)

    def write_body(t, carry):
        start = pl.multiple_of(t * tk, tk)
        kt = key_ref[pl.ds(start, tk), :]
        pos = kloc + start
        sel = ((kt > thr) | ((kt == thr) & (pos <= last_tie))) & (pos <= qpos)
        bias_ref[:, pl.ds(start, tk)] = jnp.where(sel, 0.0, MASKED).T.astype(bias_ref.dtype)
        return carry

    lax.fori_loop(0, n_t, write_body, 0)

    def fill_body(t, carry):
        bias_ref[:, pl.ds(pl.multiple_of(t * tk, tk), tk)] = jnp.full((tq, tk), MASKED, bias_ref.dtype)
        return carry

    lax.fori_loop(n_t, s_len // tk, fill_body, 0)


def _dsa_select(hm, kiwi, tq, tk):
    b, s, _ = hm.shape
    tq, tk = min(tq, s), min(tk, s)
    assert s % tk == 0 and s % tq == 0
    top_k = min(TOPK_MAX, s // 4)
    qi_half = IDX_HEADS * IDX_DIM // 2
    qicol = GRP_IN // qi_half
    return pl.pallas_call(
        functools.partial(_dsa_select_kernel, top_k=top_k, tq=tq, tk=tk),
        grid=(b, s // tq),
        in_specs=[pl.BlockSpec((None, tq, qi_half), lambda bb, i: (bb, i, qicol)),
                  pl.BlockSpec((None, tq, qi_half), lambda bb, i: (bb, i, qicol + 1)),
                  pl.BlockSpec((None, s, LANES), lambda bb, i: (bb, 0, 0)),
                  pl.BlockSpec((None, tq, LANES), lambda bb, i: (bb, i, 0))],
        out_specs=pl.BlockSpec((None, tq, s), lambda bb, i: (bb, i, 0)),
        out_shape=jax.ShapeDtypeStruct((b, s, s), jnp.bfloat16),
        scratch_shapes=[pltpu.VMEM((s, tq), jnp.int32)],
        compiler_params=_params("parallel", "parallel"),
        name="dsa_select",
    )(hm, hm, kiwi, kiwi)


def _dsa_attn_kernel(q_ref, k_ref, v_ref, bias_ref, sl_ref, o_ref,
                     kaug_ref, vaug_ref, m_ref, acc_ref, *, tk, chunk):
    i = pl.program_id(1)
    s_len = k_ref.shape[0]

    @pl.when(i == 0)
    def _():
        lane = lax.broadcasted_iota(jnp.int32, (chunk, HEAD_DIM), 1)
        ones_col = (lane == 0).astype(jnp.bfloat16)
        for c in range(s_len // chunk):
            pos = lax.broadcasted_iota(jnp.int32, (chunk, HEAD_DIM), 0) + c * chunk
            hi = pos - (pos & (POS_SPLIT - 1))
            lo = pos & (POS_SPLIT - 1)
            pcols = jnp.where(lane < N_SLOPE_PIECES, hi, jnp.where(lane < 2 * N_SLOPE_PIECES, lo, 0))
            pcols = pcols.astype(jnp.float32).astype(jnp.bfloat16)
            rows = pl.ds(c * chunk, chunk)
            for kv in range(N_KV):
                kaug_ref[kv, rows, :] = jnp.concatenate(
                    [k_ref[rows, kv * HEAD_DIM:(kv + 1) * HEAD_DIM], pcols], axis=1)
                vaug_ref[kv, rows, :] = jnp.concatenate(
                    [v_ref[rows, kv * HEAD_DIM:(kv + 1) * HEAD_DIM], ones_col], axis=1)

    n_keys = (i + 1) * BLOCK
    m_ref[...] = jnp.full(m_ref.shape, MASKED, jnp.float32)
    acc_ref[...] = jnp.zeros(acc_ref.shape, jnp.float32)
    qa = [jnp.concatenate(
        [jnp.concatenate(
            [q_ref[:, (kv * GQA_GROUP + g) * HEAD_DIM:(kv * GQA_GROUP + g + 1) * HEAD_DIM]
             for g in range(GQA_GROUP)], axis=0), sl_ref[kv][:, HEAD_DIM:]], axis=1)
        for kv in range(N_KV)]

    def key_tile(t, width):
        start = pl.multiple_of(t * width, width)
        bias = bias_ref[:, pl.ds(start, width)].astype(jnp.float32)
        scores = [lax.dot_general(qa[kv], kaug_ref[kv, pl.ds(start, width), :], (((1,), (1,)), ((), ())),
                                  preferred_element_type=jnp.float32) for kv in range(N_KV)]
        for kv in range(N_KV):
            s_all = scores[kv]
            m_old = m_ref[kv]
            ps, m_news = [], []
            for g in range(GQA_GROUP):
                s = s_all[g * BLOCK:(g + 1) * BLOCK] + bias
                m_new = jnp.maximum(m_old[g * BLOCK:(g + 1) * BLOCK],
                                    jnp.max(s, axis=-1, keepdims=True))
                ps.append(jnp.exp2(s - jnp.concatenate([m_new] * (width // LANES), axis=1))
                          .astype(jnp.bfloat16))
                m_news.append(m_new)
            m_new = jnp.concatenate(m_news, axis=0)
            pv = jnp.dot(jnp.concatenate(ps, axis=0), vaug_ref[kv, pl.ds(start, width), :],
                         preferred_element_type=jnp.float32)
            acc_ref[kv] = jnp.exp2(m_old - m_new) * acc_ref[kv] + pv
            m_ref[kv] = m_new

    def full_tile(t, carry):
        key_tile(t, tk)
        return carry

    n_full = n_keys // tk
    rest = n_keys - n_full * tk
    lax.fori_loop(0, n_full, full_tile, 0)
    half = tk // 2

    @pl.when(rest > half)
    def _():
        key_tile(n_full, tk)

    @pl.when((rest > 0) & (rest <= half))
    def _():
        key_tile(2 * n_full, half)

    for kv in range(N_KV):
        acc = acc_ref[kv]
        o_all = (acc[:, :HEAD_DIM] / acc[:, HEAD_DIM:HEAD_DIM + 1]).astype(o_ref.dtype)
        for g in range(0, GQA_GROUP, 2):
            h = kv * GQA_GROUP + g
            o_ref[:, h * HEAD_DIM:(h + 2) * HEAD_DIM] = jnp.concatenate(
                [o_all[g * BLOCK:(g + 1) * BLOCK], o_all[(g + 1) * BLOCK:(g + 2) * BLOCK]], axis=1)


def _dsa_attention(hm, bias, tk):
    b, s, _ = hm.shape
    tk = min(tk, s)
    chunk = min(512, s)
    kcol = Q_DIM // KV_DIM
    rows = GQA_GROUP * BLOCK
    return pl.pallas_call(
        functools.partial(_dsa_attn_kernel, tk=tk, chunk=chunk),
        grid=(b, s // BLOCK),
        in_specs=[pl.BlockSpec((None, BLOCK, Q_DIM), lambda bb, i: (bb, i, 0)),
                  pl.BlockSpec((None, s, KV_DIM), lambda bb, i: (bb, 0, kcol)),
                  pl.BlockSpec((None, s, KV_DIM), lambda bb, i: (bb, 0, kcol + 1)),
                  pl.BlockSpec((None, BLOCK, s), lambda bb, i: (bb, i, 0)),
                  pl.BlockSpec((N_KV, rows, LANES), lambda bb, i: (0, 0, 0))],
        out_specs=pl.BlockSpec((None, BLOCK, Q_DIM), lambda bb, i: (bb, i, 0)),
        out_shape=jax.ShapeDtypeStruct((b, s, Q_DIM), jnp.bfloat16),
        scratch_shapes=[pltpu.VMEM((N_KV, s, LANES), jnp.bfloat16),
                        pltpu.VMEM((N_KV, s, LANES), jnp.bfloat16),
                        pltpu.VMEM((N_KV, rows, LANES), jnp.float32),
                        pltpu.VMEM((N_KV, rows, LANES), jnp.float32)],
        compiler_params=_params("parallel", "arbitrary"),
        name="dsa_attention",
    )(hm, hm, hm, bias, _slope_columns(1, 2))


_QKV_SCALE = np.concatenate([np.full(Q_DIM, ATTN_SCALE * LOG2E), np.ones(2 * KV_DIM)])
_DSA_SCALE = np.concatenate([_QKV_SCALE, np.full(IDX_HEADS * IDX_DIM, IDX_DIM ** -0.5)])


def _swa_layer(xb, w_in, layer, sinks, b, s):
    h = _matmul(xb, w_in, layer, 0, _QKV_SCALE, jnp.bfloat16, 1024, 1280)
    o = _band_attention(h.reshape(b, s, GRP_IN), SWA_WINDOW - 1, 1, sinks)
    return o.reshape(b * s, Q_DIM)


def _dilated_layer(xb, w_in, layer, b, s):
    outs, lses, dils = [], [], []
    for g, (window, dil) in enumerate(DIL_PATTERNS):
        h = _matmul(xb, w_in, layer, g * GRP_IN, _QKV_SCALE, jnp.bfloat16, 1024, 1280, batch=b, dil=dil)
        o, lse = _band_attention(h.reshape(b * dil, s // dil, GRP_IN), window // dil, dil)
        outs.append(o.reshape(b, dil, s // dil, Q_DIM))
        lses.append(lse.reshape(b, dil, s // dil, Q_DIM))
        dils.append(dil)
    return _merge_groups(outs, lses, dils, 256)


def _dsa_layer(xb, w_in, layer, b, s):
    hm = _matmul(xb, w_in, layer, 0, _DSA_SCALE, jnp.bfloat16, 2048, 512).reshape(b, s, C_MAIN)
    w_small = jnp.concatenate([w_in[layer, :, C_MAIN:C_MAIN + IDX_DIM],
                               w_in[layer, :, C_MAIN + IDX_DIM:] * IDX_HEADS ** -0.5,
                               jnp.zeros((D_MODEL, LANES - IDX_DIM - IDX_HEADS), w_in.dtype)], axis=1)
    kiwi = _matmul(xb, w_small[None], 0, 0, np.ones(LANES), jnp.float32, 1024, LANES).reshape(b, s, LANES)
    bias = _dsa_select(hm, kiwi, 512, 512)
    o = _dsa_attention(hm, bias, 512)
    return o.reshape(b * s, Q_DIM)


def kernel(x, a_w_in, a_sinks, a_w_out, b_w_in, b_w_out, c_w_in, c_w_out, ln_g, ln_b,
           ffn_w_gate_up, ffn_w_down):
    b, s, d = x.shape
    xf = x.reshape(b * s, d)
    xb = xf
    for i in range(DEPTH):
        kind, j = i % N_MIXERS, i // N_MIXERS
        if kind == 0:
            o = _swa_layer(xb, a_w_in, j, a_sinks[j], b, s)
            w_out = a_w_out
        elif kind == 1:
            o = _dilated_layer(xb, b_w_in, j, b, s)
            w_out = b_w_out
        else:
            o = _dsa_layer(xb, c_w_in, j, b, s)
            w_out = c_w_out
        xf, xb = _proj_residual_ln(o, w_out, j, xf, ln_g[i, 0], ln_b[i, 0], 512, 512, 256)
        hmid = _ffn_gate_up(xb, ffn_w_gate_up, i, 1024, 512)
        xf, xb = _proj_residual_ln(hmid, ffn_w_down, i, xf, ln_g[i, 1], ln_b[i, 1], 256, 512, 256)
    return xf.reshape(b, s, d)
```

```python
import functools
import math

import numpy as np
import jax
import jax.numpy as jnp
from jax import lax
from jax.experimental import pallas as pl
from jax.experimental.pallas import tpu as pltpu

D_MODEL = 2048
DEPTH = 4
N_MIXERS = 3
HEAD_DIM = 64
N_HEADS = D_MODEL // HEAD_DIM
N_KV = N_HEADS // 8
GQA_GROUP = N_HEADS // N_KV
Q_DIM = N_HEADS * HEAD_DIM
KV_DIM = N_KV * HEAD_DIM
ATTN_SCALE = HEAD_DIM ** -0.5
BLOCK = 128
SWA_WINDOW = 128
DIL_PATTERNS = ((128, 1), (512, 4), (2048, 16))
N_DIL = len(DIL_PATTERNS)
IDX_HEADS = 16
IDX_DIM = 64
TOPK_MAX = 256
D_FF = 256 * math.ceil(8 * D_MODEL / (3 * 256))
DEEPNORM_ALPHA = (2 * DEPTH) ** 0.25
LN_EPS = 1e-5
GRP_IN = Q_DIM + 2 * KV_DIM
C_MAIN = Q_DIM + 2 * KV_DIM + IDX_HEADS * IDX_DIM
C_IN = C_MAIN + IDX_DIM + IDX_HEADS
LANES = 128
MASKED = -1e30
INT_MIN = -2 ** 31
LOG2E = 1.4426950408889634
VMEM_LIMIT = 56 * 1024 * 1024

_SLOPES = [float(np.float32(2.0 ** (-8.0 * (i + 1) / N_HEADS))) for i in range(N_HEADS)]
N_SLOPE_PIECES = 3
POS_SPLIT = 64
BAND_Q_BLOCKS = 1
BAND_AHEAD = 4
SELECT_DOT_COLS = 2048


def _params(*sem):
    return pltpu.CompilerParams(dimension_semantics=sem, vmem_limit_bytes=VMEM_LIMIT)


def _mm_kernel(x_ref, w_ref, sc_ref, o_ref, wb_ref, *acc, dil):
    @pl.when(pl.program_id(1) == 0)
    def _():
        wb_ref[...] = (w_ref[...] * sc_ref[...]).astype(jnp.bfloat16)

    y = jnp.dot(x_ref[...].astype(jnp.bfloat16), wb_ref[...], preferred_element_type=jnp.float32)
    if dil == 1:
        o_ref[...] = y.astype(o_ref.dtype)
    else:
        acc_ref, = acc
        per = acc_ref.shape[1] // dil
        for c in range(acc_ref.shape[0]):
            cols = slice(c * LANES, (c + 1) * LANES)
            acc_ref[c] = y[:, cols]
            for r in range(dil):
                o_ref[r, :, cols] = acc_ref[c, pl.ds(r, per, stride=dil), :].astype(o_ref.dtype)


def _matmul(x, w, layer, col0, scale, out_dtype, tm, tn, batch=1, dil=1):
    m, k = x.shape
    n = scale.shape[0]
    tm, tn = min(tm, m // batch), min(tn, n)
    assert m % (batch * tm) == 0 and n % tn == 0 and col0 % tn == 0 and tm % (16 * dil) == 0
    j0 = col0 // tn
    if dil == 1:
        out_specs = pl.BlockSpec((tm, tn), lambda j, i: (i, j))
        out_shape = jax.ShapeDtypeStruct((m, n), out_dtype)
        scratch = []
    else:
        per_batch = m // batch // tm
        out_specs = pl.BlockSpec((None, dil, tm // dil, tn),
                                 lambda j, i: (i // per_batch, 0, i % per_batch, j))
        out_shape = jax.ShapeDtypeStruct((batch, dil, m // batch // dil, n), out_dtype)
        scratch = [pltpu.VMEM((tn // LANES, tm, LANES), jnp.float32)]
    return pl.pallas_call(
        functools.partial(_mm_kernel, dil=dil),
        grid=(n // tn, m // tm),
        in_specs=[pl.BlockSpec((tm, k), lambda j, i: (i, 0)),
                  pl.BlockSpec((None, k, tn), lambda j, i: (layer, 0, j0 + j)),
                  pl.BlockSpec((1, tn), lambda j, i: (0, j))],
        out_specs=out_specs,
        out_shape=out_shape,
        scratch_shapes=[pltpu.VMEM((k, tn), jnp.bfloat16)] + scratch,
        compiler_params=_params("parallel", "arbitrary"),
        name="proj_matmul",
    )(x, w, jnp.asarray(scale, jnp.float32).reshape(1, n))


def _gate_up_kernel(x_ref, wg_ref, wu_ref, o_ref, wgb_ref, wub_ref):
    @pl.when(pl.program_id(1) == 0)
    def _():
        wgb_ref[...] = wg_ref[...].astype(jnp.bfloat16)
        wub_ref[...] = wu_ref[...].astype(jnp.bfloat16)

    x = x_ref[...]
    g = jnp.dot(x, wgb_ref[...], preferred_element_type=jnp.float32)
    u = jnp.dot(x, wub_ref[...], preferred_element_type=jnp.float32)
    o_ref[...] = (g * (1.0 / (1.0 + jnp.exp(-g))) * u).astype(o_ref.dtype)


def _ffn_gate_up(x, w, layer, tm, tn):
    m, k = x.shape
    d_ff = w.shape[2] // 2
    tm = min(tm, m)
    assert m % tm == 0 and d_ff % tn == 0
    nj = d_ff // tn
    return pl.pallas_call(
        _gate_up_kernel,
        grid=(nj, m // tm),
        in_specs=[pl.BlockSpec((tm, k), lambda j, i: (i, 0)),
                  pl.BlockSpec((None, k, tn), lambda j, i: (layer, 0, j)),
                  pl.BlockSpec((None, k, tn), lambda j, i: (layer, 0, j + nj))],
        out_specs=pl.BlockSpec((tm, tn), lambda j, i: (i, j)),
        out_shape=jax.ShapeDtypeStruct((m, d_ff), jnp.bfloat16),
        scratch_shapes=[pltpu.VMEM((k, tn), jnp.bfloat16), pltpu.VMEM((k, tn), jnp.bfloat16)],
        compiler_params=_params("parallel", "arbitrary"),
        name="ffn_gate_up",
    )(x, w, w)


def _proj_ln_kernel(a_ref, w_ref, r_ref, g_ref, b_ref, o_ref, ob_ref, wb_ref, *, n_w, tkc, chunk):
    s = pl.program_id(0)

    @pl.when(s < n_w)
    def _():
        wb_ref[pl.ds(pl.multiple_of(s * tkc, tkc), tkc), :] = w_ref[...].astype(jnp.bfloat16)

    @pl.when(s >= n_w)
    def _():
        pieces = [slice(c, c + chunk) for c in range(0, a_ref.shape[0], chunk)]
        ys = [jnp.dot(a_ref[rows, :], wb_ref[...], preferred_element_type=jnp.float32) for rows in pieces]
        for rows, y in zip(pieces, ys):
            z = DEEPNORM_ALPHA * r_ref[rows, :] + y
            mu = jnp.mean(z, axis=-1, keepdims=True)
            zc = z - mu
            var = jnp.mean(zc * zc, axis=-1, keepdims=True)
            out = zc * lax.rsqrt(var + LN_EPS) * g_ref[...] + b_ref[...]
            o_ref[rows, :] = out
            ob_ref[rows, :] = out.astype(jnp.bfloat16)


def _proj_residual_ln(a, w, layer, resid, g, b, tm, tkc, chunk):
    m, k = a.shape
    n = w.shape[2]
    tm, tkc = min(tm, m), min(tkc, k)
    chunk = min(chunk, tm)
    assert m % tm == 0 and k % tkc == 0 and tm % chunk == 0
    n_w = k // tkc
    row = lambda s: (jnp.maximum(s - n_w, 0), 0)
    return pl.pallas_call(
        functools.partial(_proj_ln_kernel, n_w=n_w, tkc=tkc, chunk=chunk),
        grid=(n_w + m // tm,),
        in_specs=[pl.BlockSpec((tm, k), row),
                  pl.BlockSpec((None, tkc, n), lambda s: (layer, jnp.minimum(s, n_w - 1), 0)),
                  pl.BlockSpec((tm, n), row),
                  pl.BlockSpec((1, n), lambda s: (0, 0)),
                  pl.BlockSpec((1, n), lambda s: (0, 0))],
        out_specs=[pl.BlockSpec((tm, n), row), pl.BlockSpec((tm, n), row)],
        out_shape=[jax.ShapeDtypeStruct((m, n), jnp.float32),
                   jax.ShapeDtypeStruct((m, n), jnp.bfloat16)],
        scratch_shapes=[pltpu.VMEM((k, n), jnp.bfloat16)],
        compiler_params=_params("arbitrary"),
        name="proj_residual_ln",
    )(a, w, resid, g.reshape(1, n), b.reshape(1, n))


def _bf16_pieces(x, n):
    rest = np.asarray(x, np.float64)
    pieces = []
    for _ in range(n):
        p = rest.astype(np.float32).astype(jnp.bfloat16).astype(np.float64)
        pieces.append(p)
        rest = rest - p
    return pieces


def _slope_columns(mult, n_parts):
    out = np.zeros((N_KV, GQA_GROUP * BLOCK, LANES), np.float32)
    for h in range(N_HEADS):
        pieces = [float(p) for p in _bf16_pieces(np.float64(_SLOPES[h]) * mult * LOG2E, N_SLOPE_PIECES)]
        kv, g = divmod(h, GQA_GROUP)
        out[kv, g * BLOCK:(g + 1) * BLOCK, HEAD_DIM:HEAD_DIM + n_parts * N_SLOPE_PIECES] = np.asarray(
            pieces * n_parts, np.float32)
    return jnp.asarray(out, jnp.bfloat16)


def _band_q_columns(dilation):
    out = np.zeros((N_KV, GQA_GROUP * BLOCK, LANES), np.float32)
    qpos = BLOCK + np.arange(BLOCK, dtype=np.float64)
    for h in range(N_HEADS):
        s2 = np.float64(_SLOPES[h]) * dilation * LOG2E
        kv, g = divmod(h, GQA_GROUP)
        rows = slice(g * BLOCK, (g + 1) * BLOCK)
        for c, p in enumerate(_bf16_pieces(s2, N_SLOPE_PIECES)):
            out[kv, rows, HEAD_DIM + c] = p
        for c, p in enumerate(_bf16_pieces(-s2 * qpos, N_SLOPE_PIECES)):
            out[kv, rows, HEAD_DIM + N_SLOPE_PIECES + c] = p
    return jnp.asarray(out, jnp.bfloat16)


def _band_k_columns():
    out = np.zeros((2 * BLOCK, LANES), np.float32)
    out[:, HEAD_DIM:HEAD_DIM + N_SLOPE_PIECES] = np.arange(2 * BLOCK, dtype=np.float32)[:, None]
    out[:, HEAD_DIM + N_SLOPE_PIECES:HEAD_DIM + 2 * N_SLOPE_PIECES] = 1.0
    return jnp.asarray(out, jnp.bfloat16)


def _band_kernel(*refs, max_j, has_sink, ahead):
    if has_sink:
        sink_ref, q_ref, kp_ref, kc_ref, vp_ref, vc_ref, qc_ref, kc_cols_ref, o_ref = refs
    else:
        q_ref, kp_ref, kc_ref, vp_ref, vc_ref, qc_ref, kc_cols_ref, o_ref, lse_ref = refs
    i = pl.program_id(1)
    n_qb = q_ref.shape[0] // BLOCK
    k = jnp.concatenate([kp_ref[...], kc_ref[...]], axis=0)
    v = jnp.concatenate([vp_ref[...], vc_ref[...]], axis=0)
    qpos = lax.broadcasted_iota(jnp.int32, (BLOCK, 2 * BLOCK), 0) + BLOCK
    kpos = lax.broadcasted_iota(jnp.int32, (BLOCK, 2 * BLOCK), 1)
    j = qpos - kpos
    in_band = (j >= 0) & (j <= max_j)
    bias_rest = jnp.where(in_band, 0.0, MASKED)
    bias_first = jnp.where(in_band & ((kpos >= BLOCK) | (i > 0)), 0.0, MASKED)
    low = lax.broadcasted_iota(jnp.int32, (BLOCK, LANES), 1) < HEAD_DIM
    k_cols = kc_cols_ref[:, HEAD_DIM:]
    ones64 = jnp.ones((2 * BLOCK, HEAD_DIM), jnp.bfloat16)
    ones128 = jnp.ones((2 * BLOCK, LANES), jnp.bfloat16)
    units = [(b, kv) for b in range(n_qb) for kv in range(N_KV)]

    def scores(b, kv):
        ka = jnp.concatenate([k[b * BLOCK:(b + 2) * BLOCK, kv * HEAD_DIM:(kv + 1) * HEAD_DIM], k_cols],
                             axis=1)
        qa = jnp.concatenate(
            [jnp.concatenate([q_ref[b * BLOCK:(b + 1) * BLOCK,
                                    (kv * GQA_GROUP + g) * HEAD_DIM:(kv * GQA_GROUP + g + 1) * HEAD_DIM]
                              for g in range(GQA_GROUP)], axis=0),
             qc_ref[kv][:, HEAD_DIM:]], axis=1)
        return lax.dot_general(qa, ka, (((1,), (1,)), ((), ())),
                               preferred_element_type=jnp.float32)

    pending = [scores(*units[u]) for u in range(min(ahead, len(units)))]
    for u, (b, kv) in enumerate(units):
        s_all = pending.pop(0)
        if u + ahead < len(units):
            pending.append(scores(*units[u + ahead]))
        h0 = kv * GQA_GROUP
        bias = bias_first if b == 0 else bias_rest
        out_rows = slice(b * BLOCK, (b + 1) * BLOCK)
        v_kv = v[b * BLOCK:(b + 2) * BLOCK, kv * HEAD_DIM:(kv + 1) * HEAD_DIM]
        v_even = jnp.concatenate([v_kv, ones64, ones128], axis=1)
        v_odd = jnp.concatenate([ones64, v_kv, ones128], axis=1)
        ps, ms = [], []
        for g in range(GQA_GROUP):
            s = s_all[g * BLOCK:(g + 1) * BLOCK] + bias
            m = jnp.max(s, axis=-1, keepdims=True)
            if has_sink:
                m = jnp.maximum(m, sink_ref[h0 + g] * LOG2E)
            ps.append(jnp.exp2(s - m).astype(jnp.bfloat16))
            ms.append(m)
        acc_e = jnp.dot(jnp.concatenate(ps[0::2], axis=0), v_even,
                        preferred_element_type=jnp.float32)
        acc_o = jnp.dot(jnp.concatenate(ps[1::2], axis=0), v_odd,
                        preferred_element_type=jnp.float32)
        for t in range(GQA_GROUP // 2):
            h = h0 + 2 * t
            rows = slice(t * BLOCK, (t + 1) * BLOCK)
            num = jnp.where(low, acc_e[rows, :LANES], acc_o[rows, :LANES])
            den = jnp.where(low, acc_e[rows, LANES:], acc_o[rows, LANES:])
            m2 = jnp.where(low, jnp.broadcast_to(ms[2 * t], (BLOCK, LANES)),
                           jnp.broadcast_to(ms[2 * t + 1], (BLOCK, LANES)))
            if has_sink:
                sink2 = jnp.where(low, sink_ref[h] * LOG2E, sink_ref[h + 1] * LOG2E)
                den = den + jnp.exp2(sink2 - m2)
            cols = slice(h * HEAD_DIM, (h + 2) * HEAD_DIM)
            o_ref[out_rows, cols] = (num / den).astype(o_ref.dtype)
            if not has_sink:
                lse_ref[out_rows, cols] = m2 * (1.0 / LOG2E) + jnp.log(den)


def _band_attention(h, max_j, dilation, sinks=None):
    r, l, _ = h.shape
    n_qb = min(BAND_Q_BLOCKS, l // BLOCK)
    tq = n_qb * BLOCK
    assert l % tq == 0
    kcol, vcol = Q_DIM // KV_DIM, Q_DIM // KV_DIM + 1
    has_sink = sinks is not None
    rows = GQA_GROUP * BLOCK
    prev = lambda s, i, col: (s, jnp.maximum(i * n_qb - 1, 0), col)
    in_specs = [pl.BlockSpec((None, tq, Q_DIM), lambda s, i: (s, i, 0)),
                pl.BlockSpec((None, BLOCK, KV_DIM), lambda s, i: prev(s, i, kcol)),
                pl.BlockSpec((None, tq, KV_DIM), lambda s, i: (s, i, kcol)),
                pl.BlockSpec((None, BLOCK, KV_DIM), lambda s, i: prev(s, i, vcol)),
                pl.BlockSpec((None, tq, KV_DIM), lambda s, i: (s, i, vcol)),
                pl.BlockSpec((N_KV, rows, LANES), lambda s, i: (0, 0, 0)),
                pl.BlockSpec((2 * BLOCK, LANES), lambda s, i: (0, 0))]
    o_spec = pl.BlockSpec((None, tq, Q_DIM), lambda s, i: (s, i, 0))
    args = [h, h, h, h, h, _band_q_columns(dilation), _band_k_columns()]
    if has_sink:
        in_specs = [pl.BlockSpec(memory_space=pltpu.SMEM)] + in_specs
        args = [sinks.astype(jnp.float32)] + args
        out_specs = o_spec
        out_shape = jax.ShapeDtypeStruct((r, l, Q_DIM), jnp.bfloat16)
    else:
        out_specs = [o_spec, o_spec]
        out_shape = [jax.ShapeDtypeStruct((r, l, Q_DIM), jnp.bfloat16),
                     jax.ShapeDtypeStruct((r, l, Q_DIM), jnp.float32)]
    return pl.pallas_call(
        functools.partial(_band_kernel, max_j=max_j, has_sink=has_sink, ahead=BAND_AHEAD),
        grid=(r, l // tq),
        in_specs=in_specs,
        out_specs=out_specs,
        out_shape=out_shape,
        compiler_params=_params("parallel", "parallel"),
        name="band_attention",
    )(*args)


def _merge_kernel(*refs, dils):
    n_g = len(dils)
    o_refs, l_refs, out_ref, bufs = refs[:n_g], refs[n_g:2 * n_g], refs[2 * n_g], list(refs[2 * n_g + 1:])

    def natural(ref, d):
        if d == 1:
            return ref[0].astype(jnp.float32)
        buf = bufs.pop()
        per = ref.shape[1]
        for c in range(buf.shape[0]):
            for r in range(d):
                buf[c, pl.ds(r, per, stride=d), :] = ref[r, :, c * LANES:(c + 1) * LANES].astype(jnp.float32)
        return jnp.concatenate([buf[c] for c in range(buf.shape[0])], axis=1)

    os = [natural(ref, d) for ref, d in zip(o_refs, dils)]
    ls = [natural(ref, d) for ref, d in zip(l_refs, dils)]
    m = functools.reduce(jnp.maximum, ls)
    es = [jnp.exp(l - m) for l in ls]
    num = sum(e * o for e, o in zip(es, os))
    out_ref[...] = (num / sum(es)).astype(out_ref.dtype)


def _merge_groups(outs, lses, dils, tm):
    b, _, s, n = outs[0].shape
    tm = min(tm, s)
    assert s % tm == 0 and all(tm % (8 * d) == 0 for d in dils)
    per_batch = s // tm
    specs = [pl.BlockSpec((None, d, tm // d, n), lambda i: (i // per_batch, 0, i % per_batch, 0))
             for d in dils]
    n_buf = 2 * sum(d > 1 for d in dils)
    return pl.pallas_call(
        functools.partial(_merge_kernel, dils=tuple(dils)),
        grid=(b * per_batch,),
        in_specs=specs + specs,
        out_specs=pl.BlockSpec((tm, n), lambda i: (i, 0)),
        out_shape=jax.ShapeDtypeStruct((b * s, n), jnp.bfloat16),
        scratch_shapes=[pltpu.VMEM((n // LANES, tm, LANES), jnp.float32)] * n_buf,
        compiler_params=_params("parallel"),
        name="merge_groups",
    )(*outs, *lses)


def _count_rows(n_tiles, tile_hits, tq, tk):
    rows = 32
    def body(t, acc):
        hit = tile_hits(pl.multiple_of(t * tk, tk)).astype(jnp.float32)
        return acc + hit.reshape(tk // rows, rows, tq).sum(axis=0)
    acc = lax.fori_loop(0, n_tiles, body, jnp.zeros((rows, tq), jnp.float32))
    return jnp.sum(acc, axis=0, keepdims=True)


def _dsa_select_kernel(qia_ref, qib_ref, ki_ref, wi_ref, bias_ref, key_ref, *, top_k, tq, tk):
    i = pl.program_id(1)
    s_len = key_ref.shape[0]
    n_t = ((i + 1) * tq + tk - 1) // tk
    qpos = lax.broadcasted_iota(jnp.int32, (1, tq), 1) + i * tq
    kloc = lax.broadcasted_iota(jnp.int32, (tk, tq), 0)

    half = IDX_HEADS // 2
    q_t = [ref[...].astype(jnp.float32).T.astype(jnp.bfloat16) for ref in (qia_ref, qib_ref)]
    hpd = SELECT_DOT_COLS // tq
    q_cat = [jnp.concatenate([q[h * IDX_DIM:(h + 1) * IDX_DIM, :] for h in range(h0, h0 + hpd)], axis=1)
             for q in q_t for h0 in range(0, half, hpd)]
    w_t = wi_ref[...].T

    def score_body(t, carry):
        start = pl.multiple_of(t * tk, tk)
        ki = ki_ref[pl.ds(start, tk), :][:, :IDX_DIM].astype(jnp.bfloat16)
        rels = [jnp.dot(ki, q, preferred_element_type=jnp.float32) for q in q_cat]
        sc = jnp.zeros((tk, tq), jnp.float32)
        for d, rel in enumerate(rels):
            for g in range(hpd):
                row = IDX_DIM + d * hpd + g
                sc = sc + jnp.maximum(rel[:, g * tq:(g + 1) * tq], 0.0) * w_t[row:row + 1, :]
        sc = sc + 0.0
        bits = pltpu.bitcast(sc, jnp.int32)
        keys = jnp.where(bits < 0, bits ^ jnp.int32(0x7FFFFFFF), bits)
        key_ref[pl.ds(start, tk), :] = jnp.where(kloc + start <= qpos, keys, jnp.int32(INT_MIN))
        return carry

    lax.fori_loop(0, n_t, score_body, 0)

    kf = jnp.float32(top_k)

    def count_ge(cand):
        return _count_rows(n_t, lambda st: key_ref[pl.ds(st, tk), :] >= cand, tq, tk)

    short = (qpos + 1 < top_k).astype(jnp.float32)

    def bit_cond(c):
        b, _, done, _ = c
        return (b < 32) & (jnp.min(done) < 0.5)

    def bit_body(c):
        b, prefix, done, thr = c
        cand = prefix ^ lax.shift_left(jnp.int32(1), jnp.int32(31) - b)
        cnt = count_ge(cand)
        prefix = jnp.where(cnt >= kf, cand, prefix)
        hit = (cnt == kf) & (done < 0.5)
        thr = jnp.where(hit, cand - 1, thr)
        done = jnp.where(hit, 1.0, done)
        return b + 1, prefix, done, thr

    init = (jnp.int32(0), jnp.full((1, tq), INT_MIN, jnp.int32), short,
            jnp.full((1, tq), INT_MIN, jnp.int32))
    _, prefix, done, thr = lax.while_loop(bit_cond, bit_body, init)
    is_done = done > 0.5
    thr = jnp.where(is_done, thr, prefix)

    def tie_search(_):
        n_gt = _count_rows(n_t, lambda st: key_ref[pl.ds(st, tk), :] > thr, tq, tk)
        need = kf - n_gt
        n_idx_bits = max(1, int(math.ceil(math.log2(s_len))))

        def idx_body(b, p):
            cand = p | lax.shift_left(jnp.int32(1), jnp.int32(n_idx_bits - 1) - b)
            below = _count_rows(
                n_t, lambda st: (key_ref[pl.ds(st, tk), :] == thr) & (kloc + st < cand), tq, tk)
            return jnp.where(below <= need - 1.0, cand, p)

        return lax.fori_loop(0, n_idx_bits, idx_body, jnp.zeros((1, tq), jnp.int32))

    last_tie = lax.cond(jnp.min(done) < 0.5, tie_search,
                        lambda _: jnp.full((1, tq), -1, jnp.int32), 0)
    last_tie = jnp.where(is_done, -1, last_tie)

    def write_body(t, carry):
        start = pl.multiple_of(t * tk, tk)
        kt = key_ref[pl.ds(start, tk), :]
        pos = kloc + start
        sel = ((kt > thr) | ((kt == thr) & (pos <= last_tie))) & (pos <= qpos)
        bias_ref[:, pl.ds(start, tk)] = jnp.where(sel, 0.0, MASKED).T.astype(bias_ref.dtype)
        return carry

    lax.fori_loop(0, n_t, write_body, 0)

    def fill_body(t, carry):
        bias_ref[:, pl.ds(pl.multiple_of(t * tk, tk), tk)] = jnp.full((tq, tk), MASKED, bias_ref.dtype)
        return carry

    lax.fori_loop(n_t, s_len // tk, fill_body, 0)


def _dsa_select(hm, kiwi, tq, tk):
    b, s, _ = hm.shape
    tq, tk = min(tq, s), min(tk, s)
    assert s % tk == 0 and s % tq == 0
    top_k = min(TOPK_MAX, s // 4)
    qi_half = IDX_HEADS * IDX_DIM // 2
    qicol = GRP_IN // qi_half
    return pl.pallas_call(
        functools.partial(_dsa_select_kernel, top_k=top_k, tq=tq, tk=tk),
        grid=(b, s // tq),
        in_specs=[pl.BlockSpec((None, tq, qi_half), lambda bb, i: (bb, i, qicol)),
                  pl.BlockSpec((None, tq, qi_half), lambda bb, i: (bb, i, qicol + 1)),
                  pl.BlockSpec((None, s, LANES), lambda bb, i: (bb, 0, 0)),
                  pl.BlockSpec((None, tq, LANES), lambda bb, i: (bb, i, 0))],
        out_specs=pl.BlockSpec((None, tq, s), lambda bb, i: (bb, i, 0)),
        out_shape=jax.ShapeDtypeStruct((b, s, s), jnp.bfloat16),
        scratch_shapes=[pltpu.VMEM((s, tq), jnp.int32)],
        compiler_params=_params("parallel", "parallel"),
        name="dsa_select",
    )(hm, hm, kiwi, kiwi)


def _dsa_attn_kernel(q_ref, k_ref, v_ref, bias_ref, sl_ref, o_ref,
                     kaug_ref, vaug_ref, m_ref, acc_ref, *, tk, chunk):
    i = pl.program_id(1)
    s_len = k_ref.shape[0]

    @pl.when(i == 0)
    def _():
        lane = lax.broadcasted_iota(jnp.int32, (chunk, HEAD_DIM), 1)
        ones_col = (lane == 0).astype(jnp.bfloat16)
        for c in range(s_len // chunk):
            pos = lax.broadcasted_iota(jnp.int32, (chunk, HEAD_DIM), 0) + c * chunk
            hi = pos - (pos & (POS_SPLIT - 1))
            lo = pos & (POS_SPLIT - 1)
            pcols = jnp.where(lane < N_SLOPE_PIECES, hi, jnp.where(lane < 2 * N_SLOPE_PIECES, lo, 0))
            pcols = pcols.astype(jnp.float32).astype(jnp.bfloat16)
            rows = pl.ds(c * chunk, chunk)
            for kv in range(N_KV):
                kaug_ref[kv, rows, :] = jnp.concatenate(
                    [k_ref[rows, kv * HEAD_DIM:(kv + 1) * HEAD_DIM], pcols], axis=1)
                vaug_ref[kv, rows, :] = jnp.concatenate(
                    [v_ref[rows, kv * HEAD_DIM:(kv + 1) * HEAD_DIM], ones_col], axis=1)

    n_keys = (i + 1) * BLOCK
    m_ref[...] = jnp.full(m_ref.shape, MASKED, jnp.float32)
    acc_ref[...] = jnp.zeros(acc_ref.shape, jnp.float32)
    qa = [jnp.concatenate(
        [jnp.concatenate(
            [q_ref[:, (kv * GQA_GROUP + g) * HEAD_DIM:(kv * GQA_GROUP + g + 1) * HEAD_DIM]
             for g in range(GQA_GROUP)], axis=0), sl_ref[kv][:, HEAD_DIM:]], axis=1)
        for kv in range(N_KV)]

    def key_tile(t, width):
        start = pl.multiple_of(t * width, width)
        bias = bias_ref[:, pl.ds(start, width)].astype(jnp.float32)
        scores = [lax.dot_general(qa[kv], kaug_ref[kv, pl.ds(start, width), :], (((1,), (1,)), ((), ())),
                                  preferred_element_type=jnp.float32) for kv in range(N_KV)]
        for kv in range(N_KV):
            s_all = scores[kv]
            m_old = m_ref[kv]
            ps, m_news = [], []
            for g in range(GQA_GROUP):
                s = s_all[g * BLOCK:(g + 1) * BLOCK] + bias
                m_new = jnp.maximum(m_old[g * BLOCK:(g + 1) * BLOCK],
                                    jnp.max(s, axis=-1, keepdims=True))
                ps.append(jnp.exp2(s - jnp.concatenate([m_new] * (width // LANES), axis=1))
                          .astype(jnp.bfloat16))
                m_news.append(m_new)
            m_new = jnp.concatenate(m_news, axis=0)
            pv = jnp.dot(jnp.concatenate(ps, axis=0), vaug_ref[kv, pl.ds(start, width), :],
                         preferred_element_type=jnp.float32)
            acc_ref[kv] = jnp.exp2(m_old - m_new) * acc_ref[kv] + pv
            m_ref[kv] = m_new

    def full_tile(t, carry):
        key_tile(t, tk)
        return carry

    n_full = n_keys // tk
    rest = n_keys - n_full * tk
    lax.fori_loop(0, n_full, full_tile, 0)
    widths = [tk >> e for e in range((tk // BLOCK).bit_length())]
    for width, covers_above in zip(widths, widths[1:] + [0]):
        @pl.when((rest > covers_above) & (rest <= width))
        def _(width=width):
            key_tile(n_full * (tk // width), width)

    for kv in range(N_KV):
        acc = acc_ref[kv]
        o_all = (acc[:, :HEAD_DIM] / acc[:, HEAD_DIM:HEAD_DIM + 1]).astype(o_ref.dtype)
        for g in range(0, GQA_GROUP, 2):
            h = kv * GQA_GROUP + g
            o_ref[:, h * HEAD_DIM:(h + 2) * HEAD_DIM] = jnp.concatenate(
                [o_all[g * BLOCK:(g + 1) * BLOCK], o_all[(g + 1) * BLOCK:(g + 2) * BLOCK]], axis=1)


def _dsa_attention(hm, bias, tk):
    b, s, _ = hm.shape
    tk = min(tk, s)
    assert tk % BLOCK == 0 and (tk // BLOCK) & (tk // BLOCK - 1) == 0
    chunk = min(512, s)
    kcol = Q_DIM // KV_DIM
    rows = GQA_GROUP * BLOCK
    return pl.pallas_call(
        functools.partial(_dsa_attn_kernel, tk=tk, chunk=chunk),
        grid=(b, s // BLOCK),
        in_specs=[pl.BlockSpec((None, BLOCK, Q_DIM), lambda bb, i: (bb, i, 0)),
                  pl.BlockSpec((None, s, KV_DIM), lambda bb, i: (bb, 0, kcol)),
                  pl.BlockSpec((None, s, KV_DIM), lambda bb, i: (bb, 0, kcol + 1)),
                  pl.BlockSpec((None, BLOCK, s), lambda bb, i: (bb, i, 0)),
                  pl.BlockSpec((N_KV, rows, LANES), lambda bb, i: (0, 0, 0))],
        out_specs=pl.BlockSpec((None, BLOCK, Q_DIM), lambda bb, i: (bb, i, 0)),
        out_shape=jax.ShapeDtypeStruct((b, s, Q_DIM), jnp.bfloat16),
        scratch_shapes=[pltpu.VMEM((N_KV, s, LANES), jnp.bfloat16),
                        pltpu.VMEM((N_KV, s, LANES), jnp.bfloat16),
                        pltpu.VMEM((N_KV, rows, LANES), jnp.float32),
                        pltpu.VMEM((N_KV, rows, LANES), jnp.float32)],
        compiler_params=_params("parallel", "arbitrary"),
        name="dsa_attention",
    )(hm, hm, hm, bias, _slope_columns(1, 2))


_QKV_SCALE = np.concatenate([np.full(Q_DIM, ATTN_SCALE * LOG2E), np.ones(2 * KV_DIM)])
_DSA_SCALE = np.concatenate([_QKV_SCALE, np.full(IDX_HEADS * IDX_DIM, IDX_DIM ** -0.5)])


def _swa_layer(xb, w_in, layer, sinks, b, s):
    h = _matmul(xb, w_in, layer, 0, _QKV_SCALE, jnp.bfloat16, 1024, 1280)
    o = _band_attention(h.reshape(b, s, GRP_IN), SWA_WINDOW - 1, 1, sinks)
    return o.reshape(b * s, Q_DIM)


def _dilated_layer(xb, w_in, layer, b, s):
    outs, lses, dils = [], [], []
    for g, (window, dil) in enumerate(DIL_PATTERNS):
        h = _matmul(xb, w_in, layer, g * GRP_IN, _QKV_SCALE, jnp.bfloat16, 1024, 1280, batch=b, dil=dil)
        o, lse = _band_attention(h.reshape(b * dil, s // dil, GRP_IN), window // dil, dil)
        outs.append(o.reshape(b, dil, s // dil, Q_DIM))
        lses.append(lse.reshape(b, dil, s // dil, Q_DIM))
        dils.append(dil)
    return _merge_groups(outs, lses, dils, 256)


def _dsa_layer(xb, w_in, layer, b, s):
    hm = _matmul(xb, w_in, layer, 0, _DSA_SCALE, jnp.bfloat16, 2048, 512).reshape(b, s, C_MAIN)
    w_small = jnp.concatenate([w_in[layer, :, C_MAIN:C_MAIN + IDX_DIM],
                               w_in[layer, :, C_MAIN + IDX_DIM:] * IDX_HEADS ** -0.5,
                               jnp.zeros((D_MODEL, LANES - IDX_DIM - IDX_HEADS), w_in.dtype)], axis=1)
    kiwi = _matmul(xb, w_small[None], 0, 0, np.ones(LANES), jnp.float32, 1024, LANES).reshape(b, s, LANES)
    bias = _dsa_select(hm, kiwi, 512, 512)
    o = _dsa_attention(hm, bias, 512)
    return o.reshape(b * s, Q_DIM)


def kernel(x, a_w_in, a_sinks, a_w_out, b_w_in, b_w_out, c_w_in, c_w_out, ln_g, ln_b,
           ffn_w_gate_up, ffn_w_down):
    b, s, d = x.shape
    xf = x.reshape(b * s, d)
    xb = xf
    for i in range(DEPTH):
        kind, j = i % N_MIXERS, i // N_MIXERS
        if kind == 0:
            o = _swa_layer(xb, a_w_in, j, a_sinks[j], b, s)
            w_out = a_w_out
        elif kind == 1:
            o = _dilated_layer(xb, b_w_in, j, b, s)
            w_out = b_w_out
        else:
            o = _dsa_layer(xb, c_w_in, j, b, s)
            w_out = c_w_out
        xf, xb = _proj_residual_ln(o, w_out, j, xf, ln_g[i, 0], ln_b[i, 0], 512, 512, 256)
        hmid = _ffn_gate_up(xb, ffn_w_gate_up, i, 1024, 512)
        xf, xb = _proj_residual_ln(hmid, ffn_w_down, i, xf, ln_g[i, 1], ln_b[i, 1], 256, 512, 256)
    return xf.reshape(b, s, d)
```
